```python
import jax, jax.numpy as jnp
from jax import lax
import numpy as np

D_MODEL = 1024
BATCH = 8
SEQ = 4096
DEPTH = 1

HEAD_DIM = 64
SSD_HEADS = 16
SSD_D_INNER = SSD_HEADS * HEAD_DIM
SSD_GROUPS = 4
SSD_STATE = 128
SSD_CONV = 4
SSD_CHUNK = 128
CONV_CH = SSD_D_INNER + 2 * SSD_GROUPS * SSD_STATE
ATT_HEADS = 16
ATT_D = ATT_HEADS * HEAD_DIM
DILATED_PATTERNS = ((128, 1), (512, 4), (2048, 16))
MIX_WIDTH = SSD_D_INNER + ATT_D
IN_PROJ_WIDTH = SSD_D_INNER + CONV_CH + SSD_HEADS + 3 * ATT_D
D_FF = 4 * D_MODEL
N_MOD = 6
EPS = 1e-6

kernel_name = 'hybrid_ssd_dilated_attn_block'


def _rms(t):
    tf = t.astype(jnp.float32)
    return tf * lax.rsqrt(jnp.mean(tf * tf, axis=-1, keepdims=True) + EPS)


def rms_norm(t, w):
    return (_rms(t) * w.astype(jnp.float32)).astype(t.dtype)


def causal_depthwise_conv(u, w, bias):
    k = w.shape[0]
    s = u.shape[1]
    up = jnp.pad(u, ((0, 0), (k - 1, 0), (0, 0)))
    out = bias
    for i in range(k):
        out = out + w[i] * up[:, i:i + s]
    return out


def ssd_chunked_scan(xs, dt, a, bm, cm):
    b, s, h, p = xs.shape
    g, n = bm.shape[2], bm.shape[3]
    e = h // g
    nc = s // SSD_CHUNK
    L = SSD_CHUNK
    xdt = (xs.astype(jnp.float32) * dt[..., None]).reshape(b, nc, L, g, e, p)
    bc = bm.reshape(b, nc, L, g, n)
    cc = cm.reshape(b, nc, L, g, n)
    a_cs = jnp.cumsum((dt * a).reshape(b, nc, L, g, e), axis=2)
    seg = a_cs[:, :, :, None] - a_cs[:, :, None, :]
    tril = jnp.tril(jnp.ones((L, L), dtype=bool))[None, None, :, :, None, None]
    lmat = jnp.exp(jnp.where(tril, seg, -jnp.inf))
    cb = jnp.einsum('bclgn,bcsgn->bclsg', cc, bc)
    y_diag = jnp.einsum('bclsge,bcsgep->bclgep', cb[..., None] * lmat, xdt)
    decay_states = jnp.exp(a_cs[:, :, -1:] - a_cs)
    states = jnp.einsum('bclgn,bclge,bclgep->bcgepn', bc, decay_states, xdt)
    chunk_decay = jnp.exp(a_cs[:, :, -1])

    def step(h_prev, inp):
        st, dec = inp
        return h_prev * dec[..., None, None] + st, h_prev

    init = jnp.zeros_like(states[:, 0])
    _, prev_states = lax.scan(step, init, (jnp.moveaxis(states, 1, 0), jnp.moveaxis(chunk_decay, 1, 0)))
    prev_states = jnp.moveaxis(prev_states, 0, 1)
    y_off = jnp.einsum('bclgn,bcgepn,bclge->bclgep', cc, prev_states, jnp.exp(a_cs))
    return (y_diag + y_off).reshape(b, s, h, p)


def dilated_window_attention(q, k, v, window, dilation):
    b, s, h, d = q.shape
    nw = window // dilation
    blk = nw
    L = s // dilation
    nb = -(-L // blk)
    Lp = nb * blk

    def to_classes(t):
        t = t.reshape(b, L, dilation, h, d).transpose(0, 2, 3, 1, 4)
        return jnp.pad(t, ((0, 0), (0, 0), (0, 0), (0, Lp - L), (0, 0)))

    def with_prev(t):
        tp = jnp.pad(t, ((0, 0), (0, 0), (0, 0), (blk, 0), (0, 0)))
        prev = tp[:, :, :, :Lp].reshape(b, dilation, h, nb, blk, d)
        cur = t.reshape(b, dilation, h, nb, blk, d)
        return jnp.concatenate([prev, cur], axis=-2)

    qb = to_classes(q).reshape(b, dilation, h, nb, blk, d)
    kb = with_prev(to_classes(k))
    vb = with_prev(to_classes(v))
    scores = jnp.einsum('brhiqd,brhikd->brhiqk', qb, kb).astype(jnp.float32)
    qi = jnp.arange(blk)[:, None]
    ki = jnp.arange(2 * blk)[None, :]
    dist = blk + qi - ki
    band = (dist >= 0) & (dist <= nw)
    valid = (jnp.arange(nb)[:, None, None] > 0) | (ki[None] >= blk)
    mask = band[None] & valid
    scores = jnp.where(mask, scores, -jnp.inf)
    m = jnp.max(scores, axis=-1, keepdims=True)
    pr = jnp.exp(scores - m)
    denom = jnp.sum(pr, axis=-1, keepdims=True)
    o = jnp.einsum('brhiqk,brhikd->brhiqd', pr, vb.astype(jnp.float32)) / denom
    lse = (m + jnp.log(denom))[..., 0]
    o = o.reshape(b, dilation, h, Lp, d)[:, :, :, :L].transpose(0, 3, 1, 2, 4).reshape(b, s, h, d)
    lse = lse.reshape(b, dilation, h, Lp)[:, :, :, :L].transpose(0, 3, 1, 2).reshape(b, s, h)
    return o, lse


def hybrid_mixer(hn, w_in, conv_w, conv_b, dt_bias, a_log, d_skip, ssd_norm_w,
                 q_norm_w, k_norm_w, attn_norm_w, w_out):
    b, s, _ = hn.shape
    proj = hn @ w_in
    o1 = SSD_D_INNER
    o2 = o1 + CONV_CH
    o3 = o2 + SSD_HEADS
    o4 = o3 + ATT_D
    o5 = o4 + ATT_D
    z, xbc, dt_raw, q, k, v = jnp.split(proj, [o1, o2, o3, o4, o5], axis=-1)

    xbc = jax.nn.silu(causal_depthwise_conv(xbc, conv_w, conv_b))
    xs, bm, cm = jnp.split(xbc, [SSD_D_INNER, SSD_D_INNER + SSD_GROUPS * SSD_STATE], axis=-1)
    xs = xs.reshape(b, s, SSD_HEADS, HEAD_DIM)
    bm = bm.reshape(b, s, SSD_GROUPS, SSD_STATE)
    cm = cm.reshape(b, s, SSD_GROUPS, SSD_STATE)
    dt = jax.nn.softplus(dt_raw.astype(jnp.float32) + dt_bias.astype(jnp.float32))
    a = -jnp.exp(a_log.astype(jnp.float32))
    y = ssd_chunked_scan(xs, dt, a, bm, cm) + d_skip.astype(jnp.float32)[:, None] * xs
    y = y.reshape(b, s, SSD_D_INNER).astype(hn.dtype) * jax.nn.silu(z)
    y_ssd = (_rms(y.reshape(b, s, SSD_GROUPS, SSD_D_INNER // SSD_GROUPS)).reshape(b, s, SSD_D_INNER)
             * ssd_norm_w.astype(jnp.float32)).astype(hn.dtype)

    q = rms_norm(q.reshape(b, s, ATT_HEADS, HEAD_DIM), q_norm_w) * HEAD_DIM ** -0.5
    k = rms_norm(k.reshape(b, s, ATT_HEADS, HEAD_DIM), k_norm_w)
    v = v.reshape(b, s, ATT_HEADS, HEAD_DIM)
    branches = [dilated_window_attention(q, k, v, w, r) for (w, r) in DILATED_PATTERNS]
    outs = jnp.stack([br[0] for br in branches])
    lses = jnp.stack([br[1] for br in branches])
    alpha = jax.nn.softmax(lses, axis=0)
    o = jnp.sum(alpha[..., None] * outs, axis=0).reshape(b, s, ATT_D)
    y_att = rms_norm(o, attn_norm_w).astype(hn.dtype)

    return jnp.concatenate([y_ssd, y_att], axis=-1) @ w_out


def setup_inputs(seed: int = 0) -> dict:
    key = jax.random.key(seed)
    ks = jax.random.split(key, 20)
    f32 = jnp.float32

    def nrm(k, shape, scale):
        return jax.random.normal(k, shape, f32) * scale

    dt0 = jnp.exp(jax.random.uniform(ks[9], (DEPTH, SSD_HEADS), f32, np.log(1e-3), np.log(1e-1)))
    dt_bias = dt0 + jnp.log(-jnp.expm1(-dt0))
    return {
        'x': nrm(ks[0], (BATCH, SEQ, D_MODEL), 1.0),
        'c': nrm(ks[1], (BATCH, D_MODEL), 1.0),
        'norm1_w': 1.0 + nrm(ks[2], (DEPTH, D_MODEL), 0.02),
        'norm2_w': 1.0 + nrm(ks[3], (DEPTH, D_MODEL), 0.02),
        'w_ada': nrm(ks[4], (DEPTH, D_MODEL, N_MOD * D_MODEL), D_MODEL ** -0.5),
        'b_ada': nrm(ks[5], (DEPTH, N_MOD * D_MODEL), 0.01),
        'w_in': nrm(ks[6], (DEPTH, D_MODEL, IN_PROJ_WIDTH), D_MODEL ** -0.5),
        'conv_w': nrm(ks[7], (DEPTH, SSD_CONV, CONV_CH), SSD_CONV ** -0.5),
        'conv_b': nrm(ks[8], (DEPTH, CONV_CH), 0.01),
        'dt_bias': dt_bias,
        'a_log': jnp.log(jax.random.uniform(ks[10], (DEPTH, SSD_HEADS), f32, 1.0, 16.0)),
        'd_skip': 1.0 + nrm(ks[11], (DEPTH, SSD_HEADS), 0.1),
        'ssd_norm_w': 1.0 + nrm(ks[12], (DEPTH, SSD_D_INNER), 0.02),
        'q_norm_w': 1.0 + nrm(ks[13], (DEPTH, HEAD_DIM), 0.02),
        'k_norm_w': 1.0 + nrm(ks[14], (DEPTH, HEAD_DIM), 0.02),
        'attn_norm_w': 1.0 + nrm(ks[15], (DEPTH, ATT_D), 0.02),
        'w_out': nrm(ks[16], (DEPTH, MIX_WIDTH, D_MODEL), MIX_WIDTH ** -0.5),
        'w_ff1': nrm(ks[17], (DEPTH, D_MODEL, D_FF), D_MODEL ** -0.5),
        'w_ff2': nrm(ks[18], (DEPTH, D_FF, D_MODEL), D_FF ** -0.5),
    }


def reference(x, c, norm1_w, norm2_w, w_ada, b_ada, w_in, conv_w, conv_b, dt_bias, a_log,
              d_skip, ssd_norm_w, q_norm_w, k_norm_w, attn_norm_w, w_out, w_ff1, w_ff2):
    c_act = jax.nn.silu(c)
    for l in range(DEPTH):
        mod = c_act @ w_ada[l] + b_ada[l]
        shift1, scale1, gate1, shift2, scale2, gate2 = [t[:, None, :] for t in jnp.split(mod, N_MOD, axis=-1)]
        h1 = rms_norm(x, norm1_w[l]) * (1.0 + scale1) + shift1
        mix = hybrid_mixer(h1, w_in[l], conv_w[l], conv_b[l], dt_bias[l], a_log[l], d_skip[l],
                           ssd_norm_w[l], q_norm_w[l], k_norm_w[l], attn_norm_w[l], w_out[l])
        x = x + gate1 * mix
        h2 = rms_norm(x, norm2_w[l]) * (1.0 + scale2) + shift2
        ff = jnp.square(jax.nn.relu(h2 @ w_ff1[l])) @ w_ff2[l]
        x = x + gate2 * ff
    return x.astype(c.dtype)
```

```python
import functools

import jax
import jax.numpy as jnp
from jax import lax
from jax.experimental import pallas as pl
from jax.experimental.pallas import tpu as pltpu

F32 = jnp.float32
BF16 = jnp.bfloat16

D_MODEL = 1024
HEAD_DIM = 64
N_HEADS = 16
SSD_GROUPS = 4
SSD_STATE = 128
SSD_CONV = 4
SSD_CHUNK = 128
D_INNER = N_HEADS * HEAD_DIM
CONV_CH = D_INNER + 2 * SSD_GROUPS * SSD_STATE
D_FF = 4 * D_MODEL
N_MOD = 6
EPS = 1e-6
DILATED_PATTERNS = ((128, 1), (512, 4), (2048, 16))
ATT_BLK = 128
LANES = 128
VMEM_LIMIT = 56 * 1024 * 1024

NT_DIMS = (((1,), (1,)), ((), ()))
TN_DIMS = (((0,), (0,)), ((), ()))


def _dot(a, b):
    return jnp.dot(a, b, preferred_element_type=F32)


def _split_bf16(v):
    hi = v.astype(BF16)
    lo = (v - hi.astype(F32)).astype(BF16)
    return hi, lo


def _sigmoid(v):
    return 1.0 / (1.0 + jnp.exp(-v))


def _const_spec(shape):
    nd = len(shape)
    return pl.BlockSpec(shape, lambda *_: (0,) * nd, pipeline_mode=pl.Buffered(1))


def _mod_kernel(c_ref, w_ref, b_ref, o_ref):
    c = c_ref[...]
    ca = c * _sigmoid(c)
    c_hi, c_lo = _split_bf16(ca)
    w = w_ref[...]
    w_hi, w_lo = _split_bf16(w)
    acc = _dot(c_hi, w_hi) + _dot(c_lo, w_hi) + _dot(c_hi, w_lo)
    o_ref[...] = acc + b_ref[...]


def _modulation(c, w_ada, b_ada):
    bsz, d = c.shape
    n = w_ada.shape[1]
    tn = 1536
    return pl.pallas_call(
        _mod_kernel,
        grid=(n // tn,),
        in_specs=[
            pl.BlockSpec((bsz, d), lambda j: (0, 0)),
            pl.BlockSpec((d, tn), lambda j: (0, j)),
            pl.BlockSpec((1, tn), lambda j: (0, j)),
        ],
        out_specs=pl.BlockSpec((bsz, tn), lambda j: (0, j)),
        out_shape=jax.ShapeDtypeStruct((bsz, n), F32),
        compiler_params=pltpu.CompilerParams(
            dimension_semantics=("arbitrary",), vmem_limit_bytes=VMEM_LIMIT),
        name="adaln_mod",
    )(c, w_ada, b_ada.reshape(1, n))


IN_TN = 512


def _norm_mod(x, nw, scale, shift):
    ms = jnp.mean(x * x, axis=-1, keepdims=True)
    return (x * lax.rsqrt(ms + EPS) * nw) * (1.0 + scale) + shift


def _inproj_kernel(x_ref, mod_ref, nw_ref, w_ref, qw_ref, kw_ref, bd_ref,
                   z_ref, xbc_ref, dt_ref, q_ref, k_ref, v_ref, h_ref):
    x = x_ref[0]
    h = _norm_mod(x, nw_ref[...], mod_ref[0, 1:2, :], mod_ref[0, 0:1, :])
    h_ref[...] = h.astype(BF16)

    def proj(c0, width):
        return _dot(h_ref[...], w_ref[:, c0:c0 + width])

    def qk_norm(acc, w):
        sq = (acc * acc).astype(BF16)
        parts = [_dot(sq[:, i:i + 256], bd_ref[...]) for i in range(0, IN_TN, 256)]
        ss = jnp.concatenate(parts, axis=1)
        return acc * lax.rsqrt(ss * (1.0 / HEAD_DIM) + EPS) * w

    col = 0
    for j in range(D_INNER // IN_TN):
        z_ref[0, :, j * IN_TN:(j + 1) * IN_TN] = proj(col, IN_TN).astype(BF16)
        col += IN_TN
    for j in range(CONV_CH // IN_TN):
        xbc_ref[0, :, j * IN_TN:(j + 1) * IN_TN] = proj(col, IN_TN).astype(BF16)
        col += IN_TN
    for j in range(D_INNER // IN_TN):
        sl = slice(j * IN_TN, (j + 1) * IN_TN)
        qn = qk_norm(proj(col, IN_TN), qw_ref[:, sl]) * (HEAD_DIM ** -0.5)
        q_ref[0, :, sl] = qn.astype(BF16)
        col += IN_TN
    for j in range(D_INNER // IN_TN):
        sl = slice(j * IN_TN, (j + 1) * IN_TN)
        k_ref[0, :, sl] = qk_norm(proj(col, IN_TN), kw_ref[:, sl]).astype(BF16)
        col += IN_TN
    for j in range(D_INNER // IN_TN):
        v_ref[0, :, j * IN_TN:(j + 1) * IN_TN] = proj(col, IN_TN).astype(BF16)
        col += IN_TN
    dt_ref[0] = proj(col, LANES)


def _in_projection(x, mod3, norm_w, w_pad, qw, kw, bd, tm):
    bsz, s, d = x.shape
    wcols = w_pad.shape[1]
    row = lambda b, i: (b, i, 0)
    out_bf = lambda n: jax.ShapeDtypeStruct((bsz, s, n), BF16)
    return pl.pallas_call(
        _inproj_kernel,
        grid=(bsz, s // tm),
        in_specs=[
            pl.BlockSpec((1, tm, d), row),
            pl.BlockSpec((1, N_MOD, d), lambda b, i: (b, 0, 0)),
            _const_spec((1, d)),
            _const_spec((d, wcols)),
            _const_spec((1, D_INNER)),
            _const_spec((1, D_INNER)),
            _const_spec((256, 256)),
        ],
        out_specs=[
            pl.BlockSpec((1, tm, D_INNER), row),
            pl.BlockSpec((1, tm, CONV_CH), row),
            pl.BlockSpec((1, tm, LANES), row),
            pl.BlockSpec((1, tm, D_INNER), row),
            pl.BlockSpec((1, tm, D_INNER), row),
            pl.BlockSpec((1, tm, D_INNER), row),
        ],
        out_shape=[out_bf(D_INNER), out_bf(CONV_CH),
                   jax.ShapeDtypeStruct((bsz, s, LANES), F32),
                   out_bf(D_INNER), out_bf(D_INNER), out_bf(D_INNER)],
        scratch_shapes=[pltpu.VMEM((tm, d), BF16)],
        compiler_params=pltpu.CompilerParams(
            dimension_semantics=("parallel", "arbitrary"), vmem_limit_bytes=VMEM_LIMIT),
        name="in_proj",
    )(x, mod3, norm_w, w_pad, qw, kw, bd)


SSD_ROWS = 256
CONV_HALO = 8


def _ssd_kernel(xbc_ref, z_ref, dt_ref, cw_ref, cb_ref, dtb_ref, alog_ref, dskip_ref,
                nw_ref, tril2_ref, e2_ref, o_ref, xpad_ref, xc_ref, state_ref):
    rows = xbc_ref.shape[1]
    L = SSD_CHUNK

    @pl.when(pl.program_id(1) == 0)
    def _():
        xpad_ref[0:CONV_HALO, :] = jnp.zeros((CONV_HALO, CONV_CH), F32)
        state_ref[...] = jnp.zeros_like(state_ref)

    xpad_ref[CONV_HALO:CONV_HALO + rows, :] = xbc_ref[0].astype(F32)
    for c0 in range(0, CONV_CH, 256):
        cs = slice(c0, c0 + 256)
        acc = jnp.broadcast_to(cb_ref[:, cs], (rows, 256))
        for i in range(SSD_CONV):
            start = CONV_HALO - (SSD_CONV - 1) + i
            acc = acc + cw_ref[i:i + 1, cs] * xpad_ref[start:start + rows, cs]
        xc_ref[:, cs] = acc * _sigmoid(acc)
    xpad_ref[0:CONV_HALO, :] = xpad_ref[rows:rows + CONV_HALO, :]

    lane = lax.broadcasted_iota(jnp.int32, (L, LANES), 1)
    sub = lax.broadcasted_iota(jnp.int32, (L, LANES), 0)
    tril = sub >= lane
    lo_half = lane < HEAD_DIM
    a_neg = jnp.where(lane[0:1] < N_HEADS, -jnp.exp(alog_ref[...]), 0.0)

    def expand(v):
        hi, lo = _split_bf16(v)
        return _dot(jnp.concatenate([hi, lo], axis=1), e2_ref[...])

    for c in range(rows // L):
        r0 = c * L
        rs = slice(r0, r0 + L)
        dt_raw = dt_ref[0, rs, :] + dtb_ref[...]
        dt = jnp.maximum(dt_raw, 0.0) + jnp.log(1.0 + jnp.exp(-jnp.abs(dt_raw)))
        d_a = dt * a_neg
        da_hi, da_lo = _split_bf16(d_a)
        a_cs = _dot(tril2_ref[...], jnp.concatenate([da_hi, da_lo], axis=0))
        a_cs_t = a_cs.T
        a_last = a_cs[L - 1:L, :]
        dec_in = jnp.exp(a_last - a_cs)
        dec_out = jnp.exp(a_cs)
        dt_x = expand(dt)
        w_x = expand(dt * dec_in)
        do_x = expand(dec_out)

        xs = xc_ref[rs, 0:D_INNER]
        xdt = (xs * dt_x).astype(BF16)
        xw = (xs * w_x).astype(BF16)
        y = dskip_ref[...] * xs
        z = z_ref[0, rs, :].astype(F32)
        gate = z * _sigmoid(z)

        for g in range(SSD_GROUPS):
            b0 = D_INNER + g * SSD_STATE
            c0 = D_INNER + SSD_GROUPS * SSD_STATE + g * SSD_STATE
            bg = xc_ref[rs, b0:b0 + SSD_STATE].astype(BF16)
            cg = xc_ref[rs, c0:c0 + SSD_STATE].astype(BF16)
            cb = lax.dot_general(cg, bg, NT_DIMS, preferred_element_type=F32)
            gs = slice(g * 256, (g + 1) * 256)
            yd_parts = []
            for pair in range(2):
                ms = []
                for e in range(2):
                    hd = g * 4 + pair * 2 + e
                    colb = jnp.broadcast_to(a_cs[:, hd:hd + 1], (L, L))
                    rowb = jnp.broadcast_to(a_cs_t[hd:hd + 1, :], (L, L))
                    lmat = jnp.exp(jnp.where(tril, colb - rowb, -jnp.inf))
                    ms.append((cb * lmat).astype(BF16))
                mcat = jnp.concatenate(ms, axis=1)
                p0 = g * 256 + pair * LANES
                xp = xdt[:, p0:p0 + LANES]
                zero = jnp.zeros_like(xp)
                xstack = jnp.concatenate(
                    [jnp.where(lo_half, xp, zero), jnp.where(lo_half, zero, xp)], axis=0)
                yd_parts.append(_dot(mcat, xstack))
            y_diag = jnp.concatenate(yd_parts, axis=1)
            st = state_ref[:, gs]
            y_off = _dot(cg, st.astype(BF16)) * do_x[:, gs]
            upd = lax.dot_general(bg, xw[:, gs], TN_DIMS, preferred_element_type=F32)
            state_ref[:, gs] = st * do_x[L - 1:L, gs] + upd
            yg = (y[:, gs] + y_diag + y_off) * gate[:, gs]
            ss = jnp.mean(yg * yg, axis=-1, keepdims=True)
            o_ref[0, rs, gs] = (yg * lax.rsqrt(ss + EPS) * nw_ref[:, gs]).astype(BF16)


def _ssd_mixer(xbc, z, dt, conv_w, conv_b, dt_bias, a_log, d_skip, norm_w, tril2, e2):
    bsz, s, _ = xbc.shape
    rows = SSD_ROWS
    row = lambda b, i: (b, i, 0)
    return pl.pallas_call(
        _ssd_kernel,
        grid=(bsz, s // rows),
        in_specs=[
            pl.BlockSpec((1, rows, CONV_CH), row),
            pl.BlockSpec((1, rows, D_INNER), row),
            pl.BlockSpec((1, rows, LANES), row),
            _const_spec((SSD_CONV, CONV_CH)),
            _const_spec((1, CONV_CH)),
            _const_spec((1, LANES)),
            _const_spec((1, LANES)),
            _const_spec((1, D_INNER)),
            _const_spec((1, D_INNER)),
            _const_spec((SSD_CHUNK, 2 * SSD_CHUNK)),
            _const_spec((2 * LANES, D_INNER)),
        ],
        out_specs=pl.BlockSpec((1, rows, D_INNER), row),
        out_shape=jax.ShapeDtypeStruct((bsz, s, D_INNER), BF16),
        scratch_shapes=[
            pltpu.VMEM((rows + CONV_HALO, CONV_CH), F32),
            pltpu.VMEM((rows, CONV_CH), F32),
            pltpu.VMEM((SSD_STATE, D_INNER), F32),
        ],
        compiler_params=pltpu.CompilerParams(
            dimension_semantics=("parallel", "arbitrary"), vmem_limit_bytes=VMEM_LIMIT),
        name="ssd_mixer",
    )(xbc, z, dt, conv_w, conv_b, dt_bias, a_log, d_skip, norm_w, tril2, e2)


def _attn_kernel(q_ref, kp_ref, kc_ref, vp_ref, vc_ref, o_ref, lse_ref):
    blk = ATT_BLK
    first = pl.program_id(2) == 0
    row = lax.broadcasted_iota(jnp.int32, (2 * blk, 2 * blk), 0)
    key = lax.broadcasted_iota(jnp.int32, (2 * blk, 2 * blk), 1)
    qi = jnp.where(row >= blk, row - blk, row)
    rel = key - qi
    lo_key = jnp.where(first, blk, 0)
    mask = (rel >= 0) & (rel <= blk) & (key >= lo_key)
    lane = lax.broadcasted_iota(jnp.int32, (blk, LANES), 1)
    lo_half = lane < HEAD_DIM

    lse_tile = jnp.zeros((blk, LANES), F32)
    for p in range(N_HEADS // 2):
        ps = slice(p * LANES, (p + 1) * LANES)
        qp = q_ref[0, :, ps]
        zero = jnp.zeros_like(qp)
        qs = jnp.concatenate([jnp.where(lo_half, qp, zero), jnp.where(lo_half, zero, qp)], axis=0)
        k2 = jnp.concatenate([kp_ref[0, :, ps], kc_ref[0, :, ps]], axis=0)
        v2 = jnp.concatenate([vp_ref[0, :, ps], vc_ref[0, :, ps]], axis=0)
        s = lax.dot_general(qs, k2, NT_DIMS, preferred_element_type=F32)
        s = jnp.where(mask, s, -jnp.inf)
        m = jnp.max(s, axis=-1, keepdims=True)
        e = jnp.exp(s - m)
        l = jnp.sum(e, axis=-1, keepdims=True)
        pv = _dot(e.astype(BF16), v2)
        inv = 1.0 / l
        o_pair = jnp.where(lo_half, pv[0:blk] * inv[0:blk], pv[blk:] * inv[blk:])
        o_ref[0, :, ps] = o_pair.astype(BF16)
        lse = m + jnp.log(l)
        lse_tile = jnp.where(lane == 2 * p, lse[0:blk],
                             jnp.where(lane == 2 * p + 1, lse[blk:], lse_tile))
    lse_ref[0] = lse_tile


def _dilated_attention(q, k, v, dilation):
    bsz, s, w = q.shape
    r = dilation
    n_cls = s // r
    nb = n_cls // ATT_BLK
    view = lambda t: t.reshape(bsz, n_cls, r * t.shape[-1])
    cur = lambda b, res, j: (b, j, res)
    prev = lambda b, res, j: (b, jnp.maximum(j - 1, 0), res)
    blk_spec = lambda im: pl.BlockSpec((1, ATT_BLK, w), im)
    o, lse = pl.pallas_call(
        _attn_kernel,
        grid=(bsz, r, nb),
        in_specs=[blk_spec(cur), blk_spec(prev), blk_spec(cur), blk_spec(prev), blk_spec(cur)],
        out_specs=[blk_spec(cur), pl.BlockSpec((1, ATT_BLK, LANES), cur)],
        out_shape=[jax.ShapeDtypeStruct((bsz, n_cls, r * w), BF16),
                   jax.ShapeDtypeStruct((bsz, n_cls, r * LANES), F32)],
        compiler_params=pltpu.CompilerParams(
            dimension_semantics=("parallel", "parallel", "arbitrary"),
            vmem_limit_bytes=VMEM_LIMIT),
        name=f"dilated_attn_r{r}",
    )(view(q), view(k), view(k), view(v), view(v))
    return o.reshape(bsz, s, w), lse.reshape(bsz, s, LANES)


def _outproj_kernel(x_ref, mod_ref, ys_ref, o1_ref, o2_ref, o3_ref, l1_ref, l2_ref, l3_ref,
                    nw_ref, e2_ref, w_ref, out_ref):
    l1, l2, l3 = l1_ref[0], l2_ref[0], l3_ref[0]
    mx = jnp.maximum(jnp.maximum(l1, l2), l3)
    e1, e2, e3 = jnp.exp(l1 - mx), jnp.exp(l2 - mx), jnp.exp(l3 - mx)
    inv = 1.0 / (e1 + e2 + e3)

    def expand(v):
        hi, lo = _split_bf16(v)
        return _dot(jnp.concatenate([hi, lo], axis=1), e2_ref[...])

    o = (expand(e1 * inv) * o1_ref[0].astype(F32)
         + expand(e2 * inv) * o2_ref[0].astype(F32)
         + expand(e3 * inv) * o3_ref[0].astype(F32))
    ms = jnp.mean(o * o, axis=-1, keepdims=True)
    y_att = (o * lax.rsqrt(ms + EPS) * nw_ref[...]).astype(BF16)
    mix = _dot(ys_ref[0], w_ref[0:D_INNER, :]) + _dot(y_att, w_ref[D_INNER:, :])
    out_ref[0] = x_ref[0] + mod_ref[0, 2:3, :] * mix


def _out_projection(x, mod3, y_ssd, outs, lses, norm_w, e2, w_out, tm):
    bsz, s, d = x.shape
    row = lambda b, i: (b, i, 0)
    wide = pl.BlockSpec((1, tm, D_INNER), row)
    narrow = pl.BlockSpec((1, tm, LANES), row)
    return pl.pallas_call(
        _outproj_kernel,
        grid=(bsz, s // tm),
        in_specs=[
            pl.BlockSpec((1, tm, d), row),
            pl.BlockSpec((1, N_MOD, d), lambda b, i: (b, 0, 0)),
            wide, wide, wide, wide, narrow, narrow, narrow,
            _const_spec((1, D_INNER)),
            _const_spec((2 * LANES, D_INNER)),
            _const_spec((2 * D_INNER, d)),
        ],
        out_specs=pl.BlockSpec((1, tm, d), row),
        out_shape=jax.ShapeDtypeStruct((bsz, s, d), F32),
        compiler_params=pltpu.CompilerParams(
            dimension_semantics=("parallel", "arbitrary"), vmem_limit_bytes=VMEM_LIMIT),
        name="out_proj",
    )(x, mod3, y_ssd, *outs, *lses, norm_w, e2, w_out)


FF_TN = 1024


def _mlp_kernel(x_ref, mod_ref, nw_ref, w1_ref, w2_ref, out_ref, h_ref):
    x = x_ref[0]
    h = _norm_mod(x, nw_ref[...], mod_ref[0, 4:5, :], mod_ref[0, 3:4, :])
    h_ref[...] = h.astype(BF16)
    acc = jnp.zeros(x.shape, F32)
    for j in range(D_FF // FF_TN):
        fs = slice(j * FF_TN, (j + 1) * FF_TN)
        u = jnp.maximum(_dot(h_ref[...], w1_ref[:, fs]), 0.0)
        acc = acc + _dot((u * u).astype(BF16), w2_ref[fs, :])
    out_ref[0] = x + mod_ref[0, 5:6, :] * acc


def _mlp(x, mod3, norm_w, w1, w2, tm):
    bsz, s, d = x.shape
    row = lambda b, i: (b, i, 0)
    return pl.pallas_call(
        _mlp_kernel,
        grid=(bsz, s // tm),
        in_specs=[
            pl.BlockSpec((1, tm, d), row),
            pl.BlockSpec((1, N_MOD, d), lambda b, i: (b, 0, 0)),
            _const_spec((1, d)),
            _const_spec((d, D_FF)),
            _const_spec((D_FF, d)),
        ],
        out_specs=pl.BlockSpec((1, tm, d), row),
        out_shape=jax.ShapeDtypeStruct((bsz, s, d), F32),
        scratch_shapes=[pltpu.VMEM((tm, d), BF16)],
        compiler_params=pltpu.CompilerParams(
            dimension_semantics=("parallel", "arbitrary"), vmem_limit_bytes=VMEM_LIMIT),
        name="mlp",
    )(x, mod3, norm_w, w1, w2)


def _head_expand_matrix():
    head_of_lane = jnp.arange(D_INNER) // HEAD_DIM
    e = (jnp.arange(LANES)[:, None] == head_of_lane[None, :]).astype(BF16)
    return jnp.concatenate([e, e], axis=0)


def _pad_lanes(v, n=LANES):
    return jnp.pad(v.astype(F32), (0, n - v.shape[0])).reshape(1, n)


def kernel(x, c, norm1_w, norm2_w, w_ada, b_ada, w_in, conv_w, conv_b, dt_bias, a_log, d_skip,
           ssd_norm_w, q_norm_w, k_norm_w, attn_norm_w, w_out, w_ff1, w_ff2):
    bsz, s, d = x.shape
    depth = w_ada.shape[0]
    tm = 512
    e2 = _head_expand_matrix()
    idx = jnp.arange(256) // HEAD_DIM
    bd = (idx[:, None] == idx[None, :]).astype(BF16)
    t = (jnp.arange(SSD_CHUNK)[:, None] >= jnp.arange(SSD_CHUNK)[None, :]).astype(BF16)
    tril2 = jnp.concatenate([t, t], axis=1)
    o_xbc = D_INNER + CONV_CH
    o_dt = o_xbc + N_HEADS

    for l in range(depth):
        mod3 = _modulation(c, w_ada[l], b_ada[l]).reshape(bsz, N_MOD, d)
        wl = w_in[l]
        w_pad = jnp.concatenate(
            [wl[:, :o_xbc], wl[:, o_dt:], wl[:, o_xbc:o_dt],
             jnp.zeros((d, LANES - N_HEADS), wl.dtype)], axis=1).astype(BF16)
        z, xbc, dt, q, k, v = _in_projection(
            x, mod3, norm1_w[l].reshape(1, d), w_pad,
            jnp.tile(q_norm_w[l], N_HEADS).reshape(1, D_INNER),
            jnp.tile(k_norm_w[l], N_HEADS).reshape(1, D_INNER), bd, tm)
        y_ssd = _ssd_mixer(
            xbc, z, dt, conv_w[l], conv_b[l].reshape(1, CONV_CH), _pad_lanes(dt_bias[l]),
            _pad_lanes(a_log[l]), jnp.repeat(d_skip[l], HEAD_DIM).reshape(1, D_INNER),
            ssd_norm_w[l].reshape(1, D_INNER), tril2, e2)
        outs, lses = [], []
        for window, dilation in DILATED_PATTERNS:
            assert window // dilation == ATT_BLK
            o, lse = _dilated_attention(q, k, v, dilation)
            outs.append(o)
            lses.append(lse)
        x = _out_projection(x, mod3, y_ssd, outs, lses, attn_norm_w[l].reshape(1, D_INNER),
                            e2, w_out[l].astype(BF16), tm)
        x = _mlp(x, mod3, norm2_w[l].reshape(1, d), w_ff1[l].astype(BF16),
                 w_ff2[l].astype(BF16), tm)
    return x.astype(c.dtype)
```

```python
import functools

import jax
import jax.numpy as jnp
from jax import lax
from jax.experimental import pallas as pl
from jax.experimental.pallas import tpu as pltpu

F32 = jnp.float32
BF16 = jnp.bfloat16

D_MODEL = 1024
HEAD_DIM = 64
N_HEADS = 16
SSD_GROUPS = 4
SSD_STATE = 128
SSD_CONV = 4
SSD_CHUNK = 128
D_INNER = N_HEADS * HEAD_DIM
CONV_CH = D_INNER + 2 * SSD_GROUPS * SSD_STATE
D_FF = 4 * D_MODEL
N_MOD = 6
EPS = 1e-6
DILATED_PATTERNS = ((128, 1), (512, 4), (2048, 16))
ATT_BLK = 128
CLS_STRIDE = 16
LANES = 128
VMEM_LIMIT = 56 * 1024 * 1024

NT_DIMS = (((1,), (1,)), ((), ()))
TN_DIMS = (((0,), (0,)), ((), ()))


def _dot(a, b):
    return jnp.dot(a, b, preferred_element_type=F32)


def _split_bf16(v):
    hi = v.astype(BF16)
    lo = (v - hi.astype(F32)).astype(BF16)
    return hi, lo


def _sigmoid(v):
    return 1.0 / (1.0 + jnp.exp(-v))


def _const_spec(shape):
    nd = len(shape)
    return pl.BlockSpec(shape, lambda *_: (0,) * nd, pipeline_mode=pl.Buffered(1))


def _mod_kernel(c_ref, w_ref, b_ref, o_ref):
    c = c_ref[...]
    ca = c * _sigmoid(c)
    c_hi, c_lo = _split_bf16(ca)
    w = w_ref[...]
    w_hi, w_lo = _split_bf16(w)
    acc = _dot(c_hi, w_hi) + _dot(c_lo, w_hi) + _dot(c_hi, w_lo)
    o_ref[...] = acc + b_ref[...]


def _modulation(c, w_ada, b_ada):
    bsz, d = c.shape
    n = w_ada.shape[1]
    tn = 1536
    return pl.pallas_call(
        _mod_kernel,
        grid=(n // tn,),
        in_specs=[
            pl.BlockSpec((bsz, d), lambda j: (0, 0)),
            pl.BlockSpec((d, tn), lambda j: (0, j)),
            pl.BlockSpec((1, tn), lambda j: (0, j)),
        ],
        out_specs=pl.BlockSpec((bsz, tn), lambda j: (0, j)),
        out_shape=jax.ShapeDtypeStruct((bsz, n), F32),
        compiler_params=pltpu.CompilerParams(
            dimension_semantics=("arbitrary",), vmem_limit_bytes=VMEM_LIMIT),
        name="adaln_mod",
    )(c, w_ada, b_ada.reshape(1, n))


IN_TN = 512


def _norm_mod(x, nw, scale, shift):
    ms = jnp.mean(x * x, axis=-1, keepdims=True)
    return (x * lax.rsqrt(ms + EPS) * nw) * (1.0 + scale) + shift


def _inproj_kernel(x_ref, mod_ref, nw_ref, w_ref, qw_ref, kw_ref, bd_ref,
                   z_ref, xbc_ref, dt_ref, q_ref, k_ref, v_ref, q16_ref, k16_ref, v16_ref,
                   h_ref, stage_ref):
    x = x_ref[0]
    h = _norm_mod(x, nw_ref[...], mod_ref[0, 1:2, :], mod_ref[0, 0:1, :])
    h_ref[...] = h.astype(BF16)

    def proj(c0, width):
        return _dot(h_ref[...], w_ref[:, c0:c0 + width])

    def qk_norm(acc, w):
        sq = (acc * acc).astype(BF16)
        parts = [_dot(sq[:, i:i + 256], bd_ref[...]) for i in range(0, IN_TN, 256)]
        ss = jnp.concatenate(parts, axis=1)
        return acc * lax.rsqrt(ss * (1.0 / HEAD_DIM) + EPS) * w

    cls_rows = x.shape[0] // CLS_STRIDE

    def emit(nat_ref, cls_ref, sl, val):
        nat_ref[0, :, sl] = val.astype(BF16)
        for t in range(IN_TN // LANES):
            stage_ref[t] = val[:, t * LANES:(t + 1) * LANES]
        for res in range(CLS_STRIDE):
            rows = [stage_ref[t, pl.ds(res, cls_rows, stride=CLS_STRIDE), :]
                    for t in range(IN_TN // LANES)]
            cls_ref[0, res, :, sl] = jnp.concatenate(rows, axis=1).astype(BF16)

    col = 0
    for j in range(D_INNER // IN_TN):
        z_ref[0, :, j * IN_TN:(j + 1) * IN_TN] = proj(col, IN_TN).astype(BF16)
        col += IN_TN
    for j in range(CONV_CH // IN_TN):
        xbc_ref[0, :, j * IN_TN:(j + 1) * IN_TN] = proj(col, IN_TN).astype(BF16)
        col += IN_TN
    for j in range(D_INNER // IN_TN):
        sl = slice(j * IN_TN, (j + 1) * IN_TN)
        emit(q_ref, q16_ref, sl, qk_norm(proj(col, IN_TN), qw_ref[:, sl]) * (HEAD_DIM ** -0.5))
        col += IN_TN
    for j in range(D_INNER // IN_TN):
        sl = slice(j * IN_TN, (j + 1) * IN_TN)
        emit(k_ref, k16_ref, sl, qk_norm(proj(col, IN_TN), kw_ref[:, sl]))
        col += IN_TN
    for j in range(D_INNER // IN_TN):
        emit(v_ref, v16_ref, slice(j * IN_TN, (j + 1) * IN_TN), proj(col, IN_TN))
        col += IN_TN
    dt_ref[0] = proj(col, LANES)


def _in_projection(x, mod3, norm_w, w_pad, qw, kw, bd, tm):
    bsz, s, d = x.shape
    wcols = w_pad.shape[1]
    row = lambda b, i: (b, i, 0)
    out_bf = lambda n: jax.ShapeDtypeStruct((bsz, s, n), BF16)
    cls_spec = pl.BlockSpec((1, CLS_STRIDE, tm // CLS_STRIDE, D_INNER), lambda b, i: (b, 0, i, 0))
    cls_shape = jax.ShapeDtypeStruct((bsz, CLS_STRIDE, s // CLS_STRIDE, D_INNER), BF16)
    return pl.pallas_call(
        _inproj_kernel,
        grid=(bsz, s // tm),
        in_specs=[
            pl.BlockSpec((1, tm, d), row),
            pl.BlockSpec((1, N_MOD, d), lambda b, i: (b, 0, 0)),
            _const_spec((1, d)),
            _const_spec((d, wcols)),
            _const_spec((1, D_INNER)),
            _const_spec((1, D_INNER)),
            _const_spec((256, 256)),
        ],
        out_specs=[
            pl.BlockSpec((1, tm, D_INNER), row),
            pl.BlockSpec((1, tm, CONV_CH), row),
            pl.BlockSpec((1, tm, LANES), row),
            pl.BlockSpec((1, tm, D_INNER), row),
            pl.BlockSpec((1, tm, D_INNER), row),
            pl.BlockSpec((1, tm, D_INNER), row),
            cls_spec, cls_spec, cls_spec,
        ],
        out_shape=[out_bf(D_INNER), out_bf(CONV_CH),
                   jax.ShapeDtypeStruct((bsz, s, LANES), F32),
                   out_bf(D_INNER), out_bf(D_INNER), out_bf(D_INNER),
                   cls_shape, cls_shape, cls_shape],
        scratch_shapes=[pltpu.VMEM((tm, d), BF16), pltpu.VMEM((IN_TN // LANES, tm, LANES), F32)],
        compiler_params=pltpu.CompilerParams(
            dimension_semantics=("parallel", "arbitrary"), vmem_limit_bytes=VMEM_LIMIT),
        name="in_proj",
    )(x, mod3, norm_w, w_pad, qw, kw, bd)


SSD_ROWS = 256
CONV_HALO = 8


def _ssd_kernel(xbc_ref, z_ref, dt_ref, cw_ref, cb_ref, dtb_ref, alog_ref, dskip_ref,
                nw_ref, tril2_ref, e2_ref, o_ref, xpad_ref, xc_ref, state_ref):
    rows = xbc_ref.shape[1]
    L = SSD_CHUNK

    @pl.when(pl.program_id(1) == 0)
    def _():
        xpad_ref[0:CONV_HALO, :] = jnp.zeros((CONV_HALO, CONV_CH), F32)
        state_ref[...] = jnp.zeros_like(state_ref)

    xpad_ref[CONV_HALO:CONV_HALO + rows, :] = xbc_ref[0].astype(F32)
    for c0 in range(0, CONV_CH, 256):
        cs = slice(c0, c0 + 256)
        acc = jnp.broadcast_to(cb_ref[:, cs], (rows, 256))
        for i in range(SSD_CONV):
            start = CONV_HALO - (SSD_CONV - 1) + i
            acc = acc + cw_ref[i:i + 1, cs] * xpad_ref[start:start + rows, cs]
        xc_ref[:, cs] = acc * _sigmoid(acc)
    xpad_ref[0:CONV_HALO, :] = xpad_ref[rows:rows + CONV_HALO, :]

    lane = lax.broadcasted_iota(jnp.int32, (L, LANES), 1)
    sub = lax.broadcasted_iota(jnp.int32, (L, LANES), 0)
    tril = sub >= lane
    lo_half = lane < HEAD_DIM
    a_neg = jnp.where(lane[0:1] < N_HEADS, -jnp.exp(alog_ref[...]), 0.0)

    def expand(v):
        hi, lo = _split_bf16(v)
        return _dot(jnp.concatenate([hi, lo], axis=1), e2_ref[...])

    for c in range(rows // L):
        r0 = c * L
        rs = slice(r0, r0 + L)
        dt_raw = dt_ref[0, rs, :] + dtb_ref[...]
        dt = jnp.maximum(dt_raw, 0.0) + jnp.log(1.0 + jnp.exp(-jnp.abs(dt_raw)))
        d_a = dt * a_neg
        da_hi, da_lo = _split_bf16(d_a)
        a_cs = _dot(tril2_ref[...], jnp.concatenate([da_hi, da_lo], axis=0))
        a_cs_t = a_cs.T
        a_last = a_cs[L - 1:L, :]
        dec_in = jnp.exp(a_last - a_cs)
        dec_out = jnp.exp(a_cs)
        dt_x = expand(dt)
        w_x = expand(dt * dec_in)
        do_x = expand(dec_out)

        xs = xc_ref[rs, 0:D_INNER]
        xdt = (xs * dt_x).astype(BF16)
        xw = (xs * w_x).astype(BF16)
        y = dskip_ref[...] * xs
        z = z_ref[0, rs, :].astype(F32)
        gate = z * _sigmoid(z)

        for g in range(SSD_GROUPS):
            b0 = D_INNER + g * SSD_STATE
            c0 = D_INNER + SSD_GROUPS * SSD_STATE + g * SSD_STATE
            bg = xc_ref[rs, b0:b0 + SSD_STATE].astype(BF16)
            cg = xc_ref[rs, c0:c0 + SSD_STATE].astype(BF16)
            cb = lax.dot_general(cg, bg, NT_DIMS, preferred_element_type=F32)
            gs = slice(g * 256, (g + 1) * 256)
            yd_parts = []
            for pair in range(2):
                ms = []
                for e in range(2):
                    hd = g * 4 + pair * 2 + e
                    colb = jnp.broadcast_to(a_cs[:, hd:hd + 1], (L, L))
                    rowb = jnp.broadcast_to(a_cs_t[hd:hd + 1, :], (L, L))
                    lmat = jnp.exp(jnp.where(tril, colb - rowb, -jnp.inf))
                    ms.append((cb * lmat).astype(BF16))
                mcat = jnp.concatenate(ms, axis=1)
                p0 = g * 256 + pair * LANES
                xp = xdt[:, p0:p0 + LANES]
                zero = jnp.zeros_like(xp)
                xstack = jnp.concatenate(
                    [jnp.where(lo_half, xp, zero), jnp.where(lo_half, zero, xp)], axis=0)
                yd_parts.append(_dot(mcat, xstack))
            y_diag = jnp.concatenate(yd_parts, axis=1)
            st = state_ref[:, gs]
            y_off = _dot(cg, st.astype(BF16)) * do_x[:, gs]
            upd = lax.dot_general(bg, xw[:, gs], TN_DIMS, preferred_element_type=F32)
            state_ref[:, gs] = st * do_x[L - 1:L, gs] + upd
            yg = (y[:, gs] + y_diag + y_off) * gate[:, gs]
            ss = jnp.mean(yg * yg, axis=-1, keepdims=True)
            o_ref[0, rs, gs] = (yg * lax.rsqrt(ss + EPS) * nw_ref[:, gs]).astype(BF16)


def _ssd_mixer(xbc, z, dt, conv_w, conv_b, dt_bias, a_log, d_skip, norm_w, tril2, e2):
    bsz, s, _ = xbc.shape
    rows = SSD_ROWS
    row = lambda b, i: (b, i, 0)
    return pl.pallas_call(
        _ssd_kernel,
        grid=(bsz, s // rows),
        in_specs=[
            pl.BlockSpec((1, rows, CONV_CH), row),
            pl.BlockSpec((1, rows, D_INNER), row),
            pl.BlockSpec((1, rows, LANES), row),
            _const_spec((SSD_CONV, CONV_CH)),
            _const_spec((1, CONV_CH)),
            _const_spec((1, LANES)),
            _const_spec((1, LANES)),
            _const_spec((1, D_INNER)),
            _const_spec((1, D_INNER)),
            _const_spec((SSD_CHUNK, 2 * SSD_CHUNK)),
            _const_spec((2 * LANES, D_INNER)),
        ],
        out_specs=pl.BlockSpec((1, rows, D_INNER), row),
        out_shape=jax.ShapeDtypeStruct((bsz, s, D_INNER), BF16),
        scratch_shapes=[
            pltpu.VMEM((rows + CONV_HALO, CONV_CH), F32),
            pltpu.VMEM((rows, CONV_CH), F32),
            pltpu.VMEM((SSD_STATE, D_INNER), F32),
        ],
        compiler_params=pltpu.CompilerParams(
            dimension_semantics=("parallel", "arbitrary"), vmem_limit_bytes=VMEM_LIMIT),
        name="ssd_mixer",
    )(xbc, z, dt, conv_w, conv_b, dt_bias, a_log, d_skip, norm_w, tril2, e2)


def _attn_kernel(planes, q_ref, kp_ref, kc_ref, vp_ref, vc_ref, o_ref, lse_ref):
    blk = ATT_BLK
    sub = blk // planes
    first = pl.program_id(2) == 0

    def pos(i):
        return i if planes == 1 else (i % sub) * planes + i // sub

    row = lax.broadcasted_iota(jnp.int32, (2 * blk, 2 * blk), 0)
    key = lax.broadcasted_iota(jnp.int32, (2 * blk, 2 * blk), 1)
    qi = pos(jnp.where(row >= blk, row - blk, row))
    ki = jnp.where(key >= blk, blk + pos(key - blk), pos(key))
    rel = ki - qi
    lo_key = jnp.where(first, blk, 0)
    mask = (rel >= 0) & (rel <= blk) & (key >= lo_key)
    lane = lax.broadcasted_iota(jnp.int32, (blk, LANES), 1)
    lo_half = lane < HEAD_DIM

    def load(ref, ps):
        if planes == 1:
            return ref[0, 0, :, ps]
        return ref[0, :, 0, :, ps].reshape(blk, LANES)

    def store(ref, ps, val):
        if planes == 1:
            ref[0, 0, :, ps] = val
        else:
            ref[0, :, 0, :, ps] = val.reshape(planes, sub, val.shape[-1])

    lse_tile = jnp.zeros((blk, LANES), F32)
    for p in range(N_HEADS // 2):
        ps = slice(p * LANES, (p + 1) * LANES)
        qp = load(q_ref, ps)
        zero = jnp.zeros_like(qp)
        qs = jnp.concatenate([jnp.where(lo_half, qp, zero), jnp.where(lo_half, zero, qp)], axis=0)
        k2 = jnp.concatenate([load(kp_ref, ps), load(kc_ref, ps)], axis=0)
        v2 = jnp.concatenate([load(vp_ref, ps), load(vc_ref, ps)], axis=0)
        s = lax.dot_general(qs, k2, NT_DIMS, preferred_element_type=F32)
        s = jnp.where(mask, s, -jnp.inf)
        m = jnp.max(s, axis=-1, keepdims=True)
        e = jnp.exp(s - m)
        l = jnp.sum(e, axis=-1, keepdims=True)
        pv = _dot(e.astype(BF16), v2)
        inv = 1.0 / l
        o_pair = jnp.where(lo_half, pv[0:blk] * inv[0:blk], pv[blk:] * inv[blk:])
        store(o_ref, ps, o_pair.astype(BF16))
        lse = m + jnp.log(l)
        lse_tile = jnp.where(lane == 2 * p, lse[0:blk],
                             jnp.where(lane == 2 * p + 1, lse[blk:], lse_tile))
    store(lse_ref, slice(0, LANES), lse_tile)


def _dilated_attention(q, k, v, dilation):
    bsz, n_planes, rows, w = q.shape
    planes = n_planes // dilation
    assert planes in (1, 4)
    sub = ATT_BLK // planes
    nb = rows // sub
    if planes == 1:
        view = lambda t: t
        blk_shape = lambda width: (1, 1, ATT_BLK, width)
        cur = lambda b, res, j: (b, res, j, 0)
        prev = lambda b, res, j: (b, res, jnp.maximum(j - 1, 0), 0)
    else:
        view = lambda t: t.reshape(bsz, planes, dilation, rows, t.shape[-1])
        blk_shape = lambda width: (1, planes, 1, sub, width)
        cur = lambda b, res, j: (b, 0, res, j, 0)
        prev = lambda b, res, j: (b, 0, res, jnp.maximum(j - 1, 0), 0)
    spec = lambda im, width=w: pl.BlockSpec(blk_shape(width), im)
    lse_shape = (bsz, n_planes, rows, LANES)
    o, lse = pl.pallas_call(
        functools.partial(_attn_kernel, planes),
        grid=(bsz, dilation, nb),
        in_specs=[spec(cur), spec(prev), spec(cur), spec(prev), spec(cur)],
        out_specs=[spec(cur), spec(cur, LANES)],
        out_shape=[jax.ShapeDtypeStruct(_view_shape(q.shape, planes, dilation), BF16),
                   jax.ShapeDtypeStruct(_view_shape(lse_shape, planes, dilation), F32)],
        compiler_params=pltpu.CompilerParams(
            dimension_semantics=("parallel", "parallel", "arbitrary"),
            vmem_limit_bytes=VMEM_LIMIT),
        name=f"dilated_attn_r{dilation}",
    )(view(q), view(k), view(k), view(v), view(v))
    return o.reshape(q.shape), lse.reshape(lse_shape)


def _view_shape(shape, planes, dilation):
    bsz, _, rows, width = shape
    return shape if planes == 1 else (bsz, planes, dilation, rows, width)


def _outproj_kernel(x_ref, mod_ref, ys_ref, o1_ref, o2_ref, o3_ref, l1_ref, l2_ref, l3_ref,
                    nw_ref, e2_ref, w_ref, out_ref, wide_ref, narrow_ref):
    cls_rows = x_ref.shape[1] // CLS_STRIDE

    def natural(cls_ref, stage_ref):
        tiles = stage_ref.shape[0]
        for res in range(CLS_STRIDE):
            val = cls_ref[0, res].astype(F32)
            for t in range(tiles):
                stage_ref[t, pl.ds(res, cls_rows, stride=CLS_STRIDE), :] = (
                    val[:, t * LANES:(t + 1) * LANES])
        return jnp.concatenate([stage_ref[t] for t in range(tiles)], axis=1)

    l1 = l1_ref[0]
    l2 = natural(l2_ref, narrow_ref)
    l3 = natural(l3_ref, narrow_ref)
    mx = jnp.maximum(jnp.maximum(l1, l2), l3)
    e1, e2, e3 = jnp.exp(l1 - mx), jnp.exp(l2 - mx), jnp.exp(l3 - mx)
    inv = 1.0 / (e1 + e2 + e3)

    def expand(v):
        hi, lo = _split_bf16(v)
        return _dot(jnp.concatenate([hi, lo], axis=1), e2_ref[...])

    o = expand(e1 * inv) * o1_ref[0].astype(F32)
    o = o + expand(e2 * inv) * natural(o2_ref, wide_ref)
    o = o + expand(e3 * inv) * natural(o3_ref, wide_ref)
    ms = jnp.mean(o * o, axis=-1, keepdims=True)
    y_att = (o * lax.rsqrt(ms + EPS) * nw_ref[...]).astype(BF16)
    mix = _dot(ys_ref[0], w_ref[0:D_INNER, :]) + _dot(y_att, w_ref[D_INNER:, :])
    out_ref[0] = x_ref[0] + mod_ref[0, 2:3, :] * mix


def _out_projection(x, mod3, y_ssd, outs, lses, norm_w, e2, w_out, tm):
    bsz, s, d = x.shape
    row = lambda b, i: (b, i, 0)
    wide = pl.BlockSpec((1, tm, D_INNER), row)
    narrow = pl.BlockSpec((1, tm, LANES), row)
    cls = lambda width: pl.BlockSpec((1, CLS_STRIDE, tm // CLS_STRIDE, width),
                                     lambda b, i: (b, 0, i, 0))
    return pl.pallas_call(
        _outproj_kernel,
        grid=(bsz, s // tm),
        in_specs=[
            pl.BlockSpec((1, tm, d), row),
            pl.BlockSpec((1, N_MOD, d), lambda b, i: (b, 0, 0)),
            wide, wide, cls(D_INNER), cls(D_INNER), narrow, cls(LANES), cls(LANES),
            _const_spec((1, D_INNER)),
            _const_spec((2 * LANES, D_INNER)),
            _const_spec((2 * D_INNER, d)),
        ],
        out_specs=pl.BlockSpec((1, tm, d), row),
        out_shape=jax.ShapeDtypeStruct((bsz, s, d), F32),
        scratch_shapes=[pltpu.VMEM((D_INNER // LANES, tm, LANES), F32),
                        pltpu.VMEM((1, tm, LANES), F32)],
        compiler_params=pltpu.CompilerParams(
            dimension_semantics=("parallel", "arbitrary"), vmem_limit_bytes=VMEM_LIMIT),
        name="out_proj",
    )(x, mod3, y_ssd, *outs, *lses, norm_w, e2, w_out)


FF_TN = 1024


def _mlp_kernel(x_ref, mod_ref, nw_ref, w1_ref, w2_ref, out_ref, h_ref):
    x = x_ref[0]
    h = _norm_mod(x, nw_ref[...], mod_ref[0, 4:5, :], mod_ref[0, 3:4, :])
    h_ref[...] = h.astype(BF16)
    acc = jnp.zeros(x.shape, F32)
    for j in range(D_FF // FF_TN):
        fs = slice(j * FF_TN, (j + 1) * FF_TN)
        u = jnp.maximum(_dot(h_ref[...], w1_ref[:, fs]), 0.0)
        acc = acc + _dot((u * u).astype(BF16), w2_ref[fs, :])
    out_ref[0] = x + mod_ref[0, 5:6, :] * acc


def _mlp(x, mod3, norm_w, w1, w2, tm):
    bsz, s, d = x.shape
    row = lambda b, i: (b, i, 0)
    return pl.pallas_call(
        _mlp_kernel,
        grid=(bsz, s // tm),
        in_specs=[
            pl.BlockSpec((1, tm, d), row),
            pl.BlockSpec((1, N_MOD, d), lambda b, i: (b, 0, 0)),
            _const_spec((1, d)),
            _const_spec((d, D_FF)),
            _const_spec((D_FF, d)),
        ],
        out_specs=pl.BlockSpec((1, tm, d), row),
        out_shape=jax.ShapeDtypeStruct((bsz, s, d), F32),
        scratch_shapes=[pltpu.VMEM((tm, d), BF16)],
        compiler_params=pltpu.CompilerParams(
            dimension_semantics=("parallel", "arbitrary"), vmem_limit_bytes=VMEM_LIMIT),
        name="mlp",
    )(x, mod3, norm_w, w1, w2)


def _head_expand_matrix():
    head_of_lane = jnp.arange(D_INNER) // HEAD_DIM
    e = (jnp.arange(LANES)[:, None] == head_of_lane[None, :]).astype(BF16)
    return jnp.concatenate([e, e], axis=0)


def _pad_lanes(v, n=LANES):
    return jnp.pad(v.astype(F32), (0, n - v.shape[0])).reshape(1, n)


def kernel(x, c, norm1_w, norm2_w, w_ada, b_ada, w_in, conv_w, conv_b, dt_bias, a_log, d_skip,
           ssd_norm_w, q_norm_w, k_norm_w, attn_norm_w, w_out, w_ff1, w_ff2):
    bsz, s, d = x.shape
    depth = w_ada.shape[0]
    tm = 512
    e2 = _head_expand_matrix()
    idx = jnp.arange(256) // HEAD_DIM
    bd = (idx[:, None] == idx[None, :]).astype(BF16)
    t = (jnp.arange(SSD_CHUNK)[:, None] >= jnp.arange(SSD_CHUNK)[None, :]).astype(BF16)
    tril2 = jnp.concatenate([t, t], axis=1)
    o_xbc = D_INNER + CONV_CH
    o_dt = o_xbc + N_HEADS

    for l in range(depth):
        mod3 = _modulation(c, w_ada[l], b_ada[l]).reshape(bsz, N_MOD, d)
        wl = w_in[l]
        w_pad = jnp.concatenate(
            [wl[:, :o_xbc], wl[:, o_dt:], wl[:, o_xbc:o_dt],
             jnp.zeros((d, LANES - N_HEADS), wl.dtype)], axis=1).astype(BF16)
        z, xbc, dt, q, k, v, q16, k16, v16 = _in_projection(
            x, mod3, norm1_w[l].reshape(1, d), w_pad,
            jnp.tile(q_norm_w[l], N_HEADS).reshape(1, D_INNER),
            jnp.tile(k_norm_w[l], N_HEADS).reshape(1, D_INNER), bd, tm)
        y_ssd = _ssd_mixer(
            xbc, z, dt, conv_w[l], conv_b[l].reshape(1, CONV_CH), _pad_lanes(dt_bias[l]),
            _pad_lanes(a_log[l]), jnp.repeat(d_skip[l], HEAD_DIM).reshape(1, D_INNER),
            ssd_norm_w[l].reshape(1, D_INNER), tril2, e2)
        outs, lses = [], []
        for window, dilation in DILATED_PATTERNS:
            assert window // dilation == ATT_BLK
            if dilation == 1:
                nat = lambda t: t.reshape(bsz, 1, s, t.shape[-1])
                o, lse = _dilated_attention(nat(q), nat(k), nat(v), 1)
                o, lse = o.reshape(bsz, s, D_INNER), lse.reshape(bsz, s, LANES)
            else:
                o, lse = _dilated_attention(q16, k16, v16, dilation)
            outs.append(o)
            lses.append(lse)
        x = _out_projection(x, mod3, y_ssd, outs, lses, attn_norm_w[l].reshape(1, D_INNER),
                            e2, w_out[l].astype(BF16), tm)
        x = _mlp(x, mod3, norm2_w[l].reshape(1, d), w_ff1[l].astype(BF16),
                 w_ff2[l].astype(BF16), tm)
    return x.astype(c.dtype)
```

```python
import functools

import jax
import jax.numpy as jnp
from jax import lax
from jax.experimental import pallas as pl
from jax.experimental.pallas import tpu as pltpu

F32 = jnp.float32
BF16 = jnp.bfloat16

D_MODEL = 1024
HEAD_DIM = 64
N_HEADS = 16
SSD_GROUPS = 4
SSD_STATE = 128
SSD_CONV = 4
SSD_CHUNK = 128
D_INNER = N_HEADS * HEAD_DIM
CONV_CH = D_INNER + 2 * SSD_GROUPS * SSD_STATE
D_FF = 4 * D_MODEL
N_MOD = 6
EPS = 1e-6
DILATED_PATTERNS = ((128, 1), (512, 4), (2048, 16))
ATT_BLK = 128
CLS_STRIDE = 16
LANES = 128
VMEM_LIMIT = 56 * 1024 * 1024

LOG2E = 1.4426950408889634
LN2 = 0.6931471805599453

NT_DIMS = (((1,), (1,)), ((), ()))
TN_DIMS = (((0,), (0,)), ((), ()))


def _dot(a, b):
    return jnp.dot(a, b, preferred_element_type=F32)


def _split_bf16(v):
    hi = v.astype(BF16)
    lo = (v - hi.astype(F32)).astype(BF16)
    return hi, lo


def _sigmoid(v):
    return 1.0 / (1.0 + jnp.exp(-v))


def _const_spec(shape):
    nd = len(shape)
    return pl.BlockSpec(shape, lambda *_: (0,) * nd, pipeline_mode=pl.Buffered(1))


def _mod_kernel(c_ref, w_ref, b_ref, o_ref):
    c = c_ref[...]
    ca = c * _sigmoid(c)
    c_hi, c_lo = _split_bf16(ca)
    w = w_ref[...]
    w_hi, w_lo = _split_bf16(w)
    acc = _dot(c_hi, w_hi) + _dot(c_lo, w_hi) + _dot(c_hi, w_lo)
    o_ref[...] = acc + b_ref[...]


def _modulation(c, w_ada, b_ada):
    bsz, d = c.shape
    n = w_ada.shape[1]
    tn = 1536
    return pl.pallas_call(
        _mod_kernel,
        grid=(n // tn,),
        in_specs=[
            pl.BlockSpec((bsz, d), lambda j: (0, 0)),
            pl.BlockSpec((d, tn), lambda j: (0, j)),
            pl.BlockSpec((1, tn), lambda j: (0, j)),
        ],
        out_specs=pl.BlockSpec((bsz, tn), lambda j: (0, j)),
        out_shape=jax.ShapeDtypeStruct((bsz, n), F32),
        compiler_params=pltpu.CompilerParams(
            dimension_semantics=("arbitrary",), vmem_limit_bytes=VMEM_LIMIT),
        name="adaln_mod",
    )(c, w_ada, b_ada.reshape(1, n))


IN_TN = 512


def _norm_mod(x, nw, scale, shift):
    ms = jnp.mean(x * x, axis=-1, keepdims=True)
    return (x * lax.rsqrt(ms + EPS) * nw) * (1.0 + scale) + shift


def _inproj_kernel(x_ref, mod_ref, nw_ref, w_ref, qw_ref, kw_ref, bd_ref,
                   z_ref, xbc_ref, dt_ref, q_ref, k_ref, v_ref, q16_ref, k16_ref, v16_ref,
                   h_ref, stage_ref, stage2_ref):
    x = x_ref[0]
    tm = x.shape[0]
    h = _norm_mod(x, nw_ref[...], mod_ref[0, 1:2, :], mod_ref[0, 0:1, :])
    h_ref[...] = h.astype(BF16)

    def proj(c0, width):
        return _dot(h_ref[...], w_ref[:, c0:c0 + width])

    def qk_norm(acc, w):
        sq = (acc * acc).astype(BF16)
        parts = [_dot(sq[:, i:i + 256], bd_ref[...]) for i in range(0, IN_TN, 256)]
        ss = jnp.concatenate(parts, axis=1)
        return acc * lax.rsqrt(ss * (1.0 / HEAD_DIM) + EPS) * w

    cls_rows = x.shape[0] // CLS_STRIDE

    def emit(nat_ref, cls_ref, sl, val):
        nat_ref[0, :, sl] = val.astype(BF16)
        tiles = IN_TN // LANES
        quarter = tm // 4
        for t in range(tiles):
            stage_ref[t] = val[:, t * LANES:(t + 1) * LANES]
        for t in range(tiles):
            for rho in range(4):
                stage2_ref[t, rho * quarter:(rho + 1) * quarter, :] = (
                    stage_ref[t, pl.ds(rho, quarter, stride=4), :])
        for res in range(CLS_STRIDE):
            rho, a = res % 4, res // 4
            rows = [stage2_ref[t, pl.ds(rho * quarter + a, cls_rows, stride=4), :]
                    for t in range(tiles)]
            cls_ref[0, res, :, sl] = jnp.concatenate(rows, axis=1).astype(BF16)

    col = 0
    for j in range(D_INNER // IN_TN):
        z_ref[0, :, j * IN_TN:(j + 1) * IN_TN] = proj(col, IN_TN).astype(BF16)
        col += IN_TN
    for j in range(CONV_CH // IN_TN):
        xbc_ref[0, :, j * IN_TN:(j + 1) * IN_TN] = proj(col, IN_TN).astype(BF16)
        col += IN_TN
    for j in range(D_INNER // IN_TN):
        sl = slice(j * IN_TN, (j + 1) * IN_TN)
        emit(q_ref, q16_ref, sl,
             qk_norm(proj(col, IN_TN), qw_ref[:, sl]) * (HEAD_DIM ** -0.5 * LOG2E))
        col += IN_TN
    for j in range(D_INNER // IN_TN):
        sl = slice(j * IN_TN, (j + 1) * IN_TN)
        emit(k_ref, k16_ref, sl, qk_norm(proj(col, IN_TN), kw_ref[:, sl]))
        col += IN_TN
    for j in range(D_INNER // IN_TN):
        emit(v_ref, v16_ref, slice(j * IN_TN, (j + 1) * IN_TN), proj(col, IN_TN))
        col += IN_TN
    dt_ref[0] = proj(col, LANES)


def _in_projection(x, mod3, norm_w, w_pad, qw, kw, bd, tm):
    bsz, s, d = x.shape
    wcols = w_pad.shape[1]
    row = lambda b, i: (b, i, 0)
    out_bf = lambda n: jax.ShapeDtypeStruct((bsz, s, n), BF16)
    cls_spec = pl.BlockSpec((1, CLS_STRIDE, tm // CLS_STRIDE, D_INNER), lambda b, i: (b, 0, i, 0))
    cls_shape = jax.ShapeDtypeStruct((bsz, CLS_STRIDE, s // CLS_STRIDE, D_INNER), BF16)
    return pl.pallas_call(
        _inproj_kernel,
        grid=(bsz, s // tm),
        in_specs=[
            pl.BlockSpec((1, tm, d), row),
            pl.BlockSpec((1, N_MOD, d), lambda b, i: (b, 0, 0)),
            _const_spec((1, d)),
            _const_spec((d, wcols)),
            _const_spec((1, D_INNER)),
            _const_spec((1, D_INNER)),
            _const_spec((256, 256)),
        ],
        out_specs=[
            pl.BlockSpec((1, tm, D_INNER), row),
            pl.BlockSpec((1, tm, CONV_CH), row),
            pl.BlockSpec((1, tm, LANES), row),
            pl.BlockSpec((1, tm, D_INNER), row),
            pl.BlockSpec((1, tm, D_INNER), row),
            pl.BlockSpec((1, tm, D_INNER), row),
            cls_spec, cls_spec, cls_spec,
        ],
        out_shape=[out_bf(D_INNER), out_bf(CONV_CH),
                   jax.ShapeDtypeStruct((bsz, s, LANES), F32),
                   out_bf(D_INNER), out_bf(D_INNER), out_bf(D_INNER),
                   cls_shape, cls_shape, cls_shape],
        scratch_shapes=[pltpu.VMEM((tm, d), BF16),
                        pltpu.VMEM((IN_TN // LANES, tm, LANES), F32),
                        pltpu.VMEM((IN_TN // LANES, tm, LANES), F32)],
        compiler_params=pltpu.CompilerParams(
            dimension_semantics=("parallel", "arbitrary"), vmem_limit_bytes=VMEM_LIMIT),
        name="in_proj",
    )(x, mod3, norm_w, w_pad, qw, kw, bd)


SSD_ROWS = 256
CONV_HALO = 8


def _ssd_kernel(xbc_ref, z_ref, dt_ref, cw_ref, cb_ref, dtb_ref, alog_ref, dskip_ref,
                nw_ref, tril2_ref, e2_ref, shift_ref, o_ref, halo_ref, xc_ref, state_ref):
    rows = xbc_ref.shape[1]
    L = SSD_CHUNK
    taps = SSD_CONV - 1

    @pl.when(pl.program_id(1) == 0)
    def _():
        halo_ref[...] = jnp.zeros_like(halo_ref)
        state_ref[...] = jnp.zeros_like(state_ref)

    hsub = lax.broadcasted_iota(jnp.int32, (CONV_HALO, 256), 0)

    def conv_chunk(c):
        r0 = c * L
        for c0 in range(0, CONV_CH, 256):
            cs = slice(c0, c0 + 256)
            u = xbc_ref[0, r0:r0 + L, cs]
            shifted = _dot(shift_ref[...], u)
            uf = u.astype(F32)
            acc = cb_ref[:, cs] + cw_ref[taps:taps + 1, cs] * uf
            for k in range(1, taps + 1):
                acc = acc + cw_ref[taps - k:taps - k + 1, cs] * shifted[(k - 1) * L:k * L]
            xc_ref[r0:r0 + L, cs] = acc * _sigmoid(acc)
            if c == 0:
                halo = halo_ref[:, cs]
            else:
                halo = xbc_ref[0, r0 - 2 * CONV_HALO:r0, cs].astype(F32)[CONV_HALO:]
            head = acc[0:CONV_HALO]
            for k in range(1, taps + 1):
                prev = jnp.where(hsub < k, pltpu.roll(halo, k, axis=0), 0.0)
                head = head + cw_ref[taps - k:taps - k + 1, cs] * prev
            xc_ref[r0:r0 + CONV_HALO, cs] = head * _sigmoid(head)
            if r0 + L == rows:
                halo_ref[:, cs] = uf[L - CONV_HALO:L]

    lane = lax.broadcasted_iota(jnp.int32, (L, LANES), 1)
    sub = lax.broadcasted_iota(jnp.int32, (L, LANES), 0)
    tril = sub >= lane
    lo_half = lane < HEAD_DIM
    a_neg = jnp.where(lane[0:1] < N_HEADS, -jnp.exp(alog_ref[...]), 0.0)

    def expand(v):
        hi, lo = _split_bf16(v)
        return _dot(jnp.concatenate([hi, lo], axis=1), e2_ref[...])

    for c in range(rows // L):
        conv_chunk(c)
        r0 = c * L
        rs = slice(r0, r0 + L)
        dt_raw = dt_ref[0, rs, :] + dtb_ref[...]
        dt = jnp.maximum(dt_raw, 0.0) + jnp.log(1.0 + jnp.exp(-jnp.abs(dt_raw)))
        d_a = dt * a_neg
        da_hi, da_lo = _split_bf16(d_a)
        a_cs = _dot(tril2_ref[...], jnp.concatenate([da_hi, da_lo], axis=0))
        a_cs_t = a_cs.T
        a_last = a_cs[L - 1:L, :]
        dec_in = jnp.exp(a_last - a_cs)
        dec_out = jnp.exp(a_cs)
        dt_x = expand(dt)
        w_x = expand(dt * dec_in)
        do_x = expand(dec_out)

        xs = xc_ref[rs, 0:D_INNER]
        xdt = (xs * dt_x).astype(BF16)
        xw = (xs * w_x).astype(BF16)
        y = dskip_ref[...] * xs
        z = z_ref[0, rs, :].astype(F32)
        gate = z * _sigmoid(z)

        for g in range(SSD_GROUPS):
            b0 = D_INNER + g * SSD_STATE
            c0 = D_INNER + SSD_GROUPS * SSD_STATE + g * SSD_STATE
            bg = xc_ref[rs, b0:b0 + SSD_STATE].astype(BF16)
            cg = xc_ref[rs, c0:c0 + SSD_STATE].astype(BF16)
            cb = lax.dot_general(cg, bg, NT_DIMS, preferred_element_type=F32)
            gs = slice(g * 256, (g + 1) * 256)
            yd_parts = []
            for pair in range(2):
                ms = []
                for e in range(2):
                    hd = g * 4 + pair * 2 + e
                    colb = jnp.broadcast_to(a_cs[:, hd:hd + 1], (L, L))
                    rowb = jnp.broadcast_to(a_cs_t[hd:hd + 1, :], (L, L))
                    lmat = jnp.exp(jnp.where(tril, colb - rowb, -jnp.inf))
                    ms.append((cb * lmat).astype(BF16))
                mcat = jnp.concatenate(ms, axis=1)
                p0 = g * 256 + pair * LANES
                xp = xdt[:, p0:p0 + LANES]
                zero = jnp.zeros_like(xp)
                xstack = jnp.concatenate(
                    [jnp.where(lo_half, xp, zero), jnp.where(lo_half, zero, xp)], axis=0)
                yd_parts.append(_dot(mcat, xstack))
            y_diag = jnp.concatenate(yd_parts, axis=1)
            st = state_ref[:, gs]
            y_off = _dot(cg, st.astype(BF16)) * do_x[:, gs]
            upd = lax.dot_general(bg, xw[:, gs], TN_DIMS, preferred_element_type=F32)
            state_ref[:, gs] = st * do_x[L - 1:L, gs] + upd
            yg = (y[:, gs] + y_diag + y_off) * gate[:, gs]
            ss = jnp.mean(yg * yg, axis=-1, keepdims=True)
            o_ref[0, rs, gs] = (yg * lax.rsqrt(ss + EPS) * nw_ref[:, gs]).astype(BF16)


def _ssd_mixer(xbc, z, dt, conv_w, conv_b, dt_bias, a_log, d_skip, norm_w, tril2, e2):
    bsz, s, _ = xbc.shape
    rows = SSD_ROWS
    row = lambda b, i: (b, i, 0)
    t = jnp.arange(SSD_CHUNK)
    shift = jnp.concatenate(
        [(t[:, None] - k == t[None, :]).astype(BF16) for k in range(1, SSD_CONV)], axis=0)
    return pl.pallas_call(
        _ssd_kernel,
        grid=(bsz, s // rows),
        in_specs=[
            pl.BlockSpec((1, rows, CONV_CH), row),
            pl.BlockSpec((1, rows, D_INNER), row),
            pl.BlockSpec((1, rows, LANES), row),
            _const_spec((SSD_CONV, CONV_CH)),
            _const_spec((1, CONV_CH)),
            _const_spec((1, LANES)),
            _const_spec((1, LANES)),
            _const_spec((1, D_INNER)),
            _const_spec((1, D_INNER)),
            _const_spec((SSD_CHUNK, 2 * SSD_CHUNK)),
            _const_spec((2 * LANES, D_INNER)),
            _const_spec(((SSD_CONV - 1) * SSD_CHUNK, SSD_CHUNK)),
        ],
        out_specs=pl.BlockSpec((1, rows, D_INNER), row),
        out_shape=jax.ShapeDtypeStruct((bsz, s, D_INNER), BF16),
        scratch_shapes=[
            pltpu.VMEM((CONV_HALO, CONV_CH), F32),
            pltpu.VMEM((rows, CONV_CH), F32),
            pltpu.VMEM((SSD_STATE, D_INNER), F32),
        ],
        compiler_params=pltpu.CompilerParams(
            dimension_semantics=("parallel", "arbitrary"), vmem_limit_bytes=VMEM_LIMIT),
        name="ssd_mixer",
    )(xbc, z, dt, conv_w, conv_b, dt_bias, a_log, d_skip, norm_w, tril2, e2, shift)


def _attn_kernel(planes, q_ref, kp_ref, kc_ref, vp_ref, vc_ref, o_ref, lse_ref):
    blk = ATT_BLK
    sub = blk // planes
    first = pl.program_id(2) == 0

    def pos(i):
        return i if planes == 1 else (i % sub) * planes + i // sub

    row = lax.broadcasted_iota(jnp.int32, (blk, 2 * blk), 0)
    key = lax.broadcasted_iota(jnp.int32, (blk, 2 * blk), 1)
    rel = jnp.where(key >= blk, blk + pos(key - blk), pos(key)) - pos(row)
    lo_key = jnp.where(first, blk, 0)
    bias = jnp.where((rel >= 0) & (rel <= blk) & (key >= lo_key), 0.0, -jnp.inf)
    bias2 = jnp.concatenate([bias, bias], axis=0)
    lane = lax.broadcasted_iota(jnp.int32, (blk, LANES), 1)
    lo_half = lane < HEAD_DIM

    def load(ref, ps):
        if planes == 1:
            return ref[0, 0, :, ps]
        return ref[0, :, 0, :, ps].reshape(blk, LANES)

    def store(ref, ps, val):
        if planes == 1:
            ref[0, 0, :, ps] = val
        else:
            ref[0, :, 0, :, ps] = val.reshape(planes, sub, val.shape[-1])

    store(lse_ref, slice(0, LANES), jnp.zeros((blk, LANES), F32))
    for p in range(N_HEADS // 2):
        ps = slice(p * LANES, (p + 1) * LANES)
        qp = load(q_ref, ps)
        zero = jnp.zeros_like(qp)
        qs = jnp.concatenate([jnp.where(lo_half, qp, zero), jnp.where(lo_half, zero, qp)], axis=0)
        k2 = jnp.concatenate([load(kp_ref, ps), load(kc_ref, ps)], axis=0)
        v2 = jnp.concatenate([load(vp_ref, ps), load(vc_ref, ps)], axis=0)
        s = lax.dot_general(qs, k2, NT_DIMS, preferred_element_type=F32) + bias2
        m = jnp.max(s, axis=-1, keepdims=True)
        e = jnp.exp2(s - m)
        l = jnp.sum(e, axis=-1, keepdims=True)
        pv = _dot(e.astype(BF16), v2)
        inv = 1.0 / l
        o_pair = jnp.where(lo_half, pv[0:blk] * inv[0:blk], pv[blk:] * inv[blk:])
        store(o_ref, ps, o_pair.astype(BF16))
        lse = (m + jnp.log2(l)) * LN2
        store(lse_ref, slice(2 * p, 2 * p + 1), lse[0:blk])
        store(lse_ref, slice(2 * p + 1, 2 * p + 2), lse[blk:])


def _dilated_attention(q, k, v, dilation):
    bsz, n_planes, rows, w = q.shape
    planes = n_planes // dilation
    assert planes in (1, 4)
    sub = ATT_BLK // planes
    nb = rows // sub
    if planes == 1:
        view = lambda t: t
        blk_shape = lambda width: (1, 1, ATT_BLK, width)
        cur = lambda b, res, j: (b, res, j, 0)
        prev = lambda b, res, j: (b, res, jnp.maximum(j - 1, 0), 0)
    else:
        view = lambda t: t.reshape(bsz, planes, dilation, rows, t.shape[-1])
        blk_shape = lambda width: (1, planes, 1, sub, width)
        cur = lambda b, res, j: (b, 0, res, j, 0)
        prev = lambda b, res, j: (b, 0, res, jnp.maximum(j - 1, 0), 0)
    spec = lambda im, width=w: pl.BlockSpec(blk_shape(width), im)
    lse_shape = (bsz, n_planes, rows, LANES)
    o, lse = pl.pallas_call(
        functools.partial(_attn_kernel, planes),
        grid=(bsz, dilation, nb),
        in_specs=[spec(cur), spec(prev), spec(cur), spec(prev), spec(cur)],
        out_specs=[spec(cur), spec(cur, LANES)],
        out_shape=[jax.ShapeDtypeStruct(_view_shape(q.shape, planes, dilation), BF16),
                   jax.ShapeDtypeStruct(_view_shape(lse_shape, planes, dilation), F32)],
        compiler_params=pltpu.CompilerParams(
            dimension_semantics=("parallel", "parallel", "arbitrary"),
            vmem_limit_bytes=VMEM_LIMIT),
        name=f"dilated_attn_r{dilation}",
    )(view(q), view(k), view(k), view(v), view(v))
    return o.reshape(q.shape), lse.reshape(lse_shape)


def _view_shape(shape, planes, dilation):
    bsz, _, rows, width = shape
    return shape if planes == 1 else (bsz, planes, dilation, rows, width)


def _outproj_kernel(x_ref, mod_ref, ys_ref, o1_ref, o2_ref, o3_ref, l1_ref, l2_ref, l3_ref,
                    nw_ref, e2_ref, w_ref, out_ref, wide_ref, wide2_ref, narrow_ref, narrow2_ref):
    cls_rows = x_ref.shape[1] // CLS_STRIDE

    def natural(cls_ref, stage_ref, stage2_ref):
        tiles = stage_ref.shape[0]
        quarter = cls_rows * 4
        for res in range(CLS_STRIDE):
            rho, a = res % 4, res // 4
            val = cls_ref[0, res].astype(F32)
            for t in range(tiles):
                stage2_ref[t, pl.ds(rho * quarter + a, cls_rows, stride=4), :] = (
                    val[:, t * LANES:(t + 1) * LANES])
        for t in range(tiles):
            for rho in range(4):
                stage_ref[t, pl.ds(rho, quarter, stride=4), :] = (
                    stage2_ref[t, rho * quarter:(rho + 1) * quarter, :])
        return jnp.concatenate([stage_ref[t] for t in range(tiles)], axis=1)

    l1 = l1_ref[0]
    l2 = natural(l2_ref, narrow_ref, narrow2_ref)
    l3 = natural(l3_ref, narrow_ref, narrow2_ref)
    mx = jnp.maximum(jnp.maximum(l1, l2), l3)
    e1, e2, e3 = jnp.exp(l1 - mx), jnp.exp(l2 - mx), jnp.exp(l3 - mx)
    inv = 1.0 / (e1 + e2 + e3)

    def expand(v):
        hi, lo = _split_bf16(v)
        return _dot(jnp.concatenate([hi, lo], axis=1), e2_ref[...])

    o = expand(e1 * inv) * o1_ref[0].astype(F32)
    o = o + expand(e2 * inv) * natural(o2_ref, wide_ref, wide2_ref)
    o = o + expand(e3 * inv) * natural(o3_ref, wide_ref, wide2_ref)
    ms = jnp.mean(o * o, axis=-1, keepdims=True)
    y_att = (o * lax.rsqrt(ms + EPS) * nw_ref[...]).astype(BF16)
    mix = _dot(ys_ref[0], w_ref[0:D_INNER, :]) + _dot(y_att, w_ref[D_INNER:, :])
    out_ref[0] = x_ref[0] + mod_ref[0, 2:3, :] * mix


def _out_projection(x, mod3, y_ssd, outs, lses, norm_w, e2, w_out, tm):
    bsz, s, d = x.shape
    row = lambda b, i: (b, i, 0)
    wide = pl.BlockSpec((1, tm, D_INNER), row)
    narrow = pl.BlockSpec((1, tm, LANES), row)
    cls = lambda width: pl.BlockSpec((1, CLS_STRIDE, tm // CLS_STRIDE, width),
                                     lambda b, i: (b, 0, i, 0))
    return pl.pallas_call(
        _outproj_kernel,
        grid=(bsz, s // tm),
        in_specs=[
            pl.BlockSpec((1, tm, d), row),
            pl.BlockSpec((1, N_MOD, d), lambda b, i: (b, 0, 0)),
            wide, wide, cls(D_INNER), cls(D_INNER), narrow, cls(LANES), cls(LANES),
            _const_spec((1, D_INNER)),
            _const_spec((2 * LANES, D_INNER)),
            _const_spec((2 * D_INNER, d)),
        ],
        out_specs=pl.BlockSpec((1, tm, d), row),
        out_shape=jax.ShapeDtypeStruct((bsz, s, d), F32),
        scratch_shapes=[pltpu.VMEM((D_INNER // LANES, tm, LANES), F32),
                        pltpu.VMEM((D_INNER // LANES, tm, LANES), F32),
                        pltpu.VMEM((1, tm, LANES), F32),
                        pltpu.VMEM((1, tm, LANES), F32)],
        compiler_params=pltpu.CompilerParams(
            dimension_semantics=("parallel", "arbitrary"), vmem_limit_bytes=VMEM_LIMIT),
        name="out_proj",
    )(x, mod3, y_ssd, *outs, *lses, norm_w, e2, w_out)


FF_TN = 1024


def _mlp_kernel(x_ref, mod_ref, nw_ref, w1_ref, w2_ref, out_ref, h_ref):
    x = x_ref[0]
    h = _norm_mod(x, nw_ref[...], mod_ref[0, 4:5, :], mod_ref[0, 3:4, :])
    h_ref[...] = h.astype(BF16)
    acc = jnp.zeros(x.shape, F32)
    for j in range(D_FF // FF_TN):
        fs = slice(j * FF_TN, (j + 1) * FF_TN)
        u = jnp.maximum(_dot(h_ref[...], w1_ref[:, fs]), 0.0)
        acc = acc + _dot((u * u).astype(BF16), w2_ref[fs, :])
    out_ref[0] = x + mod_ref[0, 5:6, :] * acc


def _mlp(x, mod3, norm_w, w1, w2, tm):
    bsz, s, d = x.shape
    row = lambda b, i: (b, i, 0)
    return pl.pallas_call(
        _mlp_kernel,
        grid=(bsz, s // tm),
        in_specs=[
            pl.BlockSpec((1, tm, d), row),
            pl.BlockSpec((1, N_MOD, d), lambda b, i: (b, 0, 0)),
            _const_spec((1, d)),
            _const_spec((d, D_FF)),
            _const_spec((D_FF, d)),
        ],
        out_specs=pl.BlockSpec((1, tm, d), row),
        out_shape=jax.ShapeDtypeStruct((bsz, s, d), F32),
        scratch_shapes=[pltpu.VMEM((tm, d), BF16)],
        compiler_params=pltpu.CompilerParams(
            dimension_semantics=("parallel", "arbitrary"), vmem_limit_bytes=VMEM_LIMIT),
        name="mlp",
    )(x, mod3, norm_w, w1, w2)


def _head_expand_matrix():
    head_of_lane = jnp.arange(D_INNER) // HEAD_DIM
    e = (jnp.arange(LANES)[:, None] == head_of_lane[None, :]).astype(BF16)
    return jnp.concatenate([e, e], axis=0)


def _pad_lanes(v, n=LANES):
    return jnp.pad(v.astype(F32), (0, n - v.shape[0])).reshape(1, n)


def kernel(x, c, norm1_w, norm2_w, w_ada, b_ada, w_in, conv_w, conv_b, dt_bias, a_log, d_skip,
           ssd_norm_w, q_norm_w, k_norm_w, attn_norm_w, w_out, w_ff1, w_ff2):
    bsz, s, d = x.shape
    depth = w_ada.shape[0]
    tm = 512
    e2 = _head_expand_matrix()
    idx = jnp.arange(256) // HEAD_DIM
    bd = (idx[:, None] == idx[None, :]).astype(BF16)
    t = (jnp.arange(SSD_CHUNK)[:, None] >= jnp.arange(SSD_CHUNK)[None, :]).astype(BF16)
    tril2 = jnp.concatenate([t, t], axis=1)
    o_xbc = D_INNER + CONV_CH
    o_dt = o_xbc + N_HEADS

    for l in range(depth):
        mod3 = _modulation(c, w_ada[l], b_ada[l]).reshape(bsz, N_MOD, d)
        wl = w_in[l]
        w_pad = jnp.concatenate(
            [wl[:, :o_xbc], wl[:, o_dt:], wl[:, o_xbc:o_dt],
             jnp.zeros((d, LANES - N_HEADS), wl.dtype)], axis=1).astype(BF16)
        z, xbc, dt, q, k, v, q16, k16, v16 = _in_projection(
            x, mod3, norm1_w[l].reshape(1, d), w_pad,
            jnp.tile(q_norm_w[l], N_HEADS).reshape(1, D_INNER),
            jnp.tile(k_norm_w[l], N_HEADS).reshape(1, D_INNER), bd, tm)
        y_ssd = _ssd_mixer(
            xbc, z, dt, conv_w[l], conv_b[l].reshape(1, CONV_CH), _pad_lanes(dt_bias[l]),
            _pad_lanes(a_log[l]), jnp.repeat(d_skip[l], HEAD_DIM).reshape(1, D_INNER),
            ssd_norm_w[l].reshape(1, D_INNER), tril2, e2)
        outs, lses = [], []
        for window, dilation in DILATED_PATTERNS:
            assert window // dilation == ATT_BLK
            if dilation == 1:
                nat = lambda t: t.reshape(bsz, 1, s, t.shape[-1])
                o, lse = _dilated_attention(nat(q), nat(k), nat(v), 1)
                o, lse = o.reshape(bsz, s, D_INNER), lse.reshape(bsz, s, LANES)
            else:
                o, lse = _dilated_attention(q16, k16, v16, dilation)
            outs.append(o)
            lses.append(lse)
        x = _out_projection(x, mod3, y_ssd, outs, lses, attn_norm_w[l].reshape(1, D_INNER),
                            e2, w_out[l].astype(BF16), tm)
        x = _mlp(x, mod3, norm2_w[l].reshape(1, d), w_ff1[l].astype(BF16),
                 w_ff2[l].astype(BF16), tm)
    return x.astype(c.dtype)
```

```python
import functools

import jax
import jax.numpy as jnp
from jax import lax
from jax.experimental import pallas as pl
from jax.experimental.pallas import tpu as pltpu

F32 = jnp.float32
BF16 = jnp.bfloat16

D_MODEL = 1024
HEAD_DIM = 64
N_HEADS = 16
SSD_GROUPS = 4
SSD_STATE = 128
SSD_CONV = 4
SSD_CHUNK = 128
D_INNER = N_HEADS * HEAD_DIM
CONV_CH = D_INNER + 2 * SSD_GROUPS * SSD_STATE
D_FF = 4 * D_MODEL
N_MOD = 6
EPS = 1e-6
DILATED_PATTERNS = ((128, 1), (512, 4), (2048, 16))
ATT_BLK = 128
ATT_GROUP = 4
CLS_STRIDE = 16
LANES = 128
VMEM_LIMIT = 56 * 1024 * 1024

LOG2E = 1.4426950408889634
LN2 = 0.6931471805599453
SCORE_BOUND = 60.0
BF16_SLACK = 1.02

NT_DIMS = (((1,), (1,)), ((), ()))
TN_DIMS = (((0,), (0,)), ((), ()))


def _dot(a, b):
    return jnp.dot(a, b, preferred_element_type=F32)


def _split_bf16(v):
    hi = v.astype(BF16)
    lo = (v - hi.astype(F32)).astype(BF16)
    return hi, lo


def _sigmoid(v):
    return 1.0 / (1.0 + jnp.exp(-v))


def _const_spec(shape):
    nd = len(shape)
    return pl.BlockSpec(shape, lambda *_: (0,) * nd, pipeline_mode=pl.Buffered(1))


def _mod_kernel(c_ref, w_ref, b_ref, o_ref):
    c = c_ref[...]
    ca = c * _sigmoid(c)
    c_hi, c_lo = _split_bf16(ca)
    w = w_ref[...]
    w_hi, w_lo = _split_bf16(w)
    acc = _dot(c_hi, w_hi) + _dot(c_lo, w_hi) + _dot(c_hi, w_lo)
    o_ref[...] = acc + b_ref[...]


def _modulation(c, w_ada, b_ada):
    bsz, d = c.shape
    n = w_ada.shape[1]
    tn = 1536
    return pl.pallas_call(
        _mod_kernel,
        grid=(n // tn,),
        in_specs=[
            pl.BlockSpec((bsz, d), lambda j: (0, 0)),
            pl.BlockSpec((d, tn), lambda j: (0, j)),
            pl.BlockSpec((1, tn), lambda j: (0, j)),
        ],
        out_specs=pl.BlockSpec((bsz, tn), lambda j: (0, j)),
        out_shape=jax.ShapeDtypeStruct((bsz, n), F32),
        compiler_params=pltpu.CompilerParams(
            dimension_semantics=("arbitrary",), vmem_limit_bytes=VMEM_LIMIT),
        name="adaln_mod",
    )(c, w_ada, b_ada.reshape(1, n))


IN_TN = 512


def _norm_mod(x, nw, scale, shift):
    ms = jnp.mean(x * x, axis=-1, keepdims=True)
    return (x * lax.rsqrt(ms + EPS) * nw) * (1.0 + scale) + shift


def _inproj_kernel(x_ref, mod_ref, nw_ref, w_ref, qw_ref, kw_ref, bd_ref,
                   z_ref, xbc_ref, dt_ref, q_ref, k_ref, v_ref, q16_ref, k16_ref, v16_ref,
                   h_ref, stage_ref, stage2_ref):
    x = x_ref[0]
    tm = x.shape[0]
    h = _norm_mod(x, nw_ref[...], mod_ref[0, 1:2, :], mod_ref[0, 0:1, :])
    h_ref[...] = h.astype(BF16)

    def proj(c0, width):
        return _dot(h_ref[...], w_ref[:, c0:c0 + width])

    def qk_norm(acc, w):
        sq = (acc * acc).astype(BF16)
        parts = [_dot(sq[:, i:i + 256], bd_ref[...]) for i in range(0, IN_TN, 256)]
        ss = jnp.concatenate(parts, axis=1)
        return acc * lax.rsqrt(ss * (1.0 / HEAD_DIM) + EPS) * w

    cls_rows = x.shape[0] // CLS_STRIDE

    def emit(nat_ref, cls_ref, sl, val):
        nat_ref[0, :, sl] = val.astype(BF16)
        tiles = IN_TN // LANES
        quarter = tm // 4
        for t in range(tiles):
            stage_ref[t] = val[:, t * LANES:(t + 1) * LANES]
        for t in range(tiles):
            for rho in range(4):
                stage2_ref[t, rho * quarter:(rho + 1) * quarter, :] = (
                    stage_ref[t, pl.ds(rho, quarter, stride=4), :])
        for res in range(CLS_STRIDE):
            rho, a = res % 4, res // 4
            rows = [stage2_ref[t, pl.ds(rho * quarter + a, cls_rows, stride=4), :]
                    for t in range(tiles)]
            cls_ref[0, res, :, sl] = jnp.concatenate(rows, axis=1).astype(BF16)

    col = 0
    for j in range(D_INNER // IN_TN):
        z_ref[0, :, j * IN_TN:(j + 1) * IN_TN] = proj(col, IN_TN).astype(BF16)
        col += IN_TN
    for j in range(CONV_CH // IN_TN):
        xbc_ref[0, :, j * IN_TN:(j + 1) * IN_TN] = proj(col, IN_TN).astype(BF16)
        col += IN_TN
    for j in range(D_INNER // IN_TN):
        sl = slice(j * IN_TN, (j + 1) * IN_TN)
        emit(q_ref, q16_ref, sl,
             qk_norm(proj(col, IN_TN), qw_ref[:, sl]) * (HEAD_DIM ** -0.5 * LOG2E))
        col += IN_TN
    for j in range(D_INNER // IN_TN):
        sl = slice(j * IN_TN, (j + 1) * IN_TN)
        emit(k_ref, k16_ref, sl, qk_norm(proj(col, IN_TN), kw_ref[:, sl]))
        col += IN_TN
    for j in range(D_INNER // IN_TN):
        emit(v_ref, v16_ref, slice(j * IN_TN, (j + 1) * IN_TN), proj(col, IN_TN))
        col += IN_TN
    dt_ref[0] = proj(col, LANES)


def _in_projection(x, mod3, norm_w, w_pad, qw, kw, bd, tm):
    bsz, s, d = x.shape
    wcols = w_pad.shape[1]
    row = lambda b, i: (b, i, 0)
    out_bf = lambda n: jax.ShapeDtypeStruct((bsz, s, n), BF16)
    cls_spec = pl.BlockSpec((1, CLS_STRIDE, tm // CLS_STRIDE, D_INNER), lambda b, i: (b, 0, i, 0))
    cls_shape = jax.ShapeDtypeStruct((bsz, CLS_STRIDE, s // CLS_STRIDE, D_INNER), BF16)
    return pl.pallas_call(
        _inproj_kernel,
        grid=(bsz, s // tm),
        in_specs=[
            pl.BlockSpec((1, tm, d), row),
            pl.BlockSpec((1, N_MOD, d), lambda b, i: (b, 0, 0)),
            _const_spec((1, d)),
            _const_spec((d, wcols)),
            _const_spec((1, D_INNER)),
            _const_spec((1, D_INNER)),
            _const_spec((256, 256)),
        ],
        out_specs=[
            pl.BlockSpec((1, tm, D_INNER), row),
            pl.BlockSpec((1, tm, CONV_CH), row),
            pl.BlockSpec((1, tm, LANES), row),
            pl.BlockSpec((1, tm, D_INNER), row),
            pl.BlockSpec((1, tm, D_INNER), row),
            pl.BlockSpec((1, tm, D_INNER), row),
            cls_spec, cls_spec, cls_spec,
        ],
        out_shape=[out_bf(D_INNER), out_bf(CONV_CH),
                   jax.ShapeDtypeStruct((bsz, s, LANES), F32),
                   out_bf(D_INNER), out_bf(D_INNER), out_bf(D_INNER),
                   cls_shape, cls_shape, cls_shape],
        scratch_shapes=[pltpu.VMEM((tm, d), BF16),
                        pltpu.VMEM((IN_TN // LANES, tm, LANES), F32),
                        pltpu.VMEM((IN_TN // LANES, tm, LANES), F32)],
        compiler_params=pltpu.CompilerParams(
            dimension_semantics=("parallel", "arbitrary"), vmem_limit_bytes=VMEM_LIMIT),
        name="in_proj",
    )(x, mod3, norm_w, w_pad, qw, kw, bd)


SSD_ROWS = 512
CONV_HALO = 8


def _ssd_kernel(xbc_ref, z_ref, dt_ref, cw_ref, cb_ref, dtb_ref, alog_ref, dskip_ref,
                nw_ref, tril2_ref, e2_ref, shift_ref, o_ref, halo_ref, xc_ref, state_ref):
    rows = xbc_ref.shape[1]
    L = SSD_CHUNK
    taps = SSD_CONV - 1

    @pl.when(pl.program_id(1) == 0)
    def _():
        halo_ref[...] = jnp.zeros_like(halo_ref)
        state_ref[...] = jnp.zeros_like(state_ref)

    hsub = lax.broadcasted_iota(jnp.int32, (CONV_HALO, 256), 0)

    def conv_chunk(c):
        r0 = c * L
        for c0 in range(0, CONV_CH, 256):
            cs = slice(c0, c0 + 256)
            u = xbc_ref[0, r0:r0 + L, cs]
            shifted = _dot(shift_ref[...], u)
            uf = u.astype(F32)
            acc = cb_ref[:, cs] + cw_ref[taps:taps + 1, cs] * uf
            for k in range(1, taps + 1):
                acc = acc + cw_ref[taps - k:taps - k + 1, cs] * shifted[(k - 1) * L:k * L]
            xc_ref[r0:r0 + L, cs] = acc * _sigmoid(acc)
            if c == 0:
                halo = halo_ref[:, cs]
            else:
                halo = xbc_ref[0, r0 - 2 * CONV_HALO:r0, cs].astype(F32)[CONV_HALO:]
            head = acc[0:CONV_HALO]
            for k in range(1, taps + 1):
                prev = jnp.where(hsub < k, pltpu.roll(halo, k, axis=0), 0.0)
                head = head + cw_ref[taps - k:taps - k + 1, cs] * prev
            xc_ref[r0:r0 + CONV_HALO, cs] = head * _sigmoid(head)
            if r0 + L == rows:
                halo_ref[:, cs] = uf[L - CONV_HALO:L]

    lane = lax.broadcasted_iota(jnp.int32, (L, LANES), 1)
    sub = lax.broadcasted_iota(jnp.int32, (L, LANES), 0)
    tril = sub >= lane
    lo_half = lane < HEAD_DIM
    a_neg = jnp.where(lane[0:1] < N_HEADS, -jnp.exp(alog_ref[...]), 0.0)

    def expand(v):
        hi, lo = _split_bf16(v)
        return _dot(jnp.concatenate([hi, lo], axis=1), e2_ref[...])

    for c in range(rows // L):
        conv_chunk(c)
        r0 = c * L
        rs = slice(r0, r0 + L)
        dt_raw = dt_ref[0, rs, :] + dtb_ref[...]
        dt = jnp.maximum(dt_raw, 0.0) + jnp.log(1.0 + jnp.exp(-jnp.abs(dt_raw)))
        d_a = dt * a_neg
        da_hi, da_lo = _split_bf16(d_a)
        a_cs = _dot(tril2_ref[...], jnp.concatenate([da_hi, da_lo], axis=0))
        a_cs_t = a_cs.T
        a_last = a_cs[L - 1:L, :]
        dec_in = jnp.exp(a_last - a_cs)
        dec_out = jnp.exp(a_cs)
        dt_x = expand(dt)
        w_x = expand(dt * dec_in)
        do_x = expand(dec_out)

        xs = xc_ref[rs, 0:D_INNER]
        xdt = (xs * dt_x).astype(BF16)
        xw = (xs * w_x).astype(BF16)
        y = dskip_ref[...] * xs
        z = z_ref[0, rs, :].astype(F32)
        gate = z * _sigmoid(z)

        for g in range(SSD_GROUPS):
            b0 = D_INNER + g * SSD_STATE
            c0 = D_INNER + SSD_GROUPS * SSD_STATE + g * SSD_STATE
            bg = xc_ref[rs, b0:b0 + SSD_STATE].astype(BF16)
            cg = xc_ref[rs, c0:c0 + SSD_STATE].astype(BF16)
            cb = lax.dot_general(cg, bg, NT_DIMS, preferred_element_type=F32)
            gs = slice(g * 256, (g + 1) * 256)
            yd_parts = []
            for pair in range(2):
                ms = []
                for e in range(2):
                    hd = g * 4 + pair * 2 + e
                    colb = jnp.broadcast_to(a_cs[:, hd:hd + 1], (L, L))
                    rowb = jnp.broadcast_to(a_cs_t[hd:hd + 1, :], (L, L))
                    lmat = jnp.exp(jnp.where(tril, colb - rowb, -jnp.inf))
                    ms.append((cb * lmat).astype(BF16))
                mcat = jnp.concatenate(ms, axis=1)
                p0 = g * 256 + pair * LANES
                xp = xdt[:, p0:p0 + LANES]
                zero = jnp.zeros_like(xp)
                xstack = jnp.concatenate(
                    [jnp.where(lo_half, xp, zero), jnp.where(lo_half, zero, xp)], axis=0)
                yd_parts.append(_dot(mcat, xstack))
            y_diag = jnp.concatenate(yd_parts, axis=1)
            st = state_ref[:, gs]
            y_off = _dot(cg, st.astype(BF16)) * do_x[:, gs]
            upd = lax.dot_general(bg, xw[:, gs], TN_DIMS, preferred_element_type=F32)
            state_ref[:, gs] = st * do_x[L - 1:L, gs] + upd
            yg = (y[:, gs] + y_diag + y_off) * gate[:, gs]
            ss = jnp.mean(yg * yg, axis=-1, keepdims=True)
            o_ref[0, rs, gs] = (yg * lax.rsqrt(ss + EPS) * nw_ref[:, gs]).astype(BF16)


def _ssd_mixer(xbc, z, dt, conv_w, conv_b, dt_bias, a_log, d_skip, norm_w, tril2, e2):
    bsz, s, _ = xbc.shape
    rows = SSD_ROWS
    row = lambda b, i: (b, i, 0)
    t = jnp.arange(SSD_CHUNK)
    shift = jnp.concatenate(
        [(t[:, None] - k == t[None, :]).astype(BF16) for k in range(1, SSD_CONV)], axis=0)
    return pl.pallas_call(
        _ssd_kernel,
        grid=(bsz, s // rows),
        in_specs=[
            pl.BlockSpec((1, rows, CONV_CH), row),
            pl.BlockSpec((1, rows, D_INNER), row),
            pl.BlockSpec((1, rows, LANES), row),
            _const_spec((SSD_CONV, CONV_CH)),
            _const_spec((1, CONV_CH)),
            _const_spec((1, LANES)),
            _const_spec((1, LANES)),
            _const_spec((1, D_INNER)),
            _const_spec((1, D_INNER)),
            _const_spec((SSD_CHUNK, 2 * SSD_CHUNK)),
            _const_spec((2 * LANES, D_INNER)),
            _const_spec(((SSD_CONV - 1) * SSD_CHUNK, SSD_CHUNK)),
        ],
        out_specs=pl.BlockSpec((1, rows, D_INNER), row),
        out_shape=jax.ShapeDtypeStruct((bsz, s, D_INNER), BF16),
        scratch_shapes=[
            pltpu.VMEM((CONV_HALO, CONV_CH), F32),
            pltpu.VMEM((rows, CONV_CH), F32),
            pltpu.VMEM((SSD_STATE, D_INNER), F32),
        ],
        compiler_params=pltpu.CompilerParams(
            dimension_semantics=("parallel", "arbitrary"), vmem_limit_bytes=VMEM_LIMIT),
        name="ssd_mixer",
    )(xbc, z, dt, conv_w, conv_b, dt_bias, a_log, d_skip, norm_w, tril2, e2, shift)


def _attn_kernel(planes, group, bounded_ref, q_ref, k_ref, v_ref, kp_ref, vp_ref, o_ref, lse_ref):
    blk = ATT_BLK
    sub = blk // planes
    first = pl.program_id(2) == 0

    def pos(i):
        return i if planes == 1 else (i % sub) * planes + i // sub

    row = lax.broadcasted_iota(jnp.int32, (blk, 2 * blk), 0)
    key = lax.broadcasted_iota(jnp.int32, (blk, 2 * blk), 1)
    rel = jnp.where(key >= blk, blk + pos(key - blk), pos(key)) - pos(row)
    band = (rel >= 0) & (rel <= blk)
    lo_key = jnp.where(first, blk, 0)
    bias_head = jnp.where(band & (key >= lo_key), 0.0, -jnp.inf)
    bias_head = jnp.concatenate([bias_head, bias_head], axis=0)
    bias_rest = jnp.where(band, 0.0, -jnp.inf)
    bias_rest = jnp.concatenate([bias_rest, bias_rest], axis=0)
    lane = lax.broadcasted_iota(jnp.int32, (blk, LANES), 1)
    lo_half = lane < HEAD_DIM

    def load(ref, g, ps):
        if planes == 1:
            return ref[0, 0, g * blk:(g + 1) * blk, ps]
        return ref[0, :, 0, g * sub:(g + 1) * sub, ps].reshape(blk, LANES)

    def store(ref, g, ps, val):
        if planes == 1:
            ref[0, 0, g * blk:(g + 1) * blk, ps] = val
        else:
            ref[0, :, 0, g * sub:(g + 1) * sub, ps] = val.reshape(planes, sub, val.shape[-1])

    def body(bounded):
        for g in range(group):
            store(lse_ref, g, slice(0, LANES), jnp.zeros((blk, LANES), F32))
            for p in range(N_HEADS // 2):
                ps = slice(p * LANES, (p + 1) * LANES)
                qp = load(q_ref, g, ps)
                zero = jnp.zeros_like(qp)
                qs = jnp.concatenate(
                    [jnp.where(lo_half, qp, zero), jnp.where(lo_half, zero, qp)], axis=0)
                k_prev = load(kp_ref, 0, ps) if g == 0 else load(k_ref, g - 1, ps)
                v_prev = load(vp_ref, 0, ps) if g == 0 else load(v_ref, g - 1, ps)
                k2 = jnp.concatenate([k_prev, load(k_ref, g, ps)], axis=0)
                v2 = jnp.concatenate([v_prev, load(v_ref, g, ps)], axis=0)
                s = lax.dot_general(qs, k2, NT_DIMS, preferred_element_type=F32)
                s = s + (bias_head if g == 0 else bias_rest)
                if bounded:
                    e = jnp.exp2(s)
                else:
                    m = jnp.max(s, axis=-1, keepdims=True)
                    e = jnp.exp2(s - m)
                l = jnp.sum(e, axis=-1, keepdims=True)
                pv = _dot(e.astype(BF16), v2)
                inv = 1.0 / l
                o_pair = jnp.where(lo_half, pv[0:blk] * inv[0:blk], pv[blk:] * inv[blk:])
                store(o_ref, g, ps, o_pair.astype(BF16))
                lse = jnp.log2(l) * LN2 if bounded else (m + jnp.log2(l)) * LN2
                store(lse_ref, g, slice(2 * p, 2 * p + 1), lse[0:blk])
                store(lse_ref, g, slice(2 * p + 1, 2 * p + 2), lse[blk:])

    bounded = bounded_ref[0] != 0

    @pl.when(bounded)
    def _():
        body(True)

    @pl.when(jnp.logical_not(bounded))
    def _():
        body(False)


def _dilated_attention(bounded, q, k, v, dilation):
    bsz, n_planes, rows, w = q.shape
    planes = n_planes // dilation
    assert planes in (1, 4)
    sub = ATT_BLK // planes
    nb = rows // sub
    group = min(ATT_GROUP, nb)
    if planes == 1:
        view = lambda t: t
        blk_shape = lambda n, width: (1, 1, n * ATT_BLK, width)
        cur = lambda b, res, j: (b, res, j, 0)
        prev = lambda b, res, j: (b, res, jnp.maximum(j * group - 1, 0), 0)
    else:
        view = lambda t: t.reshape(bsz, planes, dilation, rows, t.shape[-1])
        blk_shape = lambda n, width: (1, planes, 1, n * sub, width)
        cur = lambda b, res, j: (b, 0, res, j, 0)
        prev = lambda b, res, j: (b, 0, res, jnp.maximum(j * group - 1, 0), 0)
    tile = lambda width=w: pl.BlockSpec(blk_shape(group, width), cur)
    single = pl.BlockSpec(blk_shape(1, w), prev)
    lse_shape = (bsz, n_planes, rows, LANES)
    o, lse = pl.pallas_call(
        functools.partial(_attn_kernel, planes, group),
        grid=(bsz, dilation, nb // group),
        in_specs=[pl.BlockSpec(memory_space=pltpu.SMEM), tile(), tile(), tile(), single, single],
        out_specs=[tile(), tile(LANES)],
        out_shape=[jax.ShapeDtypeStruct(_view_shape(q.shape, planes, dilation), BF16),
                   jax.ShapeDtypeStruct(_view_shape(lse_shape, planes, dilation), F32)],
        compiler_params=pltpu.CompilerParams(
            dimension_semantics=("parallel", "parallel", "arbitrary"),
            vmem_limit_bytes=VMEM_LIMIT),
        name=f"dilated_attn_r{dilation}",
    )(bounded, view(q), view(k), view(v), view(k), view(v))
    return o.reshape(q.shape), lse.reshape(lse_shape)


def _view_shape(shape, planes, dilation):
    bsz, _, rows, width = shape
    return shape if planes == 1 else (bsz, planes, dilation, rows, width)


def _outproj_kernel(x_ref, mod_ref, ys_ref, o1_ref, o2_ref, o3_ref, l1_ref, l2_ref, l3_ref,
                    nw_ref, e2_ref, w_ref, out_ref, wide_ref, wide2_ref, narrow_ref, narrow2_ref):
    cls_rows = x_ref.shape[1] // CLS_STRIDE

    def natural(cls_ref, stage_ref, stage2_ref):
        tiles = stage_ref.shape[0]
        quarter = cls_rows * 4
        for res in range(CLS_STRIDE):
            rho, a = res % 4, res // 4
            val = cls_ref[0, res].astype(F32)
            for t in range(tiles):
                stage2_ref[t, pl.ds(rho * quarter + a, cls_rows, stride=4), :] = (
                    val[:, t * LANES:(t + 1) * LANES])
        for t in range(tiles):
            for rho in range(4):
                stage_ref[t, pl.ds(rho, quarter, stride=4), :] = (
                    stage2_ref[t, rho * quarter:(rho + 1) * quarter, :])
        return jnp.concatenate([stage_ref[t] for t in range(tiles)], axis=1)

    l1 = l1_ref[0]
    l2 = natural(l2_ref, narrow_ref, narrow2_ref)
    l3 = natural(l3_ref, narrow_ref, narrow2_ref)
    mx = jnp.maximum(jnp.maximum(l1, l2), l3)
    e1, e2, e3 = jnp.exp(l1 - mx), jnp.exp(l2 - mx), jnp.exp(l3 - mx)
    inv = 1.0 / (e1 + e2 + e3)

    def expand(v):
        hi, lo = _split_bf16(v)
        return _dot(jnp.concatenate([hi, lo], axis=1), e2_ref[...])

    o = expand(e1 * inv) * o1_ref[0].astype(F32)
    o = o + expand(e2 * inv) * natural(o2_ref, wide_ref, wide2_ref)
    o = o + expand(e3 * inv) * natural(o3_ref, wide_ref, wide2_ref)
    ms = jnp.mean(o * o, axis=-1, keepdims=True)
    y_att = (o * lax.rsqrt(ms + EPS) * nw_ref[...]).astype(BF16)
    mix = _dot(ys_ref[0], w_ref[0:D_INNER, :]) + _dot(y_att, w_ref[D_INNER:, :])
    out_ref[0] = x_ref[0] + mod_ref[0, 2:3, :] * mix


def _out_projection(x, mod3, y_ssd, outs, lses, norm_w, e2, w_out, tm):
    bsz, s, d = x.shape
    row = lambda b, i: (b, i, 0)
    wide = pl.BlockSpec((1, tm, D_INNER), row)
    narrow = pl.BlockSpec((1, tm, LANES), row)
    cls = lambda width: pl.BlockSpec((1, CLS_STRIDE, tm // CLS_STRIDE, width),
                                     lambda b, i: (b, 0, i, 0))
    return pl.pallas_call(
        _outproj_kernel,
        grid=(bsz, s // tm),
        in_specs=[
            pl.BlockSpec((1, tm, d), row),
            pl.BlockSpec((1, N_MOD, d), lambda b, i: (b, 0, 0)),
            wide, wide, cls(D_INNER), cls(D_INNER), narrow, cls(LANES), cls(LANES),
            _const_spec((1, D_INNER)),
            _const_spec((2 * LANES, D_INNER)),
            _const_spec((2 * D_INNER, d)),
        ],
        out_specs=pl.BlockSpec((1, tm, d), row),
        out_shape=jax.ShapeDtypeStruct((bsz, s, d), F32),
        scratch_shapes=[pltpu.VMEM((D_INNER // LANES, tm, LANES), F32),
                        pltpu.VMEM((D_INNER // LANES, tm, LANES), F32),
                        pltpu.VMEM((1, tm, LANES), F32),
                        pltpu.VMEM((1, tm, LANES), F32)],
        compiler_params=pltpu.CompilerParams(
            dimension_semantics=("parallel", "arbitrary"), vmem_limit_bytes=VMEM_LIMIT),
        name="out_proj",
    )(x, mod3, y_ssd, *outs, *lses, norm_w, e2, w_out)


FF_TN = 1024


def _mlp_kernel(x_ref, mod_ref, nw_ref, w1_ref, w2_ref, out_ref, h_ref):
    x = x_ref[0]
    h = _norm_mod(x, nw_ref[...], mod_ref[0, 4:5, :], mod_ref[0, 3:4, :])
    h_ref[...] = h.astype(BF16)
    acc = jnp.zeros(x.shape, F32)
    for j in range(D_FF // FF_TN):
        fs = slice(j * FF_TN, (j + 1) * FF_TN)
        u = jnp.maximum(_dot(h_ref[...], w1_ref[:, fs]), 0.0)
        acc = acc + _dot((u * u).astype(BF16), w2_ref[fs, :])
    out_ref[0] = x + mod_ref[0, 5:6, :] * acc


def _mlp(x, mod3, norm_w, w1, w2, tm):
    bsz, s, d = x.shape
    row = lambda b, i: (b, i, 0)
    return pl.pallas_call(
        _mlp_kernel,
        grid=(bsz, s // tm),
        in_specs=[
            pl.BlockSpec((1, tm, d), row),
            pl.BlockSpec((1, N_MOD, d), lambda b, i: (b, 0, 0)),
            _const_spec((1, d)),
            _const_spec((d, D_FF)),
            _const_spec((D_FF, d)),
        ],
        out_specs=pl.BlockSpec((1, tm, d), row),
        out_shape=jax.ShapeDtypeStruct((bsz, s, d), F32),
        scratch_shapes=[pltpu.VMEM((tm, d), BF16)],
        compiler_params=pltpu.CompilerParams(
            dimension_semantics=("parallel", "arbitrary"), vmem_limit_bytes=VMEM_LIMIT),
        name="mlp",
    )(x, mod3, norm_w, w1, w2)


def _head_expand_matrix():
    head_of_lane = jnp.arange(D_INNER) // HEAD_DIM
    e = (jnp.arange(LANES)[:, None] == head_of_lane[None, :]).astype(BF16)
    return jnp.concatenate([e, e], axis=0)


def _pad_lanes(v, n=LANES):
    return jnp.pad(v.astype(F32), (0, n - v.shape[0])).reshape(1, n)


def kernel(x, c, norm1_w, norm2_w, w_ada, b_ada, w_in, conv_w, conv_b, dt_bias, a_log, d_skip,
           ssd_norm_w, q_norm_w, k_norm_w, attn_norm_w, w_out, w_ff1, w_ff2):
    bsz, s, d = x.shape
    depth = w_ada.shape[0]
    tm = 512
    e2 = _head_expand_matrix()
    idx = jnp.arange(256) // HEAD_DIM
    bd = (idx[:, None] == idx[None, :]).astype(BF16)
    t = (jnp.arange(SSD_CHUNK)[:, None] >= jnp.arange(SSD_CHUNK)[None, :]).astype(BF16)
    tril2 = jnp.concatenate([t, t], axis=1)
    o_xbc = D_INNER + CONV_CH
    o_dt = o_xbc + N_HEADS

    for l in range(depth):
        mod3 = _modulation(c, w_ada[l], b_ada[l]).reshape(bsz, N_MOD, d)
        wl = w_in[l]
        w_pad = jnp.concatenate(
            [wl[:, :o_xbc], wl[:, o_dt:], wl[:, o_xbc:o_dt],
             jnp.zeros((d, LANES - N_HEADS), wl.dtype)], axis=1).astype(BF16)
        z, xbc, dt, q, k, v, q16, k16, v16 = _in_projection(
            x, mod3, norm1_w[l].reshape(1, d), w_pad,
            jnp.tile(q_norm_w[l], N_HEADS).reshape(1, D_INNER),
            jnp.tile(k_norm_w[l], N_HEADS).reshape(1, D_INNER), bd, tm)
        y_ssd = _ssd_mixer(
            xbc, z, dt, conv_w[l], conv_b[l].reshape(1, CONV_CH), _pad_lanes(dt_bias[l]),
            _pad_lanes(a_log[l]), jnp.repeat(d_skip[l], HEAD_DIM).reshape(1, D_INNER),
            ssd_norm_w[l].reshape(1, D_INNER), tril2, e2)
        score_bound = (HEAD_DIM ** 0.5 * LOG2E * jnp.max(jnp.abs(q_norm_w[l]))
                       * jnp.max(jnp.abs(k_norm_w[l])))
        bounded = (score_bound * BF16_SLACK <= SCORE_BOUND).astype(jnp.int32).reshape(1)
        outs, lses = [], []
        for window, dilation in DILATED_PATTERNS:
            assert window // dilation == ATT_BLK
            if dilation == 1:
                nat = lambda t: t.reshape(bsz, 1, s, t.shape[-1])
                o, lse = _dilated_attention(bounded, nat(q), nat(k), nat(v), 1)
                o, lse = o.reshape(bsz, s, D_INNER), lse.reshape(bsz, s, LANES)
            else:
                o, lse = _dilated_attention(bounded, q16, k16, v16, dilation)
            outs.append(o)
            lses.append(lse)
        x = _out_projection(x, mod3, y_ssd, outs, lses, attn_norm_w[l].reshape(1, D_INNER),
                            e2, w_out[l].astype(BF16), tm)
        x = _mlp(x, mod3, norm2_w[l].reshape(1, d), w_ff1[l].astype(BF16),
                 w_ff2[l].astype(BF16), tm)
    return x.astype(c.dtype)
```

```python
import functools

import jax
import jax.numpy as jnp
from jax import lax
from jax.experimental import pallas as pl
from jax.experimental.pallas import tpu as pltpu

F32 = jnp.float32
BF16 = jnp.bfloat16

D_MODEL = 1024
HEAD_DIM = 64
N_HEADS = 16
SSD_GROUPS = 4
SSD_STATE = 128
SSD_CONV = 4
SSD_CHUNK = 128
D_INNER = N_HEADS * HEAD_DIM
CONV_CH = D_INNER + 2 * SSD_GROUPS * SSD_STATE
D_FF = 4 * D_MODEL
N_MOD = 6
EPS = 1e-6
DILATED_PATTERNS = ((128, 1), (512, 4), (2048, 16))
ATT_BLK = 128
ATT_GROUP = 4
CLS_STRIDE = 16
LANES = 128
VMEM_LIMIT = 56 * 1024 * 1024

LOG2E = 1.4426950408889634
LN2 = 0.6931471805599453
SCORE_BOUND = 60.0
BF16_SLACK = 1.02

NT_DIMS = (((1,), (1,)), ((), ()))
TN_DIMS = (((0,), (0,)), ((), ()))


def _dot(a, b):
    return jnp.dot(a, b, preferred_element_type=F32)


def _split_bf16(v):
    hi = v.astype(BF16)
    lo = (v - hi.astype(F32)).astype(BF16)
    return hi, lo


def _silu(v):
    h = 0.5 * v
    return h + h * jnp.tanh(h)


def _const_spec(shape):
    nd = len(shape)
    return pl.BlockSpec(shape, lambda *_: (0,) * nd, pipeline_mode=pl.Buffered(1))


def _mod_kernel(c_ref, w_ref, b_ref, o_ref):
    c = c_ref[...]
    ca = _silu(c)
    c_hi, c_lo = _split_bf16(ca)
    w = w_ref[...]
    w_hi, w_lo = _split_bf16(w)
    acc = _dot(c_hi, w_hi) + _dot(c_lo, w_hi) + _dot(c_hi, w_lo)
    o_ref[...] = acc + b_ref[...]


def _modulation(c, w_ada, b_ada):
    bsz, d = c.shape
    n = w_ada.shape[1]
    tn = 1536
    return pl.pallas_call(
        _mod_kernel,
        grid=(n // tn,),
        in_specs=[
            pl.BlockSpec((bsz, d), lambda j: (0, 0)),
            pl.BlockSpec((d, tn), lambda j: (0, j)),
            pl.BlockSpec((1, tn), lambda j: (0, j)),
        ],
        out_specs=pl.BlockSpec((bsz, tn), lambda j: (0, j)),
        out_shape=jax.ShapeDtypeStruct((bsz, n), F32),
        compiler_params=pltpu.CompilerParams(
            dimension_semantics=("arbitrary",), vmem_limit_bytes=VMEM_LIMIT),
        name="adaln_mod",
    )(c, w_ada, b_ada.reshape(1, n))


IN_TN = 512


def _norm_mod(x, nw, scale, shift):
    ms = jnp.mean(x * x, axis=-1, keepdims=True)
    return (x * lax.rsqrt(ms + EPS) * nw) * (1.0 + scale) + shift


def _inproj_kernel(x_ref, mod_ref, nw_ref, wzx_ref, wqkv_ref, wdt_ref, dtb_ref, qw_ref, kw_ref,
                   bd_ref, z_ref, xbc_ref, dt_ref, q_ref, k_ref, v_ref, q16_ref, k16_ref,
                   v16_ref, h_ref, stage_ref, stage2_ref):
    x = x_ref[0]
    tm = x.shape[0]
    h = _norm_mod(x, nw_ref[...], mod_ref[0, 1:2, :], mod_ref[0, 0:1, :])
    h_ref[...] = h.astype(BF16)

    def proj(w_ref, c0, width):
        return _dot(h_ref[...], w_ref[:, c0:c0 + width])

    def qk_norm(acc, w):
        sq = (acc * acc).astype(BF16)
        parts = [_dot(sq[:, i:i + 256], bd_ref[...]) for i in range(0, IN_TN, 256)]
        ss = jnp.concatenate(parts, axis=1)
        return acc * lax.rsqrt(ss * (1.0 / HEAD_DIM) + EPS) * w

    cls_rows = x.shape[0] // CLS_STRIDE

    def emit(nat_ref, cls_ref, sl, val):
        nat_ref[0, :, sl] = val.astype(BF16)
        tiles = IN_TN // LANES
        quarter = tm // 4
        for t in range(tiles):
            stage_ref[t] = val[:, t * LANES:(t + 1) * LANES]
        for t in range(tiles):
            for rho in range(4):
                stage2_ref[t, rho * quarter:(rho + 1) * quarter, :] = (
                    stage_ref[t, pl.ds(rho, quarter, stride=4), :])
        for res in range(CLS_STRIDE):
            rho, a = res % 4, res // 4
            rows = [stage2_ref[t, pl.ds(rho * quarter + a, cls_rows, stride=4), :]
                    for t in range(tiles)]
            cls_ref[0, res, :, sl] = jnp.concatenate(rows, axis=1).astype(BF16)

    col = 0
    for j in range(D_INNER // IN_TN):
        z_ref[0, :, j * IN_TN:(j + 1) * IN_TN] = proj(wzx_ref, col, IN_TN).astype(BF16)
        col += IN_TN
    for j in range(CONV_CH // IN_TN):
        xbc_ref[0, :, j * IN_TN:(j + 1) * IN_TN] = proj(wzx_ref, col, IN_TN).astype(BF16)
        col += IN_TN
    col = 0
    for j in range(D_INNER // IN_TN):
        sl = slice(j * IN_TN, (j + 1) * IN_TN)
        emit(q_ref, q16_ref, sl,
             qk_norm(proj(wqkv_ref, col, IN_TN), qw_ref[:, sl]) * (HEAD_DIM ** -0.5 * LOG2E))
        col += IN_TN
    for j in range(D_INNER // IN_TN):
        sl = slice(j * IN_TN, (j + 1) * IN_TN)
        emit(k_ref, k16_ref, sl, qk_norm(proj(wqkv_ref, col, IN_TN), kw_ref[:, sl]))
        col += IN_TN
    for j in range(D_INNER // IN_TN):
        emit(v_ref, v16_ref, slice(j * IN_TN, (j + 1) * IN_TN), proj(wqkv_ref, col, IN_TN))
        col += IN_TN
    dt_raw = proj(wdt_ref, 0, LANES) + dtb_ref[...]
    dt_ref[0] = jnp.maximum(dt_raw, 0.0) + jnp.log(1.0 + jnp.exp(-jnp.abs(dt_raw)))


def _in_projection(x, mod3, norm_w, w_zx, w_qkv, w_dt, dt_bias, qw, kw, bd, tm):
    bsz, s, d = x.shape
    row = lambda b, i: (b, i, 0)
    out_bf = lambda n: jax.ShapeDtypeStruct((bsz, s, n), BF16)
    cls_spec = pl.BlockSpec((1, CLS_STRIDE, tm // CLS_STRIDE, D_INNER), lambda b, i: (b, 0, i, 0))
    cls_shape = jax.ShapeDtypeStruct((bsz, CLS_STRIDE, s // CLS_STRIDE, D_INNER), BF16)
    return pl.pallas_call(
        _inproj_kernel,
        grid=(bsz, s // tm),
        in_specs=[
            pl.BlockSpec((1, tm, d), row),
            pl.BlockSpec((1, N_MOD, d), lambda b, i: (b, 0, 0)),
            _const_spec((1, d)),
            _const_spec(w_zx.shape),
            _const_spec(w_qkv.shape),
            _const_spec(w_dt.shape),
            _const_spec((1, LANES)),
            _const_spec((1, D_INNER)),
            _const_spec((1, D_INNER)),
            _const_spec((256, 256)),
        ],
        out_specs=[
            pl.BlockSpec((1, tm, D_INNER), row),
            pl.BlockSpec((1, tm, CONV_CH), row),
            pl.BlockSpec((1, tm, LANES), row),
            pl.BlockSpec((1, tm, D_INNER), row),
            pl.BlockSpec((1, tm, D_INNER), row),
            pl.BlockSpec((1, tm, D_INNER), row),
            cls_spec, cls_spec, cls_spec,
        ],
        out_shape=[out_bf(D_INNER), out_bf(CONV_CH),
                   jax.ShapeDtypeStruct((bsz, s, LANES), F32),
                   out_bf(D_INNER), out_bf(D_INNER), out_bf(D_INNER),
                   cls_shape, cls_shape, cls_shape],
        scratch_shapes=[pltpu.VMEM((tm, d), BF16),
                        pltpu.VMEM((IN_TN // LANES, tm, LANES), F32),
                        pltpu.VMEM((IN_TN // LANES, tm, LANES), F32)],
        compiler_params=pltpu.CompilerParams(
            dimension_semantics=("parallel", "arbitrary"), vmem_limit_bytes=VMEM_LIMIT),
        name="in_proj",
    )(x, mod3, norm_w, w_zx, w_qkv, w_dt, dt_bias, qw, kw, bd)


SSD_ROWS = 512
CONV_HALO = 8


def _ssd_kernel(xbc_ref, z_ref, dt_ref, cw_ref, cb_ref, alog_ref, dskip_ref,
                nw_ref, tril2_ref, e2_ref, shift_ref, o_ref, halo_ref, xc_ref, state_ref):
    rows = xbc_ref.shape[1]
    L = SSD_CHUNK
    taps = SSD_CONV - 1

    @pl.when(pl.program_id(1) == 0)
    def _():
        halo_ref[...] = jnp.zeros_like(halo_ref)
        state_ref[...] = jnp.zeros_like(state_ref)

    hsub = lax.broadcasted_iota(jnp.int32, (CONV_HALO, 256), 0)

    def conv_chunk(c):
        r0 = c * L
        for c0 in range(0, CONV_CH, 256):
            cs = slice(c0, c0 + 256)
            u = xbc_ref[0, r0:r0 + L, cs]
            shifted = _dot(shift_ref[...], u)
            uf = u.astype(F32)
            acc = cb_ref[:, cs] + cw_ref[taps:taps + 1, cs] * uf
            for k in range(1, taps + 1):
                acc = acc + cw_ref[taps - k:taps - k + 1, cs] * shifted[(k - 1) * L:k * L]
            xc_ref[r0:r0 + L, cs] = _silu(acc)
            if c == 0:
                halo = halo_ref[:, cs]
            else:
                halo = xbc_ref[0, r0 - 2 * CONV_HALO:r0, cs].astype(F32)[CONV_HALO:]
            head = acc[0:CONV_HALO]
            for k in range(1, taps + 1):
                prev = jnp.where(hsub < k, pltpu.roll(halo, k, axis=0), 0.0)
                head = head + cw_ref[taps - k:taps - k + 1, cs] * prev
            xc_ref[r0:r0 + CONV_HALO, cs] = _silu(head)
            if r0 + L == rows:
                halo_ref[:, cs] = uf[L - CONV_HALO:L]

    lane = lax.broadcasted_iota(jnp.int32, (L, LANES), 1)
    sub = lax.broadcasted_iota(jnp.int32, (L, LANES), 0)
    tril = sub >= lane
    lo_half = lane < HEAD_DIM
    a_neg = jnp.where(lane[0:1] < N_HEADS, -jnp.exp(alog_ref[...]), 0.0)

    def expand(v):
        hi, lo = _split_bf16(v)
        return _dot(jnp.concatenate([hi, lo], axis=1), e2_ref[...])

    conv_chunk(0)
    for c in range(rows // L):
        if c + 1 < rows // L:
            conv_chunk(c + 1)
        r0 = c * L
        rs = slice(r0, r0 + L)
        dt = dt_ref[0, rs, :]
        d_a = dt * a_neg
        da_hi, da_lo = _split_bf16(d_a)
        a_cs = _dot(tril2_ref[...], jnp.concatenate([da_hi, da_lo], axis=0))
        a_cs_t = a_cs.T
        a_last = a_cs[L - 1:L, :]
        dec_in = jnp.exp(a_last - a_cs)
        dec_out = jnp.exp(a_cs)
        dt_x = expand(dt)
        w_x = expand(dt * dec_in)
        do_x = expand(dec_out)

        xs = xc_ref[rs, 0:D_INNER]
        xdt = (xs * dt_x).astype(BF16)
        xw = (xs * w_x).astype(BF16)
        y = dskip_ref[...] * xs
        z = z_ref[0, rs, :].astype(F32)
        gate = _silu(z)

        for g in range(SSD_GROUPS):
            b0 = D_INNER + g * SSD_STATE
            c0 = D_INNER + SSD_GROUPS * SSD_STATE + g * SSD_STATE
            bg = xc_ref[rs, b0:b0 + SSD_STATE].astype(BF16)
            cg = xc_ref[rs, c0:c0 + SSD_STATE].astype(BF16)
            cb = lax.dot_general(cg, bg, NT_DIMS, preferred_element_type=F32)
            gs = slice(g * 256, (g + 1) * 256)
            yd_parts = []
            for pair in range(2):
                ms = []
                for e in range(2):
                    hd = g * 4 + pair * 2 + e
                    colb = jnp.broadcast_to(a_cs[:, hd:hd + 1], (L, L))
                    rowb = jnp.broadcast_to(a_cs_t[hd:hd + 1, :], (L, L))
                    lmat = jnp.exp(jnp.where(tril, colb - rowb, -jnp.inf))
                    ms.append((cb * lmat).astype(BF16))
                mcat = jnp.concatenate(ms, axis=1)
                p0 = g * 256 + pair * LANES
                xp = xdt[:, p0:p0 + LANES]
                zero = jnp.zeros_like(xp)
                xstack = jnp.concatenate(
                    [jnp.where(lo_half, xp, zero), jnp.where(lo_half, zero, xp)], axis=0)
                yd_parts.append(_dot(mcat, xstack))
            y_diag = jnp.concatenate(yd_parts, axis=1)
            st = state_ref[:, gs]
            y_off = _dot(cg, st.astype(BF16)) * do_x[:, gs]
            upd = lax.dot_general(bg, xw[:, gs], TN_DIMS, preferred_element_type=F32)
            state_ref[:, gs] = st * do_x[L - 1:L, gs] + upd
            yg = (y[:, gs] + y_diag + y_off) * gate[:, gs]
            ss = jnp.mean(yg * yg, axis=-1, keepdims=True)
            o_ref[0, rs, gs] = (yg * lax.rsqrt(ss + EPS) * nw_ref[:, gs]).astype(BF16)


def _ssd_mixer(xbc, z, dt, conv_w, conv_b, a_log, d_skip, norm_w, tril2, e2):
    bsz, s, _ = xbc.shape
    rows = SSD_ROWS
    row = lambda b, i: (b, i, 0)
    t = jnp.arange(SSD_CHUNK)
    shift = jnp.concatenate(
        [(t[:, None] - k == t[None, :]).astype(BF16) for k in range(1, SSD_CONV)], axis=0)
    return pl.pallas_call(
        _ssd_kernel,
        grid=(bsz, s // rows),
        in_specs=[
            pl.BlockSpec((1, rows, CONV_CH), row),
            pl.BlockSpec((1, rows, D_INNER), row),
            pl.BlockSpec((1, rows, LANES), row),
            _const_spec((SSD_CONV, CONV_CH)),
            _const_spec((1, CONV_CH)),
            _const_spec((1, LANES)),
            _const_spec((1, D_INNER)),
            _const_spec((1, D_INNER)),
            _const_spec((SSD_CHUNK, 2 * SSD_CHUNK)),
            _const_spec((2 * LANES, D_INNER)),
            _const_spec(((SSD_CONV - 1) * SSD_CHUNK, SSD_CHUNK)),
        ],
        out_specs=pl.BlockSpec((1, rows, D_INNER), row),
        out_shape=jax.ShapeDtypeStruct((bsz, s, D_INNER), BF16),
        scratch_shapes=[
            pltpu.VMEM((CONV_HALO, CONV_CH), F32),
            pltpu.VMEM((rows, CONV_CH), F32),
            pltpu.VMEM((SSD_STATE, D_INNER), F32),
        ],
        compiler_params=pltpu.CompilerParams(
            dimension_semantics=("parallel", "arbitrary"), vmem_limit_bytes=VMEM_LIMIT),
        name="ssd_mixer",
    )(xbc, z, dt, conv_w, conv_b, a_log, d_skip, norm_w, tril2, e2, shift)


def _attn_kernel(planes, group, bounded_ref, q_ref, k_ref, v_ref, kp_ref, vp_ref, o_ref, lse_ref):
    blk = ATT_BLK
    sub = blk // planes
    first = pl.program_id(2) == 0

    def pos(i):
        return i if planes == 1 else (i % sub) * planes + i // sub

    row = lax.broadcasted_iota(jnp.int32, (blk, 2 * blk), 0)
    key = lax.broadcasted_iota(jnp.int32, (blk, 2 * blk), 1)
    rel = jnp.where(key >= blk, blk + pos(key - blk), pos(key)) - pos(row)
    band = (rel >= 0) & (rel <= blk)
    lo_key = jnp.where(first, blk, 0)
    bias_head = jnp.where(band & (key >= lo_key), 0.0, -jnp.inf)
    bias_head = jnp.concatenate([bias_head, bias_head], axis=0)
    bias_rest = jnp.where(band, 0.0, -jnp.inf)
    bias_rest = jnp.concatenate([bias_rest, bias_rest], axis=0)
    lane = lax.broadcasted_iota(jnp.int32, (blk, LANES), 1)
    lo_half = lane < HEAD_DIM

    n_res = q_ref.shape[1] if planes == 1 else 1

    def load(ref, r, g, ps):
        if planes == 1:
            return ref[0, r, g * blk:(g + 1) * blk, ps]
        return ref[0, :, 0, g * sub:(g + 1) * sub, ps].reshape(blk, LANES)

    def store(ref, r, g, ps, val):
        if planes == 1:
            ref[0, r, g * blk:(g + 1) * blk, ps] = val
        else:
            ref[0, :, 0, g * sub:(g + 1) * sub, ps] = val.reshape(planes, sub, val.shape[-1])

    def block(bounded, r, g):
        store(lse_ref, r, g, slice(0, LANES), jnp.zeros((blk, LANES), F32))
        for p in range(N_HEADS // 2):
            ps = slice(p * LANES, (p + 1) * LANES)
            qp = load(q_ref, r, g, ps)
            zero = jnp.zeros_like(qp)
            qs = jnp.concatenate(
                [jnp.where(lo_half, qp, zero), jnp.where(lo_half, zero, qp)], axis=0)
            k_prev = load(kp_ref, r, 0, ps) if g == 0 else load(k_ref, r, g - 1, ps)
            v_prev = load(vp_ref, r, 0, ps) if g == 0 else load(v_ref, r, g - 1, ps)
            k2 = jnp.concatenate([k_prev, load(k_ref, r, g, ps)], axis=0)
            v2 = jnp.concatenate([v_prev, load(v_ref, r, g, ps)], axis=0)
            s = lax.dot_general(qs, k2, NT_DIMS, preferred_element_type=F32)
            s = s + (bias_head if g == 0 else bias_rest)
            if bounded:
                e = jnp.exp2(s)
            else:
                m = jnp.max(s, axis=-1, keepdims=True)
                e = jnp.exp2(s - m)
            l = jnp.sum(e, axis=-1, keepdims=True)
            pv = _dot(e.astype(BF16), v2)
            inv = 1.0 / l
            o_pair = jnp.where(lo_half, pv[0:blk] * inv[0:blk], pv[blk:] * inv[blk:])
            store(o_ref, r, g, ps, o_pair.astype(BF16))
            lse = jnp.log2(l) * LN2 if bounded else (m + jnp.log2(l)) * LN2
            store(lse_ref, r, g, slice(2 * p, 2 * p + 1), lse[0:blk])
            store(lse_ref, r, g, slice(2 * p + 1, 2 * p + 2), lse[blk:])

    def body(bounded):
        for r in range(n_res):
            for g in range(group):
                block(bounded, r, g)

    bounded = bounded_ref[0] != 0

    @pl.when(bounded)
    def _():
        body(True)

    @pl.when(jnp.logical_not(bounded))
    def _():
        body(False)


def _dilated_attention(bounded, q, k, v, dilation):
    bsz, n_planes, rows, w = q.shape
    planes = n_planes // dilation
    assert planes in (1, 4)
    sub = ATT_BLK // planes
    nb = rows // sub
    group = min(ATT_GROUP, nb)
    n_res = 1
    if planes == 1:
        n_res = min(ATT_GROUP // group, dilation)
        view = lambda t: t
        blk_shape = lambda n, width: (1, n_res, n * ATT_BLK, width)
        cur = lambda b, res, j: (b, res, j, 0)
        prev = lambda b, res, j: (b, res, jnp.maximum(j * group - 1, 0), 0)
    else:
        view = lambda t: t.reshape(bsz, planes, dilation, rows, t.shape[-1])
        blk_shape = lambda n, width: (1, planes, 1, n * sub, width)
        cur = lambda b, res, j: (b, 0, res, j, 0)
        prev = lambda b, res, j: (b, 0, res, jnp.maximum(j * group - 1, 0), 0)
    tile = lambda width=w: pl.BlockSpec(blk_shape(group, width), cur)
    single = pl.BlockSpec(blk_shape(1, w), prev)
    lse_shape = (bsz, n_planes, rows, LANES)
    o, lse = pl.pallas_call(
        functools.partial(_attn_kernel, planes, group),
        grid=(bsz, dilation // n_res, nb // group),
        in_specs=[pl.BlockSpec(memory_space=pltpu.SMEM), tile(), tile(), tile(), single, single],
        out_specs=[tile(), tile(LANES)],
        out_shape=[jax.ShapeDtypeStruct(_view_shape(q.shape, planes, dilation), BF16),
                   jax.ShapeDtypeStruct(_view_shape(lse_shape, planes, dilation), F32)],
        compiler_params=pltpu.CompilerParams(
            dimension_semantics=("parallel", "parallel", "arbitrary"),
            vmem_limit_bytes=VMEM_LIMIT),
        name=f"dilated_attn_r{dilation}",
    )(bounded, view(q), view(k), view(v), view(k), view(v))
    return o.reshape(q.shape), lse.reshape(lse_shape)


def _view_shape(shape, planes, dilation):
    bsz, _, rows, width = shape
    return shape if planes == 1 else (bsz, planes, dilation, rows, width)


def _outproj_kernel(x_ref, mod_ref, ys_ref, o1_ref, o2_ref, o3_ref, l1_ref, l2_ref, l3_ref,
                    nw_ref, e2_ref, w_ref, out_ref, wide_ref, wide2_ref, narrow_ref, narrow2_ref):
    cls_rows = x_ref.shape[1] // CLS_STRIDE

    def natural(cls_ref, stage_ref, stage2_ref):
        tiles = stage_ref.shape[0]
        quarter = cls_rows * 4
        for res in range(CLS_STRIDE):
            rho, a = res % 4, res // 4
            val = cls_ref[0, res].astype(F32)
            for t in range(tiles):
                stage2_ref[t, pl.ds(rho * quarter + a, cls_rows, stride=4), :] = (
                    val[:, t * LANES:(t + 1) * LANES])
        for t in range(tiles):
            for rho in range(4):
                stage_ref[t, pl.ds(rho, quarter, stride=4), :] = (
                    stage2_ref[t, rho * quarter:(rho + 1) * quarter, :])
        return jnp.concatenate([stage_ref[t] for t in range(tiles)], axis=1)

    l1 = l1_ref[0]
    l2 = natural(l2_ref, narrow_ref, narrow2_ref)
    l3 = natural(l3_ref, narrow_ref, narrow2_ref)
    mx = jnp.maximum(jnp.maximum(l1, l2), l3)
    e1, e2, e3 = jnp.exp(l1 - mx), jnp.exp(l2 - mx), jnp.exp(l3 - mx)
    inv = 1.0 / (e1 + e2 + e3)

    def expand(v):
        hi, lo = _split_bf16(v)
        return _dot(jnp.concatenate([hi, lo], axis=1), e2_ref[...])

    o3 = natural(o3_ref, wide_ref, wide2_ref)
    o = o3 + expand(e1 * inv) * (o1_ref[0].astype(F32) - o3)
    o = o + expand(e2 * inv) * (natural(o2_ref, wide_ref, wide2_ref) - o3)
    ms = jnp.mean(o * o, axis=-1, keepdims=True)
    y_att = (o * lax.rsqrt(ms + EPS) * nw_ref[...]).astype(BF16)
    mix = _dot(ys_ref[0], w_ref[0:D_INNER, :]) + _dot(y_att, w_ref[D_INNER:, :])
    out_ref[0] = x_ref[0] + mod_ref[0, 2:3, :] * mix


def _out_projection(x, mod3, y_ssd, outs, lses, norm_w, e2, w_out, tm):
    bsz, s, d = x.shape
    row = lambda b, i: (b, i, 0)
    wide = pl.BlockSpec((1, tm, D_INNER), row)
    narrow = pl.BlockSpec((1, tm, LANES), row)
    cls = lambda width: pl.BlockSpec((1, CLS_STRIDE, tm // CLS_STRIDE, width),
                                     lambda b, i: (b, 0, i, 0))
    return pl.pallas_call(
        _outproj_kernel,
        grid=(bsz, s // tm),
        in_specs=[
            pl.BlockSpec((1, tm, d), row),
            pl.BlockSpec((1, N_MOD, d), lambda b, i: (b, 0, 0)),
            wide, wide, cls(D_INNER), cls(D_INNER), narrow, cls(LANES), cls(LANES),
            _const_spec((1, D_INNER)),
            _const_spec((2 * LANES, D_INNER)),
            _const_spec((2 * D_INNER, d)),
        ],
        out_specs=pl.BlockSpec((1, tm, d), row),
        out_shape=jax.ShapeDtypeStruct((bsz, s, d), F32),
        scratch_shapes=[pltpu.VMEM((D_INNER // LANES, tm, LANES), F32),
                        pltpu.VMEM((D_INNER // LANES, tm, LANES), F32),
                        pltpu.VMEM((1, tm, LANES), F32),
                        pltpu.VMEM((1, tm, LANES), F32)],
        compiler_params=pltpu.CompilerParams(
            dimension_semantics=("parallel", "arbitrary"), vmem_limit_bytes=VMEM_LIMIT),
        name="out_proj",
    )(x, mod3, y_ssd, *outs, *lses, norm_w, e2, w_out)


FF_TN = 1024


def _mlp_kernel(x_ref, mod_ref, nw_ref, w1_ref, w2_ref, out_ref, h_ref):
    x = x_ref[0]
    h = _norm_mod(x, nw_ref[...], mod_ref[0, 4:5, :], mod_ref[0, 3:4, :])
    h_ref[...] = h.astype(BF16)
    acc = jnp.zeros(x.shape, F32)
    for j in range(D_FF // FF_TN):
        fs = slice(j * FF_TN, (j + 1) * FF_TN)
        u = jnp.maximum(_dot(h_ref[...], w1_ref[:, fs]), 0.0)
        acc = acc + _dot((u * u).astype(BF16), w2_ref[fs, :])
    out_ref[0] = x + mod_ref[0, 5:6, :] * acc


def _mlp(x, mod3, norm_w, w1, w2, tm):
    bsz, s, d = x.shape
    row = lambda b, i: (b, i, 0)
    return pl.pallas_call(
        _mlp_kernel,
        grid=(bsz, s // tm),
        in_specs=[
            pl.BlockSpec((1, tm, d), row),
            pl.BlockSpec((1, N_MOD, d), lambda b, i: (b, 0, 0)),
            _const_spec((1, d)),
            _const_spec((d, D_FF)),
            _const_spec((D_FF, d)),
        ],
        out_specs=pl.BlockSpec((1, tm, d), row),
        out_shape=jax.ShapeDtypeStruct((bsz, s, d), F32),
        scratch_shapes=[pltpu.VMEM((tm, d), BF16)],
        compiler_params=pltpu.CompilerParams(
            dimension_semantics=("parallel", "arbitrary"), vmem_limit_bytes=VMEM_LIMIT),
        name="mlp",
    )(x, mod3, norm_w, w1, w2)


def _head_expand_matrix():
    head_of_lane = jnp.arange(D_INNER) // HEAD_DIM
    e = (jnp.arange(LANES)[:, None] == head_of_lane[None, :]).astype(BF16)
    return jnp.concatenate([e, e], axis=0)


def _pad_lanes(v, n=LANES):
    return jnp.pad(v.astype(F32), (0, n - v.shape[0])).reshape(1, n)


def kernel(x, c, norm1_w, norm2_w, w_ada, b_ada, w_in, conv_w, conv_b, dt_bias, a_log, d_skip,
           ssd_norm_w, q_norm_w, k_norm_w, attn_norm_w, w_out, w_ff1, w_ff2):
    bsz, s, d = x.shape
    depth = w_ada.shape[0]
    tm = 512
    e2 = _head_expand_matrix()
    idx = jnp.arange(256) // HEAD_DIM
    bd = (idx[:, None] == idx[None, :]).astype(BF16)
    t = (jnp.arange(SSD_CHUNK)[:, None] >= jnp.arange(SSD_CHUNK)[None, :]).astype(BF16)
    tril2 = jnp.concatenate([t, t], axis=1)
    o_xbc = D_INNER + CONV_CH
    o_dt = o_xbc + N_HEADS

    for l in range(depth):
        mod3 = _modulation(c, w_ada[l], b_ada[l]).reshape(bsz, N_MOD, d)
        wl = w_in[l]
        w_dt = jnp.pad(wl[:, o_xbc:o_dt], ((0, 0), (0, LANES - N_HEADS))).astype(BF16)
        z, xbc, dt, q, k, v, q16, k16, v16 = _in_projection(
            x, mod3, norm1_w[l].reshape(1, d), wl[:, :o_xbc].astype(BF16),
            wl[:, o_dt:].astype(BF16), w_dt, _pad_lanes(dt_bias[l]),
            jnp.tile(q_norm_w[l], N_HEADS).reshape(1, D_INNER),
            jnp.tile(k_norm_w[l], N_HEADS).reshape(1, D_INNER), bd, tm)
        y_ssd = _ssd_mixer(
            xbc, z, dt, conv_w[l], conv_b[l].reshape(1, CONV_CH),
            _pad_lanes(a_log[l]), jnp.repeat(d_skip[l], HEAD_DIM).reshape(1, D_INNER),
            ssd_norm_w[l].reshape(1, D_INNER), tril2, e2)
        score_bound = (HEAD_DIM ** 0.5 * LOG2E * jnp.max(jnp.abs(q_norm_w[l]))
                       * jnp.max(jnp.abs(k_norm_w[l])))
        bounded = (score_bound * BF16_SLACK <= SCORE_BOUND).astype(jnp.int32).reshape(1)
        outs, lses = [], []
        for window, dilation in DILATED_PATTERNS:
            assert window // dilation == ATT_BLK
            if dilation == 1:
                nat = lambda t: t.reshape(bsz, 1, s, t.shape[-1])
                o, lse = _dilated_attention(bounded, nat(q), nat(k), nat(v), 1)
                o, lse = o.reshape(bsz, s, D_INNER), lse.reshape(bsz, s, LANES)
            else:
                o, lse = _dilated_attention(bounded, q16, k16, v16, dilation)
            outs.append(o)
            lses.append(lse)
        x = _out_projection(x, mod3, y_ssd, outs, lses, attn_norm_w[l].reshape(1, D_INNER),
                            e2, w_out[l].astype(BF16), tm)
        x = _mlp(x, mod3, norm2_w[l].reshape(1, d), w_ff1[l].astype(BF16),
                 w_ff2[l].astype(BF16), tm)
    return x.astype(c.dtype)
```

```python
import functools

import jax
import jax.numpy as jnp
from jax import lax
from jax.experimental import pallas as pl
from jax.experimental.pallas import tpu as pltpu

F32 = jnp.float32
BF16 = jnp.bfloat16

D_MODEL = 1024
HEAD_DIM = 64
N_HEADS = 16
SSD_GROUPS = 4
SSD_STATE = 128
SSD_CONV = 4
SSD_CHUNK = 128
D_INNER = N_HEADS * HEAD_DIM
CONV_CH = D_INNER + 2 * SSD_GROUPS * SSD_STATE
D_FF = 4 * D_MODEL
N_MOD = 6
EPS = 1e-6
DILATED_PATTERNS = ((128, 1), (512, 4), (2048, 16))
ATT_BLK = 128
ATT_GROUP = 4
CLS_STRIDE = 16
LANES = 128
VMEM_LIMIT = 56 * 1024 * 1024

LOG2E = 1.4426950408889634
LN2 = 0.6931471805599453
SCORE_BOUND = 60.0
BF16_SLACK = 1.02

NT_DIMS = (((1,), (1,)), ((), ()))
TN_DIMS = (((0,), (0,)), ((), ()))


def _dot(a, b):
    return jnp.dot(a, b, preferred_element_type=F32)


def _split_bf16(v):
    hi = v.astype(BF16)
    lo = (v - hi.astype(F32)).astype(BF16)
    return hi, lo


def _silu(v):
    h = 0.5 * v
    return h + h * jnp.tanh(h)


def _const_spec(shape):
    nd = len(shape)
    return pl.BlockSpec(shape, lambda *_: (0,) * nd, pipeline_mode=pl.Buffered(1))


def _mod_kernel(c_ref, w_ref, b_ref, o_ref):
    c = c_ref[...]
    ca = _silu(c)
    c_hi, c_lo = _split_bf16(ca)
    w = w_ref[...]
    w_hi, w_lo = _split_bf16(w)
    acc = _dot(c_hi, w_hi) + _dot(c_lo, w_hi) + _dot(c_hi, w_lo)
    o_ref[...] = acc + b_ref[...]


def _modulation(c, w_ada, b_ada):
    bsz, d = c.shape
    n = w_ada.shape[1]
    tn = 1536
    return pl.pallas_call(
        _mod_kernel,
        grid=(n // tn,),
        in_specs=[
            pl.BlockSpec((bsz, d), lambda j: (0, 0)),
            pl.BlockSpec((d, tn), lambda j: (0, j)),
            pl.BlockSpec((1, tn), lambda j: (0, j)),
        ],
        out_specs=pl.BlockSpec((bsz, tn), lambda j: (0, j)),
        out_shape=jax.ShapeDtypeStruct((bsz, n), F32),
        compiler_params=pltpu.CompilerParams(
            dimension_semantics=("arbitrary",), vmem_limit_bytes=VMEM_LIMIT),
        name="adaln_mod",
    )(c, w_ada, b_ada.reshape(1, n))


IN_TN = 512


def _norm_mod(x, nw, scale, shift):
    ms = jnp.mean(x * x, axis=-1, keepdims=True)
    return (x * lax.rsqrt(ms + EPS) * nw) * (1.0 + scale) + shift


def _inproj_kernel(x_ref, mod_ref, nw_ref, wzx_ref, wqkv_ref, wdt_ref, dtb_ref, qw_ref, kw_ref,
                   bd_ref, z_ref, xbc_ref, dt_ref, q_ref, k_ref, v_ref, q16_ref, k16_ref,
                   v16_ref, h_ref, stage_ref, stage2_ref):
    x = x_ref[0]
    tm = x.shape[0]
    h = _norm_mod(x, nw_ref[...], mod_ref[0, 1:2, :], mod_ref[0, 0:1, :])
    h_ref[...] = h.astype(BF16)

    def proj(w_ref, c0, width):
        return _dot(h_ref[...], w_ref[:, c0:c0 + width])

    def qk_norm(acc, w):
        sq = (acc * acc).astype(BF16)
        parts = [_dot(sq[:, i:i + 256], bd_ref[...]) for i in range(0, IN_TN, 256)]
        ss = jnp.concatenate(parts, axis=1)
        return acc * lax.rsqrt(ss * (1.0 / HEAD_DIM) + EPS) * w

    cls_rows = x.shape[0] // CLS_STRIDE

    def emit(nat_ref, cls_ref, sl, val):
        nat_ref[0, :, sl] = val.astype(BF16)
        tiles = IN_TN // LANES
        quarter = tm // 4
        for t in range(tiles):
            stage_ref[t] = val[:, t * LANES:(t + 1) * LANES]
        for t in range(tiles):
            for rho in range(4):
                stage2_ref[t, rho * quarter:(rho + 1) * quarter, :] = (
                    stage_ref[t, pl.ds(rho, quarter, stride=4), :])
        for res in range(CLS_STRIDE):
            rho, a = res % 4, res // 4
            rows = [stage2_ref[t, pl.ds(rho * quarter + a, cls_rows, stride=4), :]
                    for t in range(tiles)]
            cls_ref[0, res, :, sl] = jnp.concatenate(rows, axis=1).astype(BF16)

    col = 0
    for j in range(D_INNER // IN_TN):
        z_ref[0, :, j * IN_TN:(j + 1) * IN_TN] = proj(wzx_ref, col, IN_TN).astype(BF16)
        col += IN_TN
    for j in range(CONV_CH // IN_TN):
        xbc_ref[0, :, j * IN_TN:(j + 1) * IN_TN] = proj(wzx_ref, col, IN_TN).astype(BF16)
        col += IN_TN
    col = 0
    for j in range(D_INNER // IN_TN):
        sl = slice(j * IN_TN, (j + 1) * IN_TN)
        emit(q_ref, q16_ref, sl,
             qk_norm(proj(wqkv_ref, col, IN_TN), qw_ref[:, sl]) * (HEAD_DIM ** -0.5 * LOG2E))
        col += IN_TN
    for j in range(D_INNER // IN_TN):
        sl = slice(j * IN_TN, (j + 1) * IN_TN)
        emit(k_ref, k16_ref, sl, qk_norm(proj(wqkv_ref, col, IN_TN), kw_ref[:, sl]))
        col += IN_TN
    for j in range(D_INNER // IN_TN):
        emit(v_ref, v16_ref, slice(j * IN_TN, (j + 1) * IN_TN), proj(wqkv_ref, col, IN_TN))
        col += IN_TN
    dt_raw = proj(wdt_ref, 0, LANES) + dtb_ref[...]
    dt_ref[0] = jnp.maximum(dt_raw, 0.0) + jnp.log(1.0 + jnp.exp(-jnp.abs(dt_raw)))


def _in_projection(x, mod3, norm_w, w_zx, w_qkv, w_dt, dt_bias, qw, kw, bd, tm):
    bsz, s, d = x.shape
    row = lambda b, i: (b, i, 0)
    out_bf = lambda n: jax.ShapeDtypeStruct((bsz, s, n), BF16)
    cls_spec = pl.BlockSpec((1, CLS_STRIDE, tm // CLS_STRIDE, D_INNER), lambda b, i: (b, 0, i, 0))
    cls_shape = jax.ShapeDtypeStruct((bsz, CLS_STRIDE, s // CLS_STRIDE, D_INNER), BF16)
    return pl.pallas_call(
        _inproj_kernel,
        grid=(bsz, s // tm),
        in_specs=[
            pl.BlockSpec((1, tm, d), row),
            pl.BlockSpec((1, N_MOD, d), lambda b, i: (b, 0, 0)),
            _const_spec((1, d)),
            _const_spec(w_zx.shape),
            _const_spec(w_qkv.shape),
            _const_spec(w_dt.shape),
            _const_spec((1, LANES)),
            _const_spec((1, D_INNER)),
            _const_spec((1, D_INNER)),
            _const_spec((256, 256)),
        ],
        out_specs=[
            pl.BlockSpec((1, tm, D_INNER), row),
            pl.BlockSpec((1, tm, CONV_CH), row),
            pl.BlockSpec((1, tm, LANES), row),
            pl.BlockSpec((1, tm, D_INNER), row),
            pl.BlockSpec((1, tm, D_INNER), row),
            pl.BlockSpec((1, tm, D_INNER), row),
            cls_spec, cls_spec, cls_spec,
        ],
        out_shape=[out_bf(D_INNER), out_bf(CONV_CH),
                   jax.ShapeDtypeStruct((bsz, s, LANES), F32),
                   out_bf(D_INNER), out_bf(D_INNER), out_bf(D_INNER),
                   cls_shape, cls_shape, cls_shape],
        scratch_shapes=[pltpu.VMEM((tm, d), BF16),
                        pltpu.VMEM((IN_TN // LANES, tm, LANES), F32),
                        pltpu.VMEM((IN_TN // LANES, tm, LANES), F32)],
        compiler_params=pltpu.CompilerParams(
            dimension_semantics=("parallel", "arbitrary"), vmem_limit_bytes=VMEM_LIMIT),
        name="in_proj",
    )(x, mod3, norm_w, w_zx, w_qkv, w_dt, dt_bias, qw, kw, bd)


SSD_ROWS = 512
CONV_HALO = 8


def _ssd_kernel(xbc_ref, z_ref, dt_ref, cw_ref, cb_ref, alog_ref, dskip_ref,
                nw_ref, tril2_ref, e2_ref, shift_ref, o_ref, halo_ref, xc_ref, state_ref):
    rows = xbc_ref.shape[1]
    L = SSD_CHUNK
    taps = SSD_CONV - 1

    @pl.when(pl.program_id(1) == 0)
    def _():
        halo_ref[...] = jnp.zeros_like(halo_ref)
        state_ref[...] = jnp.zeros_like(state_ref)

    hsub = lax.broadcasted_iota(jnp.int32, (CONV_HALO, 256), 0)

    def conv_chunk(c):
        r0 = c * L
        for c0 in range(0, CONV_CH, 256):
            cs = slice(c0, c0 + 256)
            u = xbc_ref[0, r0:r0 + L, cs]
            shifted = _dot(shift_ref[...], u)
            uf = u.astype(F32)
            acc = cb_ref[:, cs] + cw_ref[taps:taps + 1, cs] * uf
            for k in range(1, taps + 1):
                acc = acc + cw_ref[taps - k:taps - k + 1, cs] * shifted[(k - 1) * L:k * L]
            xc_ref[r0:r0 + L, cs] = _silu(acc)
            if c == 0:
                halo = halo_ref[:, cs]
            else:
                halo = xbc_ref[0, r0 - 2 * CONV_HALO:r0, cs].astype(F32)[CONV_HALO:]
            head = acc[0:CONV_HALO]
            for k in range(1, taps + 1):
                prev = jnp.where(hsub < k, pltpu.roll(halo, k, axis=0), 0.0)
                head = head + cw_ref[taps - k:taps - k + 1, cs] * prev
            xc_ref[r0:r0 + CONV_HALO, cs] = _silu(head)
            if r0 + L == rows:
                halo_ref[:, cs] = uf[L - CONV_HALO:L]

    lane = lax.broadcasted_iota(jnp.int32, (L, LANES), 1)
    sub = lax.broadcasted_iota(jnp.int32, (L, LANES), 0)
    tril = sub >= lane
    lo_half = lane < HEAD_DIM
    a_neg = jnp.where(lane[0:1] < N_HEADS, -jnp.exp(alog_ref[...]), 0.0)

    def expand(v):
        hi, lo = _split_bf16(v)
        return _dot(jnp.concatenate([hi, lo], axis=1), e2_ref[...])

    conv_chunk(0)
    for c in range(rows // L):
        if c + 1 < rows // L:
            conv_chunk(c + 1)
        r0 = c * L
        rs = slice(r0, r0 + L)
        dt = dt_ref[0, rs, :]
        d_a = dt * a_neg
        da_hi, da_lo = _split_bf16(d_a)
        a_cs = _dot(tril2_ref[...], jnp.concatenate([da_hi, da_lo], axis=0))
        a_cs_t = a_cs.T
        a_last = a_cs[L - 1:L, :]
        dec_in = jnp.exp(a_last - a_cs)
        dec_out = jnp.exp(a_cs)
        dt_x = expand(dt)
        w_x = expand(dt * dec_in)
        do_x = expand(dec_out)

        xs = xc_ref[rs, 0:D_INNER]
        xdt = (xs * dt_x).astype(BF16)
        xw = (xs * w_x).astype(BF16)
        y = dskip_ref[...] * xs
        z = z_ref[0, rs, :].astype(F32)
        gate = _silu(z)

        for g in range(SSD_GROUPS):
            b0 = D_INNER + g * SSD_STATE
            c0 = D_INNER + SSD_GROUPS * SSD_STATE + g * SSD_STATE
            bg = xc_ref[rs, b0:b0 + SSD_STATE].astype(BF16)
            cg = xc_ref[rs, c0:c0 + SSD_STATE].astype(BF16)
            cb = lax.dot_general(cg, bg, NT_DIMS, preferred_element_type=F32)
            gs = slice(g * 256, (g + 1) * 256)
            yd_parts = []
            for pair in range(2):
                ms = []
                for e in range(2):
                    hd = g * 4 + pair * 2 + e
                    colb = jnp.broadcast_to(a_cs[:, hd:hd + 1], (L, L))
                    rowb = jnp.broadcast_to(a_cs_t[hd:hd + 1, :], (L, L))
                    lmat = jnp.exp(jnp.where(tril, colb - rowb, -jnp.inf))
                    ms.append((cb * lmat).astype(BF16))
                mcat = jnp.concatenate(ms, axis=1)
                p0 = g * 256 + pair * LANES
                xp = xdt[:, p0:p0 + LANES]
                zero = jnp.zeros_like(xp)
                xstack = jnp.concatenate(
                    [jnp.where(lo_half, xp, zero), jnp.where(lo_half, zero, xp)], axis=0)
                yd_parts.append(_dot(mcat, xstack))
            y_diag = jnp.concatenate(yd_parts, axis=1)
            st = state_ref[:, gs]
            y_off = _dot(cg, st.astype(BF16)) * do_x[:, gs]
            upd = lax.dot_general(bg, xw[:, gs], TN_DIMS, preferred_element_type=F32)
            state_ref[:, gs] = st * do_x[L - 1:L, gs] + upd
            yg = (y[:, gs] + y_diag + y_off) * gate[:, gs]
            ss = jnp.mean(yg * yg, axis=-1, keepdims=True)
            o_ref[0, rs, gs] = (yg * lax.rsqrt(ss + EPS) * nw_ref[:, gs]).astype(BF16)


def _ssd_mixer(xbc, z, dt, conv_w, conv_b, a_log, d_skip, norm_w, tril2, e2):
    bsz, s, _ = xbc.shape
    rows = SSD_ROWS
    row = lambda b, i: (b, i, 0)
    t = jnp.arange(SSD_CHUNK)
    shift = jnp.concatenate(
        [(t[:, None] - k == t[None, :]).astype(BF16) for k in range(1, SSD_CONV)], axis=0)
    return pl.pallas_call(
        _ssd_kernel,
        grid=(bsz, s // rows),
        in_specs=[
            pl.BlockSpec((1, rows, CONV_CH), row),
            pl.BlockSpec((1, rows, D_INNER), row),
            pl.BlockSpec((1, rows, LANES), row),
            _const_spec((SSD_CONV, CONV_CH)),
            _const_spec((1, CONV_CH)),
            _const_spec((1, LANES)),
            _const_spec((1, D_INNER)),
            _const_spec((1, D_INNER)),
            _const_spec((SSD_CHUNK, 2 * SSD_CHUNK)),
            _const_spec((2 * LANES, D_INNER)),
            _const_spec(((SSD_CONV - 1) * SSD_CHUNK, SSD_CHUNK)),
        ],
        out_specs=pl.BlockSpec((1, rows, D_INNER), row),
        out_shape=jax.ShapeDtypeStruct((bsz, s, D_INNER), BF16),
        scratch_shapes=[
            pltpu.VMEM((CONV_HALO, CONV_CH), F32),
            pltpu.VMEM((rows, CONV_CH), F32),
            pltpu.VMEM((SSD_STATE, D_INNER), F32),
        ],
        compiler_params=pltpu.CompilerParams(
            dimension_semantics=("parallel", "arbitrary"), vmem_limit_bytes=VMEM_LIMIT),
        name="ssd_mixer",
    )(xbc, z, dt, conv_w, conv_b, a_log, d_skip, norm_w, tril2, e2, shift)


def _attn_kernel(planes, group, bounded_ref, q_ref, k_ref, v_ref, kp_ref, vp_ref, o_ref, lse_ref):
    blk = ATT_BLK
    sub = blk // planes
    first = pl.program_id(2) == 0

    def pos(i):
        return i if planes == 1 else (i % sub) * planes + i // sub

    row = lax.broadcasted_iota(jnp.int32, (blk, 2 * blk), 0)
    key = lax.broadcasted_iota(jnp.int32, (blk, 2 * blk), 1)
    rel = jnp.where(key >= blk, blk + pos(key - blk), pos(key)) - pos(row)
    band = (rel >= 0) & (rel <= blk)
    lo_key = jnp.where(first, blk, 0)
    bias_head = jnp.where(band & (key >= lo_key), 0.0, -jnp.inf)
    bias_head = jnp.concatenate([bias_head, bias_head], axis=0)
    bias_rest = jnp.where(band, 0.0, -jnp.inf)
    bias_rest = jnp.concatenate([bias_rest, bias_rest], axis=0)
    lane = lax.broadcasted_iota(jnp.int32, (blk, LANES), 1)
    lo_half = lane < HEAD_DIM

    n_res = q_ref.shape[1] if planes == 1 else 1

    def load(ref, r, g, ps):
        if planes == 1:
            return ref[0, r, g * blk:(g + 1) * blk, ps]
        return ref[0, :, 0, g * sub:(g + 1) * sub, ps].reshape(blk, LANES)

    def store(ref, r, g, ps, val):
        if planes == 1:
            ref[0, r, g * blk:(g + 1) * blk, ps] = val
        else:
            ref[0, :, 0, g * sub:(g + 1) * sub, ps] = val.reshape(planes, sub, val.shape[-1])

    def block(bounded, r, g):
        store(lse_ref, r, g, slice(0, LANES), jnp.zeros((blk, LANES), F32))
        for p in range(N_HEADS // 2):
            ps = slice(p * LANES, (p + 1) * LANES)
            qp = load(q_ref, r, g, ps)
            zero = jnp.zeros_like(qp)
            qs = jnp.concatenate(
                [jnp.where(lo_half, qp, zero), jnp.where(lo_half, zero, qp)], axis=0)
            k_prev = load(kp_ref, r, 0, ps) if g == 0 else load(k_ref, r, g - 1, ps)
            v_prev = load(vp_ref, r, 0, ps) if g == 0 else load(v_ref, r, g - 1, ps)
            k2 = jnp.concatenate([k_prev, load(k_ref, r, g, ps)], axis=0)
            v2 = jnp.concatenate([v_prev, load(v_ref, r, g, ps)], axis=0)
            s = lax.dot_general(qs, k2, NT_DIMS, preferred_element_type=F32)
            s = s + (bias_head if g == 0 else bias_rest)
            if bounded:
                e = jnp.exp2(s)
            else:
                m = jnp.max(s, axis=-1, keepdims=True)
                e = jnp.exp2(s - m)
            l = jnp.sum(e, axis=-1, keepdims=True)
            pv = _dot(e.astype(BF16), v2)
            store(o_ref, r, g, ps, jnp.where(lo_half, pv[0:blk], pv[blk:]).astype(BF16))
            store(lse_ref, r, g, slice(2 * p, 2 * p + 1), l[0:blk])
            store(lse_ref, r, g, slice(2 * p + 1, 2 * p + 2), l[blk:])
            if not bounded:
                c0 = N_HEADS + 2 * p
                store(lse_ref, r, g, slice(c0, c0 + 1), m[0:blk])
                store(lse_ref, r, g, slice(c0 + 1, c0 + 2), m[blk:])

    def body(bounded):
        for r in range(n_res):
            for g in range(group):
                block(bounded, r, g)

    bounded = bounded_ref[0] != 0

    @pl.when(bounded)
    def _():
        body(True)

    @pl.when(jnp.logical_not(bounded))
    def _():
        body(False)


def _dilated_attention(bounded, q, k, v, dilation):
    bsz, n_planes, rows, w = q.shape
    planes = n_planes // dilation
    assert planes in (1, 4)
    sub = ATT_BLK // planes
    nb = rows // sub
    group = min(ATT_GROUP, nb)
    n_res = 1
    if planes == 1:
        n_res = min(ATT_GROUP // group, dilation)
        view = lambda t: t
        blk_shape = lambda n, width: (1, n_res, n * ATT_BLK, width)
        cur = lambda b, res, j: (b, res, j, 0)
        prev = lambda b, res, j: (b, res, jnp.maximum(j * group - 1, 0), 0)
    else:
        view = lambda t: t.reshape(bsz, planes, dilation, rows, t.shape[-1])
        blk_shape = lambda n, width: (1, planes, 1, n * sub, width)
        cur = lambda b, res, j: (b, 0, res, j, 0)
        prev = lambda b, res, j: (b, 0, res, jnp.maximum(j * group - 1, 0), 0)
    tile = lambda width=w: pl.BlockSpec(blk_shape(group, width), cur)
    single = pl.BlockSpec(blk_shape(1, w), prev)
    lse_shape = (bsz, n_planes, rows, LANES)
    o, lse = pl.pallas_call(
        functools.partial(_attn_kernel, planes, group),
        grid=(bsz, dilation // n_res, nb // group),
        in_specs=[pl.BlockSpec(memory_space=pltpu.SMEM), tile(), tile(), tile(), single, single],
        out_specs=[tile(), tile(LANES)],
        out_shape=[jax.ShapeDtypeStruct(_view_shape(q.shape, planes, dilation), BF16),
                   jax.ShapeDtypeStruct(_view_shape(lse_shape, planes, dilation), F32)],
        compiler_params=pltpu.CompilerParams(
            dimension_semantics=("parallel", "parallel", "arbitrary"),
            vmem_limit_bytes=VMEM_LIMIT),
        name=f"dilated_attn_r{dilation}",
    )(bounded, view(q), view(k), view(v), view(k), view(v))
    return o.reshape(q.shape), lse.reshape(lse_shape)


def _view_shape(shape, planes, dilation):
    bsz, _, rows, width = shape
    return shape if planes == 1 else (bsz, planes, dilation, rows, width)


def _outproj_kernel(x_ref, mod_ref, ys_ref, o1_ref, o2_ref, o3_ref, l1_ref, l2_ref, l3_ref,
                    nw_ref, e2_ref, w_ref, out_ref, wide_ref, wide2_ref, narrow_ref, narrow2_ref):
    cls_rows = x_ref.shape[1] // CLS_STRIDE

    def natural(cls_ref, stage_ref, stage2_ref):
        tiles = stage_ref.shape[0]
        quarter = cls_rows * 4
        for res in range(CLS_STRIDE):
            rho, a = res % 4, res // 4
            val = cls_ref[0, res].astype(F32)
            for t in range(tiles):
                stage2_ref[t, pl.ds(rho * quarter + a, cls_rows, stride=4), :] = (
                    val[:, t * LANES:(t + 1) * LANES])
        for t in range(tiles):
            for rho in range(4):
                stage_ref[t, pl.ds(rho, quarter, stride=4), :] = (
                    stage2_ref[t, rho * quarter:(rho + 1) * quarter, :])
        return jnp.concatenate([stage_ref[t] for t in range(tiles)], axis=1)

    s1 = l1_ref[0]
    s2 = natural(l2_ref, narrow_ref, narrow2_ref)
    s3 = natural(l3_ref, narrow_ref, narrow2_ref)
    shifts = [pltpu.roll(s, LANES - N_HEADS, axis=1) for s in (s1, s2, s3)]
    mx = jnp.maximum(jnp.maximum(shifts[0], shifts[1]), shifts[2])
    w1, w2, w3 = [jnp.exp2(m - mx) for m in shifts]
    inv = 1.0 / (w1 * s1 + w2 * s2 + w3 * s3)
    head_lane = lax.broadcasted_iota(jnp.int32, s1.shape, 1) < N_HEADS

    def expand(v):
        hi, lo = _split_bf16(jnp.where(head_lane, v, 0.0))
        return _dot(jnp.concatenate([hi, lo], axis=1), e2_ref[...])

    o = expand(w1 * inv) * o1_ref[0].astype(F32)
    o = o + expand(w2 * inv) * natural(o2_ref, wide_ref, wide2_ref)
    o = o + expand(w3 * inv) * natural(o3_ref, wide_ref, wide2_ref)
    ms = jnp.mean(o * o, axis=-1, keepdims=True)
    y_att = (o * lax.rsqrt(ms + EPS) * nw_ref[...]).astype(BF16)
    mix = _dot(ys_ref[0], w_ref[0:D_INNER, :]) + _dot(y_att, w_ref[D_INNER:, :])
    out_ref[0] = x_ref[0] + mod_ref[0, 2:3, :] * mix


def _out_projection(x, mod3, y_ssd, outs, lses, norm_w, e2, w_out, tm):
    bsz, s, d = x.shape
    row = lambda b, i: (b, i, 0)
    wide = pl.BlockSpec((1, tm, D_INNER), row)
    narrow = pl.BlockSpec((1, tm, LANES), row)
    cls = lambda width: pl.BlockSpec((1, CLS_STRIDE, tm // CLS_STRIDE, width),
                                     lambda b, i: (b, 0, i, 0))
    return pl.pallas_call(
        _outproj_kernel,
        grid=(bsz, s // tm),
        in_specs=[
            pl.BlockSpec((1, tm, d), row),
            pl.BlockSpec((1, N_MOD, d), lambda b, i: (b, 0, 0)),
            wide, wide, cls(D_INNER), cls(D_INNER), narrow, cls(LANES), cls(LANES),
            _const_spec((1, D_INNER)),
            _const_spec((2 * LANES, D_INNER)),
            _const_spec((2 * D_INNER, d)),
        ],
        out_specs=pl.BlockSpec((1, tm, d), row),
        out_shape=jax.ShapeDtypeStruct((bsz, s, d), F32),
        scratch_shapes=[pltpu.VMEM((D_INNER // LANES, tm, LANES), F32),
                        pltpu.VMEM((D_INNER // LANES, tm, LANES), F32),
                        pltpu.VMEM((1, tm, LANES), F32),
                        pltpu.VMEM((1, tm, LANES), F32)],
        compiler_params=pltpu.CompilerParams(
            dimension_semantics=("parallel", "arbitrary"), vmem_limit_bytes=VMEM_LIMIT),
        name="out_proj",
    )(x, mod3, y_ssd, *outs, *lses, norm_w, e2, w_out)


FF_TN = 1024


def _mlp_kernel(x_ref, mod_ref, nw_ref, w1_ref, w2_ref, out_ref, h_ref):
    x = x_ref[0]
    h = _norm_mod(x, nw_ref[...], mod_ref[0, 4:5, :], mod_ref[0, 3:4, :])
    h_ref[...] = h.astype(BF16)
    acc = jnp.zeros(x.shape, F32)
    for j in range(D_FF // FF_TN):
        fs = slice(j * FF_TN, (j + 1) * FF_TN)
        u = jnp.maximum(_dot(h_ref[...], w1_ref[:, fs]), 0.0)
        acc = acc + _dot((u * u).astype(BF16), w2_ref[fs, :])
    out_ref[0] = x + mod_ref[0, 5:6, :] * acc


def _mlp(x, mod3, norm_w, w1, w2, tm):
    bsz, s, d = x.shape
    row = lambda b, i: (b, i, 0)
    return pl.pallas_call(
        _mlp_kernel,
        grid=(bsz, s // tm),
        in_specs=[
            pl.BlockSpec((1, tm, d), row),
            pl.BlockSpec((1, N_MOD, d), lambda b, i: (b, 0, 0)),
            _const_spec((1, d)),
            _const_spec((d, D_FF)),
            _const_spec((D_FF, d)),
        ],
        out_specs=pl.BlockSpec((1, tm, d), row),
        out_shape=jax.ShapeDtypeStruct((bsz, s, d), F32),
        scratch_shapes=[pltpu.VMEM((tm, d), BF16)],
        compiler_params=pltpu.CompilerParams(
            dimension_semantics=("parallel", "arbitrary"), vmem_limit_bytes=VMEM_LIMIT),
        name="mlp",
    )(x, mod3, norm_w, w1, w2)


def _head_expand_matrix():
    head_of_lane = jnp.arange(D_INNER) // HEAD_DIM
    e = (jnp.arange(LANES)[:, None] == head_of_lane[None, :]).astype(BF16)
    return jnp.concatenate([e, e], axis=0)


def _pad_lanes(v, n=LANES):
    return jnp.pad(v.astype(F32), (0, n - v.shape[0])).reshape(1, n)


def kernel(x, c, norm1_w, norm2_w, w_ada, b_ada, w_in, conv_w, conv_b, dt_bias, a_log, d_skip,
           ssd_norm_w, q_norm_w, k_norm_w, attn_norm_w, w_out, w_ff1, w_ff2):
    bsz, s, d = x.shape
    depth = w_ada.shape[0]
    tm = 512
    e2 = _head_expand_matrix()
    idx = jnp.arange(256) // HEAD_DIM
    bd = (idx[:, None] == idx[None, :]).astype(BF16)
    t = (jnp.arange(SSD_CHUNK)[:, None] >= jnp.arange(SSD_CHUNK)[None, :]).astype(BF16)
    tril2 = jnp.concatenate([t, t], axis=1)
    o_xbc = D_INNER + CONV_CH
    o_dt = o_xbc + N_HEADS

    for l in range(depth):
        mod3 = _modulation(c, w_ada[l], b_ada[l]).reshape(bsz, N_MOD, d)
        wl = w_in[l]
        w_dt = jnp.pad(wl[:, o_xbc:o_dt], ((0, 0), (0, LANES - N_HEADS))).astype(BF16)
        z, xbc, dt, q, k, v, q16, k16, v16 = _in_projection(
            x, mod3, norm1_w[l].reshape(1, d), wl[:, :o_xbc].astype(BF16),
            wl[:, o_dt:].astype(BF16), w_dt, _pad_lanes(dt_bias[l]),
            jnp.tile(q_norm_w[l], N_HEADS).reshape(1, D_INNER),
            jnp.tile(k_norm_w[l], N_HEADS).reshape(1, D_INNER), bd, tm)
        y_ssd = _ssd_mixer(
            xbc, z, dt, conv_w[l], conv_b[l].reshape(1, CONV_CH),
            _pad_lanes(a_log[l]), jnp.repeat(d_skip[l], HEAD_DIM).reshape(1, D_INNER),
            ssd_norm_w[l].reshape(1, D_INNER), tril2, e2)
        score_bound = (HEAD_DIM ** 0.5 * LOG2E * jnp.max(jnp.abs(q_norm_w[l]))
                       * jnp.max(jnp.abs(k_norm_w[l])))
        bounded = (score_bound * BF16_SLACK <= SCORE_BOUND).astype(jnp.int32).reshape(1)
        outs, lses = [], []
        for window, dilation in DILATED_PATTERNS:
            assert window // dilation == ATT_BLK
            if dilation == 1:
                nat = lambda t: t.reshape(bsz, 1, s, t.shape[-1])
                o, lse = _dilated_attention(bounded, nat(q), nat(k), nat(v), 1)
                o, lse = o.reshape(bsz, s, D_INNER), lse.reshape(bsz, s, LANES)
            else:
                o, lse = _dilated_attention(bounded, q16, k16, v16, dilation)
            outs.append(o)
            lses.append(lse)
        x = _out_projection(x, mod3, y_ssd, outs, lses, attn_norm_w[l].reshape(1, D_INNER),
                            e2, w_out[l].astype(BF16), tm)
        x = _mlp(x, mod3, norm2_w[l].reshape(1, d), w_ff1[l].astype(BF16),
                 w_ff2[l].astype(BF16), tm)
    return x.astype(c.dtype)
```

```python
import functools

import jax
import jax.numpy as jnp
from jax import lax
from jax.experimental import pallas as pl
from jax.experimental.pallas import tpu as pltpu

F32 = jnp.float32
BF16 = jnp.bfloat16

D_MODEL = 1024
HEAD_DIM = 64
N_HEADS = 16
SSD_GROUPS = 4
SSD_STATE = 128
SSD_CONV = 4
SSD_CHUNK = 128
D_INNER = N_HEADS * HEAD_DIM
CONV_CH = D_INNER + 2 * SSD_GROUPS * SSD_STATE
D_FF = 4 * D_MODEL
N_MOD = 6
EPS = 1e-6
DILATED_PATTERNS = ((128, 1), (512, 4), (2048, 16))
ATT_BLK = 128
ATT_GROUP = 8
CLS_STRIDE = 16
LANES = 128
VMEM_LIMIT = 56 * 1024 * 1024

LOG2E = 1.4426950408889634
LN2 = 0.6931471805599453
SCORE_BOUND = 60.0
BF16_SLACK = 1.02

NT_DIMS = (((1,), (1,)), ((), ()))
TN_DIMS = (((0,), (0,)), ((), ()))


def _dot(a, b):
    return jnp.dot(a, b, preferred_element_type=F32)


def _split_bf16(v):
    hi = v.astype(BF16)
    lo = (v - hi.astype(F32)).astype(BF16)
    return hi, lo


def _silu(v):
    h = 0.5 * v
    return h + h * jnp.tanh(h)


def _const_spec(shape):
    nd = len(shape)
    return pl.BlockSpec(shape, lambda *_: (0,) * nd, pipeline_mode=pl.Buffered(1))


def _mod_kernel(c_ref, w_ref, b_ref, o_ref):
    c = c_ref[...]
    ca = _silu(c)
    c_hi, c_lo = _split_bf16(ca)
    w = w_ref[...]
    w_hi, w_lo = _split_bf16(w)
    acc = _dot(c_hi, w_hi) + _dot(c_lo, w_hi) + _dot(c_hi, w_lo)
    o_ref[...] = acc + b_ref[...]


def _modulation(c, w_ada, b_ada):
    bsz, d = c.shape
    n = w_ada.shape[1]
    tn = 1536
    return pl.pallas_call(
        _mod_kernel,
        grid=(n // tn,),
        in_specs=[
            pl.BlockSpec((bsz, d), lambda j: (0, 0)),
            pl.BlockSpec((d, tn), lambda j: (0, j)),
            pl.BlockSpec((1, tn), lambda j: (0, j)),
        ],
        out_specs=pl.BlockSpec((bsz, tn), lambda j: (0, j)),
        out_shape=jax.ShapeDtypeStruct((bsz, n), F32),
        compiler_params=pltpu.CompilerParams(
            dimension_semantics=("arbitrary",), vmem_limit_bytes=VMEM_LIMIT),
        name="adaln_mod",
    )(c, w_ada, b_ada.reshape(1, n))


IN_TN = 512


def _norm_mod(x, nw, scale, shift):
    ms = jnp.mean(x * x, axis=-1, keepdims=True)
    return (x * lax.rsqrt(ms + EPS) * nw) * (1.0 + scale) + shift


def _inproj_kernel(x_ref, mod_ref, nw_ref, wzx_ref, wqkv_ref, wdt_ref, dtb_ref, qw_ref, kw_ref,
                   bd_ref, z_ref, xbc_ref, dt_ref, q_ref, k_ref, v_ref, q16_ref, k16_ref,
                   v16_ref, h_ref, stage_ref, stage2_ref):
    x = x_ref[0]
    tm = x.shape[0]
    h = _norm_mod(x, nw_ref[...], mod_ref[0, 1:2, :], mod_ref[0, 0:1, :])
    h_ref[...] = h.astype(BF16)

    def proj(w_ref, c0, width):
        return _dot(h_ref[...], w_ref[:, c0:c0 + width])

    def qk_norm(acc, w):
        sq = (acc * acc).astype(BF16)
        parts = [_dot(sq[:, i:i + 256], bd_ref[...]) for i in range(0, IN_TN, 256)]
        ss = jnp.concatenate(parts, axis=1)
        return acc * lax.rsqrt(ss * (1.0 / HEAD_DIM) + EPS) * w

    cls_rows = x.shape[0] // CLS_STRIDE

    def emit(nat_ref, cls_ref, sl, val):
        nat_ref[0, :, sl] = val.astype(BF16)
        tiles = IN_TN // LANES
        quarter = tm // 4
        for t in range(tiles):
            stage_ref[t] = val[:, t * LANES:(t + 1) * LANES]
        for t in range(tiles):
            for rho in range(4):
                stage2_ref[t, rho * quarter:(rho + 1) * quarter, :] = (
                    stage_ref[t, pl.ds(rho, quarter, stride=4), :])
        for res in range(CLS_STRIDE):
            rho, a = res % 4, res // 4
            rows = [stage2_ref[t, pl.ds(rho * quarter + a, cls_rows, stride=4), :]
                    for t in range(tiles)]
            cls_ref[0, res, :, sl] = jnp.concatenate(rows, axis=1).astype(BF16)

    col = 0
    for j in range(D_INNER // IN_TN):
        z_ref[0, :, j * IN_TN:(j + 1) * IN_TN] = proj(wzx_ref, col, IN_TN).astype(BF16)
        col += IN_TN
    for j in range(CONV_CH // IN_TN):
        xbc_ref[0, :, j * IN_TN:(j + 1) * IN_TN] = proj(wzx_ref, col, IN_TN).astype(BF16)
        col += IN_TN
    col = 0
    for j in range(D_INNER // IN_TN):
        sl = slice(j * IN_TN, (j + 1) * IN_TN)
        emit(q_ref, q16_ref, sl,
             qk_norm(proj(wqkv_ref, col, IN_TN), qw_ref[:, sl]) * (HEAD_DIM ** -0.5 * LOG2E))
        col += IN_TN
    for j in range(D_INNER // IN_TN):
        sl = slice(j * IN_TN, (j + 1) * IN_TN)
        emit(k_ref, k16_ref, sl, qk_norm(proj(wqkv_ref, col, IN_TN), kw_ref[:, sl]))
        col += IN_TN
    for j in range(D_INNER // IN_TN):
        emit(v_ref, v16_ref, slice(j * IN_TN, (j + 1) * IN_TN), proj(wqkv_ref, col, IN_TN))
        col += IN_TN
    dt_raw = proj(wdt_ref, 0, LANES) + dtb_ref[...]
    dt_ref[0] = jnp.maximum(dt_raw, 0.0) + jnp.log(1.0 + jnp.exp(-jnp.abs(dt_raw)))


def _in_projection(x, mod3, norm_w, w_zx, w_qkv, w_dt, dt_bias, qw, kw, bd, tm):
    bsz, s, d = x.shape
    row = lambda b, i: (b, i, 0)
    out_bf = lambda n: jax.ShapeDtypeStruct((bsz, s, n), BF16)
    cls_spec = pl.BlockSpec((1, CLS_STRIDE, tm // CLS_STRIDE, D_INNER), lambda b, i: (b, 0, i, 0))
    cls_shape = jax.ShapeDtypeStruct((bsz, CLS_STRIDE, s // CLS_STRIDE, D_INNER), BF16)
    return pl.pallas_call(
        _inproj_kernel,
        grid=(bsz, s // tm),
        in_specs=[
            pl.BlockSpec((1, tm, d), row),
            pl.BlockSpec((1, N_MOD, d), lambda b, i: (b, 0, 0)),
            _const_spec((1, d)),
            _const_spec(w_zx.shape),
            _const_spec(w_qkv.shape),
            _const_spec(w_dt.shape),
            _const_spec((1, LANES)),
            _const_spec((1, D_INNER)),
            _const_spec((1, D_INNER)),
            _const_spec((256, 256)),
        ],
        out_specs=[
            pl.BlockSpec((1, tm, D_INNER), row),
            pl.BlockSpec((1, tm, CONV_CH), row),
            pl.BlockSpec((1, tm, LANES), row),
            pl.BlockSpec((1, tm, D_INNER), row),
            pl.BlockSpec((1, tm, D_INNER), row),
            pl.BlockSpec((1, tm, D_INNER), row),
            cls_spec, cls_spec, cls_spec,
        ],
        out_shape=[out_bf(D_INNER), out_bf(CONV_CH),
                   jax.ShapeDtypeStruct((bsz, s, LANES), F32),
                   out_bf(D_INNER), out_bf(D_INNER), out_bf(D_INNER),
                   cls_shape, cls_shape, cls_shape],
        scratch_shapes=[pltpu.VMEM((tm, d), BF16),
                        pltpu.VMEM((IN_TN // LANES, tm, LANES), F32),
                        pltpu.VMEM((IN_TN // LANES, tm, LANES), F32)],
        compiler_params=pltpu.CompilerParams(
            dimension_semantics=("parallel", "arbitrary"), vmem_limit_bytes=VMEM_LIMIT),
        name="in_proj",
    )(x, mod3, norm_w, w_zx, w_qkv, w_dt, dt_bias, qw, kw, bd)


SSD_ROWS = 512
CONV_HALO = 8


def _ssd_kernel(xbc_ref, z_ref, dt_ref, cw_ref, cb_ref, alog_ref, dskip_ref,
                nw_ref, tril2_ref, e2_ref, shift_ref, o_ref, halo_ref, xc_ref, state_ref):
    rows = xbc_ref.shape[1]
    L = SSD_CHUNK
    taps = SSD_CONV - 1

    @pl.when(pl.program_id(1) == 0)
    def _():
        halo_ref[...] = jnp.zeros_like(halo_ref)
        state_ref[...] = jnp.zeros_like(state_ref)

    hsub = lax.broadcasted_iota(jnp.int32, (CONV_HALO, 256), 0)

    def conv_chunk(c):
        r0 = c * L
        for c0 in range(0, CONV_CH, 256):
            cs = slice(c0, c0 + 256)
            u = xbc_ref[0, r0:r0 + L, cs]
            shifted = _dot(shift_ref[...], u)
            uf = u.astype(F32)
            acc = cb_ref[:, cs] + cw_ref[taps:taps + 1, cs] * uf
            for k in range(1, taps + 1):
                acc = acc + cw_ref[taps - k:taps - k + 1, cs] * shifted[(k - 1) * L:k * L]
            xc_ref[r0:r0 + L, cs] = _silu(acc)
            if c == 0:
                halo = halo_ref[:, cs]
            else:
                halo = xbc_ref[0, r0 - 2 * CONV_HALO:r0, cs].astype(F32)[CONV_HALO:]
            head = acc[0:CONV_HALO]
            for k in range(1, taps + 1):
                prev = jnp.where(hsub < k, pltpu.roll(halo, k, axis=0), 0.0)
                head = head + cw_ref[taps - k:taps - k + 1, cs] * prev
            xc_ref[r0:r0 + CONV_HALO, cs] = _silu(head)
            if r0 + L == rows:
                halo_ref[:, cs] = uf[L - CONV_HALO:L]

    lane = lax.broadcasted_iota(jnp.int32, (L, LANES), 1)
    sub = lax.broadcasted_iota(jnp.int32, (L, LANES), 0)
    tril = sub >= lane
    lo_half = lane < HEAD_DIM
    a_neg = jnp.where(lane[0:1] < N_HEADS, -jnp.exp(alog_ref[...]), 0.0)

    def split_cat(v):
        hi, lo = _split_bf16(v)
        return jnp.concatenate([hi, lo], axis=1)

    def expand(v_cat, gs):
        return _dot(v_cat, e2_ref[:, gs])

    conv_chunk(0)
    for c in range(rows // L):
        if c + 1 < rows // L:
            conv_chunk(c + 1)
        r0 = c * L
        rs = slice(r0, r0 + L)
        dt = dt_ref[0, rs, :]
        d_a = dt * a_neg
        da_hi, da_lo = _split_bf16(d_a)
        a_cs = _dot(tril2_ref[...], jnp.concatenate([da_hi, da_lo], axis=0))
        a_cs_t = a_cs.T
        a_last = a_cs[L - 1:L, :]
        dt_cat = split_cat(dt)
        w_cat = split_cat(dt * jnp.exp(a_last - a_cs))
        do_cat = split_cat(jnp.exp(a_cs))

        for g in range(SSD_GROUPS):
            gs = slice(g * 256, (g + 1) * 256)
            xs = xc_ref[rs, gs]
            do_x = expand(do_cat, gs)
            xdt = (xs * expand(dt_cat, gs)).astype(BF16)
            xw = (xs * expand(w_cat, gs)).astype(BF16)
            b0 = D_INNER + g * SSD_STATE
            c0 = D_INNER + SSD_GROUPS * SSD_STATE + g * SSD_STATE
            bg = xc_ref[rs, b0:b0 + SSD_STATE].astype(BF16)
            cg = xc_ref[rs, c0:c0 + SSD_STATE].astype(BF16)
            cb = lax.dot_general(cg, bg, NT_DIMS, preferred_element_type=F32)
            yd_parts = []
            for pair in range(2):
                ms = []
                for e in range(2):
                    hd = g * 4 + pair * 2 + e
                    colb = jnp.broadcast_to(a_cs[:, hd:hd + 1], (L, L))
                    rowb = jnp.broadcast_to(a_cs_t[hd:hd + 1, :], (L, L))
                    lmat = jnp.exp(jnp.where(tril, colb - rowb, -jnp.inf))
                    ms.append((cb * lmat).astype(BF16))
                mcat = jnp.concatenate(ms, axis=1)
                xp = xdt[:, pair * LANES:(pair + 1) * LANES]
                zero = jnp.zeros_like(xp)
                xstack = jnp.concatenate(
                    [jnp.where(lo_half, xp, zero), jnp.where(lo_half, zero, xp)], axis=0)
                yd_parts.append(_dot(mcat, xstack))
            y_diag = jnp.concatenate(yd_parts, axis=1)
            st = state_ref[:, gs]
            y_off = _dot(cg, st.astype(BF16)) * do_x
            upd = lax.dot_general(bg, xw, TN_DIMS, preferred_element_type=F32)
            state_ref[:, gs] = st * do_x[L - 1:L] + upd
            gate = _silu(z_ref[0, rs, gs].astype(F32))
            yg = (dskip_ref[:, gs] * xs + y_diag + y_off) * gate
            ss = jnp.mean(yg * yg, axis=-1, keepdims=True)
            o_ref[0, rs, gs] = (yg * lax.rsqrt(ss + EPS) * nw_ref[:, gs]).astype(BF16)


def _ssd_mixer(xbc, z, dt, conv_w, conv_b, a_log, d_skip, norm_w, tril2, e2):
    bsz, s, _ = xbc.shape
    rows = SSD_ROWS
    row = lambda b, i: (b, i, 0)
    t = jnp.arange(SSD_CHUNK)
    shift = jnp.concatenate(
        [(t[:, None] - k == t[None, :]).astype(BF16) for k in range(1, SSD_CONV)], axis=0)
    return pl.pallas_call(
        _ssd_kernel,
        grid=(bsz, s // rows),
        in_specs=[
            pl.BlockSpec((1, rows, CONV_CH), row),
            pl.BlockSpec((1, rows, D_INNER), row),
            pl.BlockSpec((1, rows, LANES), row),
            _const_spec((SSD_CONV, CONV_CH)),
            _const_spec((1, CONV_CH)),
            _const_spec((1, LANES)),
            _const_spec((1, D_INNER)),
            _const_spec((1, D_INNER)),
            _const_spec((SSD_CHUNK, 2 * SSD_CHUNK)),
            _const_spec((2 * LANES, D_INNER)),
            _const_spec(((SSD_CONV - 1) * SSD_CHUNK, SSD_CHUNK)),
        ],
        out_specs=pl.BlockSpec((1, rows, D_INNER), row),
        out_shape=jax.ShapeDtypeStruct((bsz, s, D_INNER), BF16),
        scratch_shapes=[
            pltpu.VMEM((CONV_HALO, CONV_CH), F32),
            pltpu.VMEM((rows, CONV_CH), F32),
            pltpu.VMEM((SSD_STATE, D_INNER), F32),
        ],
        compiler_params=pltpu.CompilerParams(
            dimension_semantics=("parallel", "arbitrary"), vmem_limit_bytes=VMEM_LIMIT),
        name="ssd_mixer",
    )(xbc, z, dt, conv_w, conv_b, a_log, d_skip, norm_w, tril2, e2, shift)


def _attn_kernel(planes, group, bounded_ref, q_ref, k_ref, v_ref, kp_ref, vp_ref, o_ref, lse_ref):
    blk = ATT_BLK
    sub = blk // planes
    first = pl.program_id(2) == 0

    def pos(i):
        return i if planes == 1 else (i % sub) * planes + i // sub

    row = lax.broadcasted_iota(jnp.int32, (blk, 2 * blk), 0)
    key = lax.broadcasted_iota(jnp.int32, (blk, 2 * blk), 1)
    rel = jnp.where(key >= blk, blk + pos(key - blk), pos(key)) - pos(row)
    band = (rel >= 0) & (rel <= blk)
    lo_key = jnp.where(first, blk, 0)
    bias_head = jnp.where(band & (key >= lo_key), 0.0, -jnp.inf)
    bias_head = jnp.concatenate([bias_head, bias_head], axis=0)
    bias_rest = jnp.where(band, 0.0, -jnp.inf)
    bias_rest = jnp.concatenate([bias_rest, bias_rest], axis=0)
    lane = lax.broadcasted_iota(jnp.int32, (blk, LANES), 1)
    lo_half = lane < HEAD_DIM

    n_res = q_ref.shape[1] if planes == 1 else 1

    def load(ref, r, g, ps):
        if planes == 1:
            return ref[0, r, g * blk:(g + 1) * blk, ps]
        return ref[0, :, 0, g * sub:(g + 1) * sub, ps].reshape(blk, LANES)

    def store(ref, r, g, ps, val):
        if planes == 1:
            ref[0, r, g * blk:(g + 1) * blk, ps] = val
        else:
            ref[0, :, 0, g * sub:(g + 1) * sub, ps] = val.reshape(planes, sub, val.shape[-1])

    def block(bounded, r, g):
        store(lse_ref, r, g, slice(0, LANES), jnp.zeros((blk, LANES), F32))
        for p in range(N_HEADS // 2):
            ps = slice(p * LANES, (p + 1) * LANES)
            qp = load(q_ref, r, g, ps)
            zero = jnp.zeros_like(qp)
            qs = jnp.concatenate(
                [jnp.where(lo_half, qp, zero), jnp.where(lo_half, zero, qp)], axis=0)
            k_prev = load(kp_ref, r, 0, ps) if g == 0 else load(k_ref, r, g - 1, ps)
            v_prev = load(vp_ref, r, 0, ps) if g == 0 else load(v_ref, r, g - 1, ps)
            k2 = jnp.concatenate([k_prev, load(k_ref, r, g, ps)], axis=0)
            v2 = jnp.concatenate([v_prev, load(v_ref, r, g, ps)], axis=0)
            s = lax.dot_general(qs, k2, NT_DIMS, preferred_element_type=F32)
            s = s + (bias_head if g == 0 else bias_rest)
            if bounded:
                e = jnp.exp2(s)
            else:
                m = jnp.max(s, axis=-1, keepdims=True)
                e = jnp.exp2(s - m)
            l = jnp.sum(e, axis=-1, keepdims=True)
            pv = _dot(e.astype(BF16), v2)
            store(o_ref, r, g, ps, jnp.where(lo_half, pv[0:blk], pv[blk:]).astype(BF16))
            store(lse_ref, r, g, slice(2 * p, 2 * p + 1), l[0:blk])
            store(lse_ref, r, g, slice(2 * p + 1, 2 * p + 2), l[blk:])
            if not bounded:
                c0 = N_HEADS + 2 * p
                store(lse_ref, r, g, slice(c0, c0 + 1), m[0:blk])
                store(lse_ref, r, g, slice(c0 + 1, c0 + 2), m[blk:])

    def body(bounded):
        for r in range(n_res):
            for g in range(group):
                block(bounded, r, g)

    bounded = bounded_ref[0] != 0

    @pl.when(bounded)
    def _():
        body(True)

    @pl.when(jnp.logical_not(bounded))
    def _():
        body(False)


def _dilated_attention(bounded, q, k, v, dilation):
    bsz, n_planes, rows, w = q.shape
    planes = n_planes // dilation
    assert planes in (1, 4)
    sub = ATT_BLK // planes
    nb = rows // sub
    group = min(ATT_GROUP, nb)
    n_res = 1
    if planes == 1:
        n_res = min(ATT_GROUP // group, dilation)
        view = lambda t: t
        blk_shape = lambda n, width: (1, n_res, n * ATT_BLK, width)
        cur = lambda b, res, j: (b, res, j, 0)
        prev = lambda b, res, j: (b, res, jnp.maximum(j * group - 1, 0), 0)
    else:
        view = lambda t: t.reshape(bsz, planes, dilation, rows, t.shape[-1])
        blk_shape = lambda n, width: (1, planes, 1, n * sub, width)
        cur = lambda b, res, j: (b, 0, res, j, 0)
        prev = lambda b, res, j: (b, 0, res, jnp.maximum(j * group - 1, 0), 0)
    tile = lambda width=w: pl.BlockSpec(blk_shape(group, width), cur)
    single = pl.BlockSpec(blk_shape(1, w), prev)
    lse_shape = (bsz, n_planes, rows, LANES)
    o, lse = pl.pallas_call(
        functools.partial(_attn_kernel, planes, group),
        grid=(bsz, dilation // n_res, nb // group),
        in_specs=[pl.BlockSpec(memory_space=pltpu.SMEM), tile(), tile(), tile(), single, single],
        out_specs=[tile(), tile(LANES)],
        out_shape=[jax.ShapeDtypeStruct(_view_shape(q.shape, planes, dilation), BF16),
                   jax.ShapeDtypeStruct(_view_shape(lse_shape, planes, dilation), F32)],
        compiler_params=pltpu.CompilerParams(
            dimension_semantics=("parallel", "parallel", "arbitrary"),
            vmem_limit_bytes=VMEM_LIMIT),
        name=f"dilated_attn_r{dilation}",
    )(bounded, view(q), view(k), view(v), view(k), view(v))
    return o.reshape(q.shape), lse.reshape(lse_shape)


def _view_shape(shape, planes, dilation):
    bsz, _, rows, width = shape
    return shape if planes == 1 else (bsz, planes, dilation, rows, width)


def _outproj_kernel(x_ref, mod_ref, ys_ref, o1_ref, o2_ref, o3_ref, l1_ref, l2_ref, l3_ref,
                    nw_ref, e2_ref, w_ref, out_ref, wide_ref, wide2_ref, narrow_ref, narrow2_ref,
                    acc_ref):
    cls_rows = x_ref.shape[1] // CLS_STRIDE
    acc_ref[...] = _dot(ys_ref[0], w_ref[0:D_INNER, :])

    def natural(cls_ref, stage_ref, stage2_ref):
        tiles = stage_ref.shape[0]
        quarter = cls_rows * 4
        for res in range(CLS_STRIDE):
            rho, a = res % 4, res // 4
            val = cls_ref[0, res].astype(F32)
            for t in range(tiles):
                stage2_ref[t, pl.ds(rho * quarter + a, cls_rows, stride=4), :] = (
                    val[:, t * LANES:(t + 1) * LANES])
        for t in range(tiles):
            for rho in range(4):
                stage_ref[t, pl.ds(rho, quarter, stride=4), :] = (
                    stage2_ref[t, rho * quarter:(rho + 1) * quarter, :])
        return jnp.concatenate([stage_ref[t] for t in range(tiles)], axis=1)

    s1 = l1_ref[0]
    s2 = natural(l2_ref, narrow_ref, narrow2_ref)
    s3 = natural(l3_ref, narrow_ref, narrow2_ref)
    shifts = [pltpu.roll(s, LANES - N_HEADS, axis=1) for s in (s1, s2, s3)]
    mx = jnp.maximum(jnp.maximum(shifts[0], shifts[1]), shifts[2])
    w1, w2, w3 = [jnp.exp2(m - mx) for m in shifts]
    inv = 1.0 / (w1 * s1 + w2 * s2 + w3 * s3)
    head_lane = lax.broadcasted_iota(jnp.int32, s1.shape, 1) < N_HEADS

    def expand(v):
        hi, lo = _split_bf16(jnp.where(head_lane, v, 0.0))
        return _dot(jnp.concatenate([hi, lo], axis=1), e2_ref[...])

    o = expand(w1 * inv) * o1_ref[0].astype(F32)
    o = o + expand(w2 * inv) * natural(o2_ref, wide_ref, wide2_ref)
    o = o + expand(w3 * inv) * natural(o3_ref, wide_ref, wide2_ref)
    ms = jnp.mean(o * o, axis=-1, keepdims=True)
    y_att = (o * lax.rsqrt(ms + EPS) * nw_ref[...]).astype(BF16)
    mix = acc_ref[...] + _dot(y_att, w_ref[D_INNER:, :])
    out_ref[0] = x_ref[0] + mod_ref[0, 2:3, :] * mix


def _out_projection(x, mod3, y_ssd, outs, lses, norm_w, e2, w_out, tm):
    bsz, s, d = x.shape
    row = lambda b, i: (b, i, 0)
    wide = pl.BlockSpec((1, tm, D_INNER), row)
    narrow = pl.BlockSpec((1, tm, LANES), row)
    cls = lambda width: pl.BlockSpec((1, CLS_STRIDE, tm // CLS_STRIDE, width),
                                     lambda b, i: (b, 0, i, 0))
    return pl.pallas_call(
        _outproj_kernel,
        grid=(bsz, s // tm),
        in_specs=[
            pl.BlockSpec((1, tm, d), row),
            pl.BlockSpec((1, N_MOD, d), lambda b, i: (b, 0, 0)),
            wide, wide, cls(D_INNER), cls(D_INNER), narrow, cls(LANES), cls(LANES),
            _const_spec((1, D_INNER)),
            _const_spec((2 * LANES, D_INNER)),
            _const_spec((2 * D_INNER, d)),
        ],
        out_specs=pl.BlockSpec((1, tm, d), row),
        out_shape=jax.ShapeDtypeStruct((bsz, s, d), F32),
        scratch_shapes=[pltpu.VMEM((D_INNER // LANES, tm, LANES), F32),
                        pltpu.VMEM((D_INNER // LANES, tm, LANES), F32),
                        pltpu.VMEM((1, tm, LANES), F32),
                        pltpu.VMEM((1, tm, LANES), F32),
                        pltpu.VMEM((tm, d), F32)],
        compiler_params=pltpu.CompilerParams(
            dimension_semantics=("parallel", "arbitrary"), vmem_limit_bytes=VMEM_LIMIT),
        name="out_proj",
    )(x, mod3, y_ssd, *outs, *lses, norm_w, e2, w_out)


FF_TN = 1024


def _mlp_kernel(x_ref, mod_ref, nw_ref, w1_ref, w2_ref, out_ref, h_ref):
    x = x_ref[0]
    h = _norm_mod(x, nw_ref[...], mod_ref[0, 4:5, :], mod_ref[0, 3:4, :])
    h_ref[...] = h.astype(BF16)
    acc = jnp.zeros(x.shape, F32)
    for j in range(D_FF // FF_TN):
        fs = slice(j * FF_TN, (j + 1) * FF_TN)
        u = jnp.maximum(_dot(h_ref[...], w1_ref[:, fs]), 0.0)
        acc = acc + _dot((u * u).astype(BF16), w2_ref[fs, :])
    out_ref[0] = x + mod_ref[0, 5:6, :] * acc


def _mlp(x, mod3, norm_w, w1, w2, tm):
    bsz, s, d = x.shape
    row = lambda b, i: (b, i, 0)
    return pl.pallas_call(
        _mlp_kernel,
        grid=(bsz, s // tm),
        in_specs=[
            pl.BlockSpec((1, tm, d), row),
            pl.BlockSpec((1, N_MOD, d), lambda b, i: (b, 0, 0)),
            _const_spec((1, d)),
            _const_spec((d, D_FF)),
            _const_spec((D_FF, d)),
        ],
        out_specs=pl.BlockSpec((1, tm, d), row),
        out_shape=jax.ShapeDtypeStruct((bsz, s, d), F32),
        scratch_shapes=[pltpu.VMEM((tm, d), BF16)],
        compiler_params=pltpu.CompilerParams(
            dimension_semantics=("parallel", "arbitrary"), vmem_limit_bytes=VMEM_LIMIT),
        name="mlp",
    )(x, mod3, norm_w, w1, w2)


def _head_expand_matrix():
    head_of_lane = jnp.arange(D_INNER) // HEAD_DIM
    e = (jnp.arange(LANES)[:, None] == head_of_lane[None, :]).astype(BF16)
    return jnp.concatenate([e, e], axis=0)


def _pad_lanes(v, n=LANES):
    return jnp.pad(v.astype(F32), (0, n - v.shape[0])).reshape(1, n)


def kernel(x, c, norm1_w, norm2_w, w_ada, b_ada, w_in, conv_w, conv_b, dt_bias, a_log, d_skip,
           ssd_norm_w, q_norm_w, k_norm_w, attn_norm_w, w_out, w_ff1, w_ff2):
    bsz, s, d = x.shape
    depth = w_ada.shape[0]
    tm = 512
    e2 = _head_expand_matrix()
    idx = jnp.arange(256) // HEAD_DIM
    bd = (idx[:, None] == idx[None, :]).astype(BF16)
    t = (jnp.arange(SSD_CHUNK)[:, None] >= jnp.arange(SSD_CHUNK)[None, :]).astype(BF16)
    tril2 = jnp.concatenate([t, t], axis=1)
    o_xbc = D_INNER + CONV_CH
    o_dt = o_xbc + N_HEADS

    for l in range(depth):
        mod3 = _modulation(c, w_ada[l], b_ada[l]).reshape(bsz, N_MOD, d)
        wl = w_in[l]
        w_dt = jnp.pad(wl[:, o_xbc:o_dt], ((0, 0), (0, LANES - N_HEADS))).astype(BF16)
        z, xbc, dt, q, k, v, q16, k16, v16 = _in_projection(
            x, mod3, norm1_w[l].reshape(1, d), wl[:, :o_xbc].astype(BF16),
            wl[:, o_dt:].astype(BF16), w_dt, _pad_lanes(dt_bias[l]),
            jnp.tile(q_norm_w[l], N_HEADS).reshape(1, D_INNER),
            jnp.tile(k_norm_w[l], N_HEADS).reshape(1, D_INNER), bd, tm)
        y_ssd = _ssd_mixer(
            xbc, z, dt, conv_w[l], conv_b[l].reshape(1, CONV_CH),
            _pad_lanes(a_log[l]), jnp.repeat(d_skip[l], HEAD_DIM).reshape(1, D_INNER),
            ssd_norm_w[l].reshape(1, D_INNER), tril2, e2)
        score_bound = (HEAD_DIM ** 0.5 * LOG2E * jnp.max(jnp.abs(q_norm_w[l]))
                       * jnp.max(jnp.abs(k_norm_w[l])))
        bounded = (score_bound * BF16_SLACK <= SCORE_BOUND).astype(jnp.int32).reshape(1)
        outs, lses = [], []
        for window, dilation in DILATED_PATTERNS:
            assert window // dilation == ATT_BLK
            if dilation == 1:
                nat = lambda t: t.reshape(bsz, 1, s, t.shape[-1])
                o, lse = _dilated_attention(bounded, nat(q), nat(k), nat(v), 1)
                o, lse = o.reshape(bsz, s, D_INNER), lse.reshape(bsz, s, LANES)
            else:
                o, lse = _dilated_attention(bounded, q16, k16, v16, dilation)
            outs.append(o)
            lses.append(lse)
        x = _out_projection(x, mod3, y_ssd, outs, lses, attn_norm_w[l].reshape(1, D_INNER),
                            e2, w_out[l].astype(BF16), tm)
        x = _mlp(x, mod3, norm2_w[l].reshape(1, d), w_ff1[l].astype(BF16),
                 w_ff2[l].astype(BF16), tm)
    return x.astype(c.dtype)
```

```python
import functools

import jax
import jax.numpy as jnp
from jax import lax
from jax.experimental import pallas as pl
from jax.experimental.pallas import tpu as pltpu

F32 = jnp.float32
BF16 = jnp.bfloat16

D_MODEL = 1024
HEAD_DIM = 64
N_HEADS = 16
SSD_GROUPS = 4
SSD_STATE = 128
SSD_CONV = 4
SSD_CHUNK = 128
D_INNER = N_HEADS * HEAD_DIM
CONV_CH = D_INNER + 2 * SSD_GROUPS * SSD_STATE
D_FF = 4 * D_MODEL
N_MOD = 6
EPS = 1e-6
DILATED_PATTERNS = ((128, 1), (512, 4), (2048, 16))
ATT_BLK = 128
ATT_GROUP = 8
CLS_STRIDE = 16
LANES = 128
VMEM_LIMIT = 56 * 1024 * 1024

LOG2E = 1.4426950408889634
LN2 = 0.6931471805599453
SCORE_BOUND = 60.0
BF16_SLACK = 1.02

NT_DIMS = (((1,), (1,)), ((), ()))
TN_DIMS = (((0,), (0,)), ((), ()))


def _dot(a, b):
    return jnp.dot(a, b, preferred_element_type=F32)


def _split_bf16(v):
    hi = v.astype(BF16)
    lo = (v - hi.astype(F32)).astype(BF16)
    return hi, lo


def _silu(v):
    h = 0.5 * v
    return h + h * jnp.tanh(h)


def _const_spec(shape):
    nd = len(shape)
    return pl.BlockSpec(shape, lambda *_: (0,) * nd, pipeline_mode=pl.Buffered(1))


def _mod_kernel(c_ref, w_ref, b_ref, o_ref):
    c = c_ref[...]
    ca = _silu(c)
    c_hi, c_lo = _split_bf16(ca)
    w = w_ref[...]
    w_hi, w_lo = _split_bf16(w)
    acc = _dot(c_hi, w_hi) + _dot(c_lo, w_hi) + _dot(c_hi, w_lo)
    o_ref[...] = acc + b_ref[...]


def _modulation(c, w_ada, b_ada):
    bsz, d = c.shape
    n = w_ada.shape[1]
    tn = 1536
    return pl.pallas_call(
        _mod_kernel,
        grid=(n // tn,),
        in_specs=[
            pl.BlockSpec((bsz, d), lambda j: (0, 0)),
            pl.BlockSpec((d, tn), lambda j: (0, j)),
            pl.BlockSpec((1, tn), lambda j: (0, j)),
        ],
        out_specs=pl.BlockSpec((bsz, tn), lambda j: (0, j)),
        out_shape=jax.ShapeDtypeStruct((bsz, n), F32),
        compiler_params=pltpu.CompilerParams(
            dimension_semantics=("arbitrary",), vmem_limit_bytes=VMEM_LIMIT),
        name="adaln_mod",
    )(c, w_ada, b_ada.reshape(1, n))


IN_TN = 512


def _norm_mod(x, nw, scale, shift):
    ms = jnp.mean(x * x, axis=-1, keepdims=True)
    return (x * lax.rsqrt(ms + EPS) * nw) * (1.0 + scale) + shift


def _inproj_kernel(x_ref, mod_ref, nw_ref, wzx_ref, wqkv_ref, wdt_ref, dtb_ref, qw_ref, kw_ref,
                   bd_ref, z_ref, xbc_ref, dt_ref, q16_ref, k16_ref, v16_ref,
                   h_ref, stage_ref, stage2_ref):
    x = x_ref[0]
    tm = x.shape[0]
    h = _norm_mod(x, nw_ref[...], mod_ref[0, 1:2, :], mod_ref[0, 0:1, :])
    h_ref[...] = h.astype(BF16)

    def proj(w_ref, c0, width):
        return _dot(h_ref[...], w_ref[:, c0:c0 + width])

    def qk_norm(acc, w):
        sq = (acc * acc).astype(BF16)
        parts = [_dot(sq[:, i:i + 256], bd_ref[...]) for i in range(0, IN_TN, 256)]
        ss = jnp.concatenate(parts, axis=1)
        return acc * lax.rsqrt(ss * (1.0 / HEAD_DIM) + EPS) * w

    cls_rows = x.shape[0] // CLS_STRIDE

    def emit(cls_ref, sl, val):
        tiles = IN_TN // LANES
        quarter = tm // 4
        for t in range(tiles):
            stage_ref[t] = val[:, t * LANES:(t + 1) * LANES]
        for t in range(tiles):
            for rho in range(4):
                stage2_ref[t, rho * quarter:(rho + 1) * quarter, :] = (
                    stage_ref[t, pl.ds(rho, quarter, stride=4), :])
        for res in range(CLS_STRIDE):
            rho, a = res % 4, res // 4
            rows = [stage2_ref[t, pl.ds(rho * quarter + a, cls_rows, stride=4), :]
                    for t in range(tiles)]
            cls_ref[0, res, :, sl] = jnp.concatenate(rows, axis=1).astype(BF16)

    col = 0
    for j in range(D_INNER // IN_TN):
        z_ref[0, :, j * IN_TN:(j + 1) * IN_TN] = proj(wzx_ref, col, IN_TN).astype(BF16)
        col += IN_TN
    for j in range(CONV_CH // IN_TN):
        xbc_ref[0, :, j * IN_TN:(j + 1) * IN_TN] = proj(wzx_ref, col, IN_TN).astype(BF16)
        col += IN_TN
    col = 0
    for j in range(D_INNER // IN_TN):
        sl = slice(j * IN_TN, (j + 1) * IN_TN)
        emit(q16_ref, sl,
             qk_norm(proj(wqkv_ref, col, IN_TN), qw_ref[:, sl]) * (HEAD_DIM ** -0.5 * LOG2E))
        col += IN_TN
    for j in range(D_INNER // IN_TN):
        sl = slice(j * IN_TN, (j + 1) * IN_TN)
        emit(k16_ref, sl, qk_norm(proj(wqkv_ref, col, IN_TN), kw_ref[:, sl]))
        col += IN_TN
    for j in range(D_INNER // IN_TN):
        emit(v16_ref, slice(j * IN_TN, (j + 1) * IN_TN), proj(wqkv_ref, col, IN_TN))
        col += IN_TN
    dt_raw = proj(wdt_ref, 0, LANES) + dtb_ref[...]
    dt_ref[0] = jnp.maximum(dt_raw, 0.0) + jnp.log(1.0 + jnp.exp(-jnp.abs(dt_raw)))


def _in_projection(x, mod3, norm_w, w_zx, w_qkv, w_dt, dt_bias, qw, kw, bd, tm):
    bsz, s, d = x.shape
    row = lambda b, i: (b, i, 0)
    out_bf = lambda n: jax.ShapeDtypeStruct((bsz, s, n), BF16)
    cls_spec = pl.BlockSpec((1, CLS_STRIDE, tm // CLS_STRIDE, D_INNER), lambda b, i: (b, 0, i, 0))
    cls_shape = jax.ShapeDtypeStruct((bsz, CLS_STRIDE, s // CLS_STRIDE, D_INNER), BF16)
    return pl.pallas_call(
        _inproj_kernel,
        grid=(bsz, s // tm),
        in_specs=[
            pl.BlockSpec((1, tm, d), row),
            pl.BlockSpec((1, N_MOD, d), lambda b, i: (b, 0, 0)),
            _const_spec((1, d)),
            _const_spec(w_zx.shape),
            _const_spec(w_qkv.shape),
            _const_spec(w_dt.shape),
            _const_spec((1, LANES)),
            _const_spec((1, D_INNER)),
            _const_spec((1, D_INNER)),
            _const_spec((256, 256)),
        ],
        out_specs=[
            pl.BlockSpec((1, tm, D_INNER), row),
            pl.BlockSpec((1, tm, CONV_CH), row),
            pl.BlockSpec((1, tm, LANES), row),
            cls_spec, cls_spec, cls_spec,
        ],
        out_shape=[out_bf(D_INNER), out_bf(CONV_CH),
                   jax.ShapeDtypeStruct((bsz, s, LANES), F32),
                   cls_shape, cls_shape, cls_shape],
        scratch_shapes=[pltpu.VMEM((tm, d), BF16),
                        pltpu.VMEM((IN_TN // LANES, tm, LANES), F32),
                        pltpu.VMEM((IN_TN // LANES, tm, LANES), F32)],
        compiler_params=pltpu.CompilerParams(
            dimension_semantics=("parallel", "arbitrary"), vmem_limit_bytes=VMEM_LIMIT),
        name="in_proj",
    )(x, mod3, norm_w, w_zx, w_qkv, w_dt, dt_bias, qw, kw, bd)


SSD_ROWS = 512
CONV_HALO = 8


def _ssd_kernel(xbc_ref, z_ref, dt_ref, cw_ref, cb_ref, alog_ref, dskip_ref,
                nw_ref, tril2_ref, e2_ref, shift_ref, o_ref, halo_ref, xc_ref, state_ref):
    rows = xbc_ref.shape[1]
    L = SSD_CHUNK
    taps = SSD_CONV - 1

    @pl.when(pl.program_id(1) == 0)
    def _():
        halo_ref[...] = jnp.zeros_like(halo_ref)
        state_ref[...] = jnp.zeros_like(state_ref)

    hsub = lax.broadcasted_iota(jnp.int32, (CONV_HALO, 256), 0)

    def conv_chunk(c):
        r0 = c * L
        for c0 in range(0, CONV_CH, 256):
            cs = slice(c0, c0 + 256)
            u = xbc_ref[0, r0:r0 + L, cs]
            shifted = _dot(shift_ref[...], u)
            uf = u.astype(F32)
            acc = cb_ref[:, cs] + cw_ref[taps:taps + 1, cs] * uf
            for k in range(1, taps + 1):
                acc = acc + cw_ref[taps - k:taps - k + 1, cs] * shifted[(k - 1) * L:k * L]
            xc_ref[r0:r0 + L, cs] = _silu(acc)
            if c == 0:
                halo = halo_ref[:, cs]
            else:
                halo = xbc_ref[0, r0 - 2 * CONV_HALO:r0, cs].astype(F32)[CONV_HALO:]
            head = acc[0:CONV_HALO]
            for k in range(1, taps + 1):
                prev = jnp.where(hsub < k, pltpu.roll(halo, k, axis=0), 0.0)
                head = head + cw_ref[taps - k:taps - k + 1, cs] * prev
            xc_ref[r0:r0 + CONV_HALO, cs] = _silu(head)
            if r0 + L == rows:
                halo_ref[:, cs] = uf[L - CONV_HALO:L]

    lane = lax.broadcasted_iota(jnp.int32, (L, LANES), 1)
    sub = lax.broadcasted_iota(jnp.int32, (L, LANES), 0)
    tril = sub >= lane
    lo_half = lane < HEAD_DIM
    a_neg = jnp.where(lane[0:1] < N_HEADS, -jnp.exp(alog_ref[...]), 0.0)

    def split_cat(v):
        hi, lo = _split_bf16(v)
        return jnp.concatenate([hi, lo], axis=1)

    def expand(v_cat, gs):
        return _dot(v_cat, e2_ref[:, gs])

    conv_chunk(0)
    for c in range(rows // L):
        if c + 1 < rows // L:
            conv_chunk(c + 1)
        r0 = c * L
        rs = slice(r0, r0 + L)
        dt = dt_ref[0, rs, :]
        d_a = dt * a_neg
        da_hi, da_lo = _split_bf16(d_a)
        a_cs = _dot(tril2_ref[...], jnp.concatenate([da_hi, da_lo], axis=0))
        a_cs_t = a_cs.T
        a_last = a_cs[L - 1:L, :]
        dt_cat = split_cat(dt)
        w_cat = split_cat(dt * jnp.exp(a_last - a_cs))
        do_cat = split_cat(jnp.exp(a_cs))

        for g in range(SSD_GROUPS):
            gs = slice(g * 256, (g + 1) * 256)
            xs = xc_ref[rs, gs]
            do_x = expand(do_cat, gs)
            xdt = (xs * expand(dt_cat, gs)).astype(BF16)
            xw = (xs * expand(w_cat, gs)).astype(BF16)
            b0 = D_INNER + g * SSD_STATE
            c0 = D_INNER + SSD_GROUPS * SSD_STATE + g * SSD_STATE
            bg = xc_ref[rs, b0:b0 + SSD_STATE].astype(BF16)
            cg = xc_ref[rs, c0:c0 + SSD_STATE].astype(BF16)
            cb = lax.dot_general(cg, bg, NT_DIMS, preferred_element_type=F32)
            yd_parts = []
            for pair in range(2):
                ms = []
                for e in range(2):
                    hd = g * 4 + pair * 2 + e
                    colb = jnp.broadcast_to(a_cs[:, hd:hd + 1], (L, L))
                    rowb = jnp.broadcast_to(a_cs_t[hd:hd + 1, :], (L, L))
                    lmat = jnp.exp(jnp.where(tril, colb - rowb, -jnp.inf))
                    ms.append((cb * lmat).astype(BF16))
                mcat = jnp.concatenate(ms, axis=1)
                xp = xdt[:, pair * LANES:(pair + 1) * LANES]
                zero = jnp.zeros_like(xp)
                xstack = jnp.concatenate(
                    [jnp.where(lo_half, xp, zero), jnp.where(lo_half, zero, xp)], axis=0)
                yd_parts.append(_dot(mcat, xstack))
            y_diag = jnp.concatenate(yd_parts, axis=1)
            st = state_ref[:, gs]
            y_off = _dot(cg, st.astype(BF16)) * do_x
            upd = lax.dot_general(bg, xw, TN_DIMS, preferred_element_type=F32)
            state_ref[:, gs] = st * do_x[L - 1:L] + upd
            gate = _silu(z_ref[0, rs, gs].astype(F32))
            yg = (dskip_ref[:, gs] * xs + y_diag + y_off) * gate
            ss = jnp.mean(yg * yg, axis=-1, keepdims=True)
            o_ref[0, rs, gs] = (yg * lax.rsqrt(ss + EPS) * nw_ref[:, gs]).astype(BF16)


def _ssd_mixer(xbc, z, dt, conv_w, conv_b, a_log, d_skip, norm_w, tril2, e2):
    bsz, s, _ = xbc.shape
    rows = SSD_ROWS
    row = lambda b, i: (b, i, 0)
    t = jnp.arange(SSD_CHUNK)
    shift = jnp.concatenate(
        [(t[:, None] - k == t[None, :]).astype(BF16) for k in range(1, SSD_CONV)], axis=0)
    return pl.pallas_call(
        _ssd_kernel,
        grid=(bsz, s // rows),
        in_specs=[
            pl.BlockSpec((1, rows, CONV_CH), row),
            pl.BlockSpec((1, rows, D_INNER), row),
            pl.BlockSpec((1, rows, LANES), row),
            _const_spec((SSD_CONV, CONV_CH)),
            _const_spec((1, CONV_CH)),
            _const_spec((1, LANES)),
            _const_spec((1, D_INNER)),
            _const_spec((1, D_INNER)),
            _const_spec((SSD_CHUNK, 2 * SSD_CHUNK)),
            _const_spec((2 * LANES, D_INNER)),
            _const_spec(((SSD_CONV - 1) * SSD_CHUNK, SSD_CHUNK)),
        ],
        out_specs=pl.BlockSpec((1, rows, D_INNER), row),
        out_shape=jax.ShapeDtypeStruct((bsz, s, D_INNER), BF16),
        scratch_shapes=[
            pltpu.VMEM((CONV_HALO, CONV_CH), F32),
            pltpu.VMEM((rows, CONV_CH), F32),
            pltpu.VMEM((SSD_STATE, D_INNER), F32),
        ],
        compiler_params=pltpu.CompilerParams(
            dimension_semantics=("parallel", "arbitrary"), vmem_limit_bytes=VMEM_LIMIT),
        name="ssd_mixer",
    )(xbc, z, dt, conv_w, conv_b, a_log, d_skip, norm_w, tril2, e2, shift)


def _attn_kernel(planes, group, chain, bounded_ref, q_ref, k_ref, v_ref, kp_ref, vp_ref, *refs):
    if chain:
        acc_in_ref, st_in_ref, e2_ref, o_ref, lse_ref = refs
        assert planes in (1, 4)
    else:
        o_ref, lse_ref = refs
    blk = ATT_BLK
    sub = blk // planes
    first = pl.program_id(2) == 0

    def pos(i):
        return i if planes == 1 else (i % sub) * planes + i // sub

    row = lax.broadcasted_iota(jnp.int32, (blk, 2 * blk), 0)
    key = lax.broadcasted_iota(jnp.int32, (blk, 2 * blk), 1)
    rel = jnp.where(key >= blk, blk + pos(key - blk), pos(key)) - pos(row)
    band = (rel >= 0) & (rel <= blk)
    lo_key = jnp.where(first, blk, 0)
    bias_head = jnp.where(band & (key >= lo_key), 0.0, -jnp.inf)
    bias_head = jnp.concatenate([bias_head, bias_head], axis=0)
    bias_rest = jnp.where(band, 0.0, -jnp.inf)
    bias_rest = jnp.concatenate([bias_rest, bias_rest], axis=0)
    lane = lax.broadcasted_iota(jnp.int32, (blk, LANES), 1)
    lo_half = lane < HEAD_DIM

    n_res = q_ref.shape[1] if planes == 1 else 1

    half = 2 * sub

    def load(ref, r, g, ps):
        if planes == 1:
            return ref[0, r, g * blk:(g + 1) * blk, ps]
        if planes == 4:
            return ref[0, :, 0, g * sub:(g + 1) * sub, ps].reshape(blk, ps.stop - ps.start)
        rows = ref[0, :, (g // 2) * half:(g // 2 + 1) * half, ps].astype(F32)
        return rows[:, (g % 2) * sub:(g % 2 + 1) * sub, :].reshape(blk, LANES).astype(BF16)

    def store_cols(ref, r, g, ps, val):
        if planes == 1:
            ref[0, r, g * blk:(g + 1) * blk, ps] = val
        elif planes == 4:
            ref[0, :, 0, g * sub:(g + 1) * sub, ps] = val.reshape(planes, sub, val.shape[-1])
        else:
            ref[0, :, g * sub:(g + 1) * sub, ps] = val.reshape(planes, sub, val.shape[-1])

    pending = {}

    def store_out(r, g, ps, val):
        if planes == 1:
            o_ref[0, r, g * blk:(g + 1) * blk, ps] = val.astype(BF16)
        elif planes == 4:
            o_ref[0, :, 0, g * sub:(g + 1) * sub, ps] = (
                val.reshape(planes, sub, val.shape[-1]).astype(BF16))
        elif g % 2 == 0:
            pending[ps.start] = val.reshape(planes, sub, LANES)
        else:
            both = jnp.concatenate(
                [pending.pop(ps.start), val.reshape(planes, sub, LANES)], axis=1)
            o_ref[0, :, (g // 2) * half:(g // 2 + 1) * half, ps] = both.astype(BF16)

    def pair(bounded, r, g, p):
        ps = slice(p * LANES, (p + 1) * LANES)
        qp = load(q_ref, r, g, ps)
        zero = jnp.zeros_like(qp)
        qs = jnp.concatenate(
            [jnp.where(lo_half, qp, zero), jnp.where(lo_half, zero, qp)], axis=0)
        if g == 0:
            last = kp_ref.shape[-2] // sub - 1
            k_prev, v_prev = load(kp_ref, r, last, ps), load(vp_ref, r, last, ps)
        else:
            k_prev, v_prev = load(k_ref, r, g - 1, ps), load(v_ref, r, g - 1, ps)
        k2 = jnp.concatenate([k_prev, load(k_ref, r, g, ps)], axis=0)
        v2 = jnp.concatenate([v_prev, load(v_ref, r, g, ps)], axis=0)
        s = lax.dot_general(qs, k2, NT_DIMS, preferred_element_type=F32)
        s = s + (bias_head if g == 0 else bias_rest)
        if bounded:
            e = jnp.exp2(s)
        else:
            m = jnp.max(s, axis=-1, keepdims=True)
            e = jnp.exp2(s - m)
        l = jnp.sum(e, axis=-1, keepdims=True)
        pv = _dot(e.astype(BF16), v2)
        out = jnp.where(lo_half, pv[0:blk], pv[blk:])
        if chain and bounded:
            out = out + load(acc_in_ref, r, g, ps).astype(F32)
        store_out(r, g, ps, out)
        store_cols(lse_ref, r, g, slice(2 * p, 2 * p + 1), l[0:blk])
        store_cols(lse_ref, r, g, slice(2 * p + 1, 2 * p + 2), l[blk:])
        if not bounded:
            c0 = N_HEADS + 2 * p
            store_cols(lse_ref, r, g, slice(c0, c0 + 1), m[0:blk])
            store_cols(lse_ref, r, g, slice(c0 + 1, c0 + 2), m[blk:])

    def merge_block(r, g):
        everything = slice(0, LANES)
        head_lane = lane < N_HEADS
        s_own, s_prev = load(lse_ref, r, g, everything), load(st_in_ref, r, g, everything)
        m_own = pltpu.roll(s_own, LANES - N_HEADS, axis=1)
        m_prev = pltpu.roll(s_prev, LANES - N_HEADS, axis=1)
        m_new = jnp.maximum(m_prev, m_own)
        a = jnp.where(head_lane, jnp.exp2(m_prev - m_new), 0.0)
        b = jnp.where(head_lane, jnp.exp2(m_own - m_new), 0.0)
        stats = (jnp.where(head_lane, a * s_prev + b * s_own, 0.0)
                 + pltpu.roll(jnp.where(head_lane, m_new, 0.0), N_HEADS, axis=1))
        store_cols(lse_ref, r, g, everything, stats)

        def expand(v):
            hi, lo = _split_bf16(v)
            return _dot(jnp.concatenate([hi, lo], axis=1), e2_ref[...])

        wide = slice(0, o_ref.shape[-1])
        merged = (expand(a) * load(acc_in_ref, r, g, wide).astype(F32)
                  + expand(b) * load(o_ref, r, g, wide).astype(F32))
        store_out(r, g, wide, merged)

    def body(bounded):
        lse_ref[...] = jnp.zeros(lse_ref.shape, F32)
        for r in range(n_res):
            if planes == 16:
                order = [(g0 + h, p) for g0 in range(0, group, 2)
                         for p in range(N_HEADS // 2) for h in range(2)]
            else:
                order = [(g, p) for g in range(group) for p in range(N_HEADS // 2)]
            for g, p in order:
                pair(bounded, r, g, p)
                if chain and not bounded and p == N_HEADS // 2 - 1:
                    merge_block(r, g)
        if chain and bounded:
            lse_ref[...] = lse_ref[...] + st_in_ref[...]

    bounded = bounded_ref[0] != 0

    @pl.when(bounded)
    def _():
        body(True)

    @pl.when(jnp.logical_not(bounded))
    def _():
        body(False)


def _dilated_attention(bounded, q, k, v, dilation, e2, running=None):
    bsz, n_planes, rows, w = q.shape
    planes = n_planes // dilation
    assert planes in (1, 4, 16)
    sub = ATT_BLK // planes
    nb = rows // sub
    group = min(ATT_GROUP, nb)
    n_res = 1
    window = 1
    if planes == 1:
        n_res = min(ATT_GROUP // group, dilation)
        view = lambda t: t
        blk_shape = lambda n, width: (1, n_res, n * ATT_BLK, width)
        cur = lambda b, res, j: (b, res, j, 0)
        prev = lambda b, res, j: (b, res, jnp.maximum(j * group - 1, 0), 0)
    elif planes == 4:
        view = lambda t: t.reshape(bsz, planes, dilation, rows, t.shape[-1])
        blk_shape = lambda n, width: (1, planes, 1, n * sub, width)
        cur = lambda b, res, j: (b, 0, res, j, 0)
        prev = lambda b, res, j: (b, 0, res, jnp.maximum(j * group - 1, 0), 0)
    else:
        assert group % 2 == 0
        window = 2
        view = lambda t: t
        blk_shape = lambda n, width: (1, planes, n * sub, width)
        cur = lambda b, res, j: (b, 0, j, 0)
        prev = lambda b, res, j: (b, 0, jnp.maximum(j * (group // 2) - 1, 0), 0)
    tile = lambda width=w: pl.BlockSpec(blk_shape(group, width), cur)
    single = pl.BlockSpec(blk_shape(window, w), prev)
    lse_shape = (bsz, n_planes, rows, LANES)
    chain = running is not None
    extra_specs = [tile(), tile(LANES), _const_spec(e2.shape)] if chain else []
    extra_args = [view(running[0]), view(running[1]), e2] if chain else []
    o, lse = pl.pallas_call(
        functools.partial(_attn_kernel, planes, group, chain),
        grid=(bsz, dilation // n_res, nb // group),
        in_specs=[pl.BlockSpec(memory_space=pltpu.SMEM), tile(), tile(), tile(), single, single,
                  *extra_specs],
        out_specs=[tile(), tile(LANES)],
        out_shape=[jax.ShapeDtypeStruct(_view_shape(q.shape, planes, dilation), BF16),
                   jax.ShapeDtypeStruct(_view_shape(lse_shape, planes, dilation), F32)],
        compiler_params=pltpu.CompilerParams(
            dimension_semantics=("parallel", "parallel", "arbitrary"),
            vmem_limit_bytes=VMEM_LIMIT),
        name=f"dilated_attn_r{dilation}",
    )(bounded, view(q), view(k), view(v), view(k), view(v), *extra_args)
    return o.reshape(q.shape), lse.reshape(lse_shape)


def _view_shape(shape, planes, dilation):
    bsz, _, rows, width = shape
    return (bsz, planes, dilation, rows, width) if planes == 4 else shape


def _outproj_kernel(x_ref, mod_ref, ys_ref, o_ref, st_ref, nw_ref, e2_ref, w_ref, out_ref,
                    wide_ref, wide2_ref, acc_ref):
    tm = x_ref.shape[1]
    cls_rows = tm // CLS_STRIDE
    acc_ref[...] = _dot(ys_ref[0], w_ref[0:D_INNER, :])

    def regrouped(cls_ref):
        return cls_ref[0].reshape(tm, cls_ref.shape[-1])

    def natural(val):
        tiles = wide_ref.shape[0]
        quarter = cls_rows * 4
        for res in range(CLS_STRIDE):
            rho, a = res % 4, res // 4
            for t in range(tiles):
                wide2_ref[t, pl.ds(rho * quarter + a, cls_rows, stride=4), :] = (
                    val[res * cls_rows:(res + 1) * cls_rows, t * LANES:(t + 1) * LANES])
        for t in range(tiles):
            for rho in range(4):
                wide_ref[t, pl.ds(rho, quarter, stride=4), :] = (
                    wide2_ref[t, rho * quarter:(rho + 1) * quarter, :])
        return jnp.concatenate([wide_ref[t] for t in range(tiles)], axis=1)

    st = regrouped(st_ref)
    head_lane = lax.broadcasted_iota(jnp.int32, st.shape, 1) < N_HEADS
    hi, lo = _split_bf16(jnp.where(head_lane, 1.0 / st, 0.0))
    inv_x = _dot(jnp.concatenate([hi, lo], axis=1), e2_ref[...])
    o = inv_x * regrouped(o_ref).astype(F32)
    ms = jnp.mean(o * o, axis=-1, keepdims=True)
    y_att = (o * lax.rsqrt(ms + EPS) * nw_ref[...]).astype(BF16)
    mix = acc_ref[...] + natural(_dot(y_att, w_ref[D_INNER:, :]))
    out_ref[0] = x_ref[0] + mod_ref[0, 2:3, :] * mix


def _out_projection(x, mod3, y_ssd, att_acc, att_stats, norm_w, e2, w_out, tm):
    bsz, s, d = x.shape
    row = lambda b, i: (b, i, 0)
    wide = pl.BlockSpec((1, tm, D_INNER), row)
    cls = lambda width: pl.BlockSpec((1, CLS_STRIDE, tm // CLS_STRIDE, width),
                                     lambda b, i: (b, 0, i, 0))
    return pl.pallas_call(
        _outproj_kernel,
        grid=(bsz, s // tm),
        in_specs=[
            pl.BlockSpec((1, tm, d), row),
            pl.BlockSpec((1, N_MOD, d), lambda b, i: (b, 0, 0)),
            wide, cls(D_INNER), cls(LANES),
            _const_spec((1, D_INNER)),
            _const_spec((2 * LANES, D_INNER)),
            _const_spec((2 * D_INNER, d)),
        ],
        out_specs=pl.BlockSpec((1, tm, d), row),
        out_shape=jax.ShapeDtypeStruct((bsz, s, d), F32),
        scratch_shapes=[pltpu.VMEM((d // LANES, tm, LANES), F32),
                        pltpu.VMEM((d // LANES, tm, LANES), F32),
                        pltpu.VMEM((tm, d), F32)],
        compiler_params=pltpu.CompilerParams(
            dimension_semantics=("parallel", "arbitrary"), vmem_limit_bytes=VMEM_LIMIT),
        name="out_proj",
    )(x, mod3, y_ssd, att_acc, att_stats, norm_w, e2, w_out)


FF_TN = 1024


def _mlp_kernel(x_ref, mod_ref, nw_ref, w1_ref, w2_ref, out_ref, h_ref):
    x = x_ref[0]
    h = _norm_mod(x, nw_ref[...], mod_ref[0, 4:5, :], mod_ref[0, 3:4, :])
    h_ref[...] = h.astype(BF16)
    acc = jnp.zeros(x.shape, F32)
    for j in range(D_FF // FF_TN):
        fs = slice(j * FF_TN, (j + 1) * FF_TN)
        u = jnp.maximum(_dot(h_ref[...], w1_ref[:, fs]), 0.0)
        acc = acc + _dot((u * u).astype(BF16), w2_ref[fs, :])
    out_ref[0] = x + mod_ref[0, 5:6, :] * acc


def _mlp(x, mod3, norm_w, w1, w2, tm):
    bsz, s, d = x.shape
    row = lambda b, i: (b, i, 0)
    return pl.pallas_call(
        _mlp_kernel,
        grid=(bsz, s // tm),
        in_specs=[
            pl.BlockSpec((1, tm, d), row),
            pl.BlockSpec((1, N_MOD, d), lambda b, i: (b, 0, 0)),
            _const_spec((1, d)),
            _const_spec((d, D_FF)),
            _const_spec((D_FF, d)),
        ],
        out_specs=pl.BlockSpec((1, tm, d), row),
        out_shape=jax.ShapeDtypeStruct((bsz, s, d), F32),
        scratch_shapes=[pltpu.VMEM((tm, d), BF16)],
        compiler_params=pltpu.CompilerParams(
            dimension_semantics=("parallel", "arbitrary"), vmem_limit_bytes=VMEM_LIMIT),
        name="mlp",
    )(x, mod3, norm_w, w1, w2)


def _head_expand_matrix():
    head_of_lane = jnp.arange(D_INNER) // HEAD_DIM
    e = (jnp.arange(LANES)[:, None] == head_of_lane[None, :]).astype(BF16)
    return jnp.concatenate([e, e], axis=0)


def _pad_lanes(v, n=LANES):
    return jnp.pad(v.astype(F32), (0, n - v.shape[0])).reshape(1, n)


def kernel(x, c, norm1_w, norm2_w, w_ada, b_ada, w_in, conv_w, conv_b, dt_bias, a_log, d_skip,
           ssd_norm_w, q_norm_w, k_norm_w, attn_norm_w, w_out, w_ff1, w_ff2):
    bsz, s, d = x.shape
    depth = w_ada.shape[0]
    tm = 512
    e2 = _head_expand_matrix()
    idx = jnp.arange(256) // HEAD_DIM
    bd = (idx[:, None] == idx[None, :]).astype(BF16)
    t = (jnp.arange(SSD_CHUNK)[:, None] >= jnp.arange(SSD_CHUNK)[None, :]).astype(BF16)
    tril2 = jnp.concatenate([t, t], axis=1)
    o_xbc = D_INNER + CONV_CH
    o_dt = o_xbc + N_HEADS

    for l in range(depth):
        mod3 = _modulation(c, w_ada[l], b_ada[l]).reshape(bsz, N_MOD, d)
        wl = w_in[l]
        w_dt = jnp.pad(wl[:, o_xbc:o_dt], ((0, 0), (0, LANES - N_HEADS))).astype(BF16)
        z, xbc, dt, q16, k16, v16 = _in_projection(
            x, mod3, norm1_w[l].reshape(1, d), wl[:, :o_xbc].astype(BF16),
            wl[:, o_dt:].astype(BF16), w_dt, _pad_lanes(dt_bias[l]),
            jnp.tile(q_norm_w[l], N_HEADS).reshape(1, D_INNER),
            jnp.tile(k_norm_w[l], N_HEADS).reshape(1, D_INNER), bd, tm)
        y_ssd = _ssd_mixer(
            xbc, z, dt, conv_w[l], conv_b[l].reshape(1, CONV_CH),
            _pad_lanes(a_log[l]), jnp.repeat(d_skip[l], HEAD_DIM).reshape(1, D_INNER),
            ssd_norm_w[l].reshape(1, D_INNER), tril2, e2)
        score_bound = (HEAD_DIM ** 0.5 * LOG2E * jnp.max(jnp.abs(q_norm_w[l]))
                       * jnp.max(jnp.abs(k_norm_w[l])))
        bounded = (score_bound * BF16_SLACK <= SCORE_BOUND).astype(jnp.int32).reshape(1)
        running = None
        for window, dilation in DILATED_PATTERNS:
            assert window // dilation == ATT_BLK
            running = _dilated_attention(bounded, q16, k16, v16, dilation, e2, running)
        x = _out_projection(x, mod3, y_ssd, *running, attn_norm_w[l].reshape(1, D_INNER),
                            e2, w_out[l].astype(BF16), tm)
        x = _mlp(x, mod3, norm2_w[l].reshape(1, d), w_ff1[l].astype(BF16),
                 w_ff2[l].astype(BF16), tm)
    return x.astype(c.dtype)
```

```python
import functools

import jax
import jax.numpy as jnp
from jax import lax
from jax.experimental import pallas as pl
from jax.experimental.pallas import tpu as pltpu

F32 = jnp.float32
BF16 = jnp.bfloat16

D_MODEL = 1024
HEAD_DIM = 64
N_HEADS = 16
SSD_GROUPS = 4
SSD_STATE = 128
SSD_CONV = 4
SSD_CHUNK = 128
D_INNER = N_HEADS * HEAD_DIM
CONV_CH = D_INNER + 2 * SSD_GROUPS * SSD_STATE
D_FF = 4 * D_MODEL
N_MOD = 6
EPS = 1e-6
DILATED_PATTERNS = ((128, 1), (512, 4), (2048, 16))
ATT_BLK = 128
ATT_GROUP = 8
CLS_STRIDE = 16
ROW_TILE = 512
LANES = 128
MXU_TILE = 256
VMEM_LIMIT = 56 * 1024 * 1024

LOG2E = 1.4426950408889634
SCORE_BOUND = 60.0
BF16_SLACK = 1.02

NT_DIMS = (((1,), (1,)), ((), ()))
TN_DIMS = (((0,), (0,)), ((), ()))


def _dot(a, b):
    return jnp.dot(a, b, preferred_element_type=F32)


def _split_bf16(v):
    hi = v.astype(BF16)
    lo = (v - hi.astype(F32)).astype(BF16)
    return hi, lo


def _silu(v):
    h = 0.5 * v
    return h + h * jnp.tanh(h)


def _const_spec(shape):
    nd = len(shape)
    return pl.BlockSpec(shape, lambda *_: (0,) * nd, pipeline_mode=pl.Buffered(1))


def _mod_kernel(c_ref, w_ref, b_ref, o_ref):
    c = c_ref[...]
    ca = _silu(c)
    c_hi, c_lo = _split_bf16(ca)
    w = w_ref[...]
    w_hi, w_lo = _split_bf16(w)
    acc = _dot(c_hi, w_hi) + _dot(c_lo, w_hi) + _dot(c_hi, w_lo)
    o_ref[...] = acc + b_ref[...]


def _modulation(c, w_ada, b_ada):
    bsz, d = c.shape
    n = w_ada.shape[1]
    tn = n // 4
    return pl.pallas_call(
        _mod_kernel,
        grid=(n // tn,),
        in_specs=[
            pl.BlockSpec((bsz, d), lambda j: (0, 0)),
            pl.BlockSpec((d, tn), lambda j: (0, j)),
            pl.BlockSpec((1, tn), lambda j: (0, j)),
        ],
        out_specs=pl.BlockSpec((bsz, tn), lambda j: (0, j)),
        out_shape=jax.ShapeDtypeStruct((bsz, n), F32),
        compiler_params=pltpu.CompilerParams(
            dimension_semantics=("arbitrary",), vmem_limit_bytes=VMEM_LIMIT),
        name="adaln_mod",
    )(c, w_ada, b_ada.reshape(1, n))


IN_TN = 512


def _norm_mod(x, nw, scale, shift):
    ms = jnp.mean(x * x, axis=-1, keepdims=True)
    return (x * lax.rsqrt(ms + EPS) * nw) * (1.0 + scale) + shift


def _inproj_kernel(x_ref, mod_ref, nw_ref, wzx_ref, wqkv_ref, wdt_ref, dtb_ref, qw_ref, kw_ref,
                   bd_ref, z_ref, xbc_ref, dt_ref, q16_ref, k16_ref, v16_ref,
                   h_ref, stage_ref, stage2_ref):
    x = x_ref[0]
    tm = x.shape[0]
    h = _norm_mod(x, nw_ref[...], mod_ref[0, 1:2, :], mod_ref[0, 0:1, :])
    h_ref[...] = h.astype(BF16)

    def proj(w_ref, c0, width):
        return _dot(h_ref[...], w_ref[:, c0:c0 + width])

    def qk_norm(acc, w):
        sq = (acc * acc).astype(BF16)
        parts = [_dot(sq[:, i:i + MXU_TILE], bd_ref[...]) for i in range(0, IN_TN, MXU_TILE)]
        ss = jnp.concatenate(parts, axis=1)
        return acc * lax.rsqrt(ss * (1.0 / HEAD_DIM) + EPS) * w

    cls_rows = x.shape[0] // CLS_STRIDE

    def emit(cls_ref, sl, val):
        tiles = IN_TN // LANES
        quarter = tm // 4
        for t in range(tiles):
            stage_ref[t] = val[:, t * LANES:(t + 1) * LANES]
        for t in range(tiles):
            for rho in range(4):
                stage2_ref[t, rho * quarter:(rho + 1) * quarter, :] = (
                    stage_ref[t, pl.ds(rho, quarter, stride=4), :])
        for res in range(CLS_STRIDE):
            rho, a = res % 4, res // 4
            rows = [stage2_ref[t, pl.ds(rho * quarter + a, cls_rows, stride=4), :]
                    for t in range(tiles)]
            cls_ref[0, res, :, sl] = jnp.concatenate(rows, axis=1).astype(BF16)

    col = 0
    for j in range(D_INNER // IN_TN):
        z_ref[0, :, j * IN_TN:(j + 1) * IN_TN] = proj(wzx_ref, col, IN_TN).astype(BF16)
        col += IN_TN
    for j in range(CONV_CH // IN_TN):
        xbc_ref[0, :, j * IN_TN:(j + 1) * IN_TN] = proj(wzx_ref, col, IN_TN).astype(BF16)
        col += IN_TN
    col = 0
    for j in range(D_INNER // IN_TN):
        sl = slice(j * IN_TN, (j + 1) * IN_TN)
        emit(q16_ref, sl,
             qk_norm(proj(wqkv_ref, col, IN_TN), qw_ref[:, sl]) * (HEAD_DIM ** -0.5 * LOG2E))
        col += IN_TN
    for j in range(D_INNER // IN_TN):
        sl = slice(j * IN_TN, (j + 1) * IN_TN)
        emit(k16_ref, sl, qk_norm(proj(wqkv_ref, col, IN_TN), kw_ref[:, sl]))
        col += IN_TN
    for j in range(D_INNER // IN_TN):
        emit(v16_ref, slice(j * IN_TN, (j + 1) * IN_TN), proj(wqkv_ref, col, IN_TN))
        col += IN_TN
    dt_raw = proj(wdt_ref, 0, LANES) + dtb_ref[...]
    dt_ref[0] = jnp.maximum(dt_raw, 0.0) + jnp.log(1.0 + jnp.exp(-jnp.abs(dt_raw)))


def _in_projection(x, mod3, norm_w, w_zx, w_qkv, w_dt, dt_bias, qw, kw, bd, tm):
    bsz, s, d = x.shape
    row = lambda b, i: (b, i, 0)
    out_bf = lambda n: jax.ShapeDtypeStruct((bsz, s, n), BF16)
    cls_spec = pl.BlockSpec((1, CLS_STRIDE, tm // CLS_STRIDE, D_INNER), lambda b, i: (b, 0, i, 0))
    cls_shape = jax.ShapeDtypeStruct((bsz, CLS_STRIDE, s // CLS_STRIDE, D_INNER), BF16)
    return pl.pallas_call(
        _inproj_kernel,
        grid=(bsz, s // tm),
        in_specs=[
            pl.BlockSpec((1, tm, d), row),
            pl.BlockSpec((1, N_MOD, d), lambda b, i: (b, 0, 0)),
            _const_spec((1, d)),
            _const_spec(w_zx.shape),
            _const_spec(w_qkv.shape),
            _const_spec(w_dt.shape),
            _const_spec((1, LANES)),
            _const_spec((1, D_INNER)),
            _const_spec((1, D_INNER)),
            _const_spec((MXU_TILE, MXU_TILE)),
        ],
        out_specs=[
            pl.BlockSpec((1, tm, D_INNER), row),
            pl.BlockSpec((1, tm, CONV_CH), row),
            pl.BlockSpec((1, tm, LANES), row),
            cls_spec, cls_spec, cls_spec,
        ],
        out_shape=[out_bf(D_INNER), out_bf(CONV_CH),
                   jax.ShapeDtypeStruct((bsz, s, LANES), F32),
                   cls_shape, cls_shape, cls_shape],
        scratch_shapes=[pltpu.VMEM((tm, d), BF16),
                        pltpu.VMEM((IN_TN // LANES, tm, LANES), F32),
                        pltpu.VMEM((IN_TN // LANES, tm, LANES), F32)],
        compiler_params=pltpu.CompilerParams(
            dimension_semantics=("parallel", "arbitrary"), vmem_limit_bytes=VMEM_LIMIT),
        name="in_proj",
    )(x, mod3, norm_w, w_zx, w_qkv, w_dt, dt_bias, qw, kw, bd)


SSD_ROWS = 1024
CONV_HALO = 8


def _ssd_kernel(xbc_ref, z_ref, dt_ref, cw_ref, cb_ref, alog_ref, dskip_ref,
                nw_ref, tril2_ref, e2_ref, shift_ref, o_ref, halo_ref, xc_ref, state_ref):
    rows = xbc_ref.shape[1]
    L = SSD_CHUNK
    taps = SSD_CONV - 1

    @pl.when(pl.program_id(1) == 0)
    def _():
        halo_ref[...] = jnp.zeros_like(halo_ref)
        state_ref[...] = jnp.zeros_like(state_ref)

    hsub = lax.broadcasted_iota(jnp.int32, (CONV_HALO, MXU_TILE), 0)
    group_w = D_INNER // SSD_GROUPS
    heads_per_group = N_HEADS // SSD_GROUPS

    def conv_chunk(c):
        r0 = c * L
        for c0 in range(0, CONV_CH, MXU_TILE):
            cs = slice(c0, c0 + MXU_TILE)
            u = xbc_ref[0, r0:r0 + L, cs]
            shifted = _dot(shift_ref[...], u)
            uf = u.astype(F32)
            acc = cb_ref[:, cs] + cw_ref[taps:taps + 1, cs] * uf
            for k in range(1, taps + 1):
                acc = acc + cw_ref[taps - k:taps - k + 1, cs] * shifted[(k - 1) * L:k * L]
            xc_ref[r0:r0 + L, cs] = _silu(acc)
            if c == 0:
                halo = halo_ref[:, cs]
            else:
                halo = xbc_ref[0, r0 - 2 * CONV_HALO:r0, cs].astype(F32)[CONV_HALO:]
            head = acc[0:CONV_HALO]
            for k in range(1, taps + 1):
                prev = jnp.where(hsub < k, pltpu.roll(halo, k, axis=0), 0.0)
                head = head + cw_ref[taps - k:taps - k + 1, cs] * prev
            xc_ref[r0:r0 + CONV_HALO, cs] = _silu(head)
            if r0 + L == rows:
                halo_ref[:, cs] = uf[L - CONV_HALO:L]

    lane = lax.broadcasted_iota(jnp.int32, (L, LANES), 1)
    sub = lax.broadcasted_iota(jnp.int32, (L, LANES), 0)
    tril = sub >= lane
    lo_half = lane < HEAD_DIM
    a_neg = jnp.where(lane[0:1] < N_HEADS, -jnp.exp(alog_ref[...]), 0.0)

    def split_cat(v):
        hi, lo = _split_bf16(v)
        return jnp.concatenate([hi, lo], axis=1)

    def expand(v_cat, gs):
        return _dot(v_cat, e2_ref[:, gs])

    conv_chunk(0)
    for c in range(rows // L):
        if c + 1 < rows // L:
            conv_chunk(c + 1)
        r0 = c * L
        rs = slice(r0, r0 + L)
        dt = dt_ref[0, rs, :]
        d_a = dt * a_neg
        da_hi, da_lo = _split_bf16(d_a)
        a_cs = _dot(tril2_ref[...], jnp.concatenate([da_hi, da_lo], axis=0))
        a_cs_t = a_cs.T
        a_last = a_cs[L - 1:L, :]
        dt_cat = split_cat(dt)
        w_cat = split_cat(dt * jnp.exp(a_last - a_cs))
        do_cat = split_cat(jnp.exp(a_cs))

        for g in range(SSD_GROUPS):
            gs = slice(g * group_w, (g + 1) * group_w)
            xs = xc_ref[rs, gs]
            do_x = expand(do_cat, gs)
            xdt = (xs * expand(dt_cat, gs)).astype(BF16)
            xw = (xs * expand(w_cat, gs)).astype(BF16)
            b0 = D_INNER + g * SSD_STATE
            c0 = D_INNER + SSD_GROUPS * SSD_STATE + g * SSD_STATE
            bg = xc_ref[rs, b0:b0 + SSD_STATE].astype(BF16)
            cg = xc_ref[rs, c0:c0 + SSD_STATE].astype(BF16)
            cb = lax.dot_general(cg, bg, NT_DIMS, preferred_element_type=F32)
            yd_parts = []
            for pair in range(heads_per_group // 2):
                ms = []
                for e in range(2):
                    hd = g * heads_per_group + pair * 2 + e
                    colb = jnp.broadcast_to(a_cs[:, hd:hd + 1], (L, L))
                    rowb = jnp.broadcast_to(a_cs_t[hd:hd + 1, :], (L, L))
                    lmat = jnp.exp(jnp.where(tril, colb - rowb, -jnp.inf))
                    ms.append((cb * lmat).astype(BF16))
                mcat = jnp.concatenate(ms, axis=1)
                xp = xdt[:, pair * LANES:(pair + 1) * LANES]
                zero = jnp.zeros_like(xp)
                xstack = jnp.concatenate(
                    [jnp.where(lo_half, xp, zero), jnp.where(lo_half, zero, xp)], axis=0)
                yd_parts.append(_dot(mcat, xstack))
            y_diag = jnp.concatenate(yd_parts, axis=1)
            st = state_ref[:, gs]
            y_off = _dot(cg, st.astype(BF16)) * do_x
            upd = lax.dot_general(bg, xw, TN_DIMS, preferred_element_type=F32)
            state_ref[:, gs] = st * do_x[L - 1:L] + upd
            gate = _silu(z_ref[0, rs, gs].astype(F32))
            yg = (dskip_ref[:, gs] * xs + y_diag + y_off) * gate
            ss = jnp.mean(yg * yg, axis=-1, keepdims=True)
            o_ref[0, rs, gs] = (yg * lax.rsqrt(ss + EPS) * nw_ref[:, gs]).astype(BF16)


def _ssd_mixer(xbc, z, dt, conv_w, conv_b, a_log, d_skip, norm_w, tril2, e2):
    bsz, s, _ = xbc.shape
    rows = SSD_ROWS
    row = lambda b, i: (b, i, 0)
    t = jnp.arange(SSD_CHUNK)
    shift = jnp.concatenate(
        [(t[:, None] - k == t[None, :]).astype(BF16) for k in range(1, SSD_CONV)], axis=0)
    return pl.pallas_call(
        _ssd_kernel,
        grid=(bsz, s // rows),
        in_specs=[
            pl.BlockSpec((1, rows, CONV_CH), row),
            pl.BlockSpec((1, rows, D_INNER), row),
            pl.BlockSpec((1, rows, LANES), row),
            _const_spec((SSD_CONV, CONV_CH)),
            _const_spec((1, CONV_CH)),
            _const_spec((1, LANES)),
            _const_spec((1, D_INNER)),
            _const_spec((1, D_INNER)),
            _const_spec((SSD_CHUNK, 2 * SSD_CHUNK)),
            _const_spec((2 * LANES, D_INNER)),
            _const_spec(((SSD_CONV - 1) * SSD_CHUNK, SSD_CHUNK)),
        ],
        out_specs=pl.BlockSpec((1, rows, D_INNER), row),
        out_shape=jax.ShapeDtypeStruct((bsz, s, D_INNER), BF16),
        scratch_shapes=[
            pltpu.VMEM((CONV_HALO, CONV_CH), F32),
            pltpu.VMEM((rows, CONV_CH), F32),
            pltpu.VMEM((SSD_STATE, D_INNER), F32),
        ],
        compiler_params=pltpu.CompilerParams(
            dimension_semantics=("parallel", "arbitrary"), vmem_limit_bytes=VMEM_LIMIT),
        name="ssd_mixer",
    )(xbc, z, dt, conv_w, conv_b, a_log, d_skip, norm_w, tril2, e2, shift)


def _attn_kernel(planes, group, chain, bounded_ref, q_ref, k_ref, v_ref, kp_ref, vp_ref, *refs):
    if chain:
        acc_in_ref, st_in_ref, e2_ref, o_ref, lse_ref = refs
        assert planes in (1, 4)
    else:
        o_ref, lse_ref = refs
    blk = ATT_BLK
    sub = blk // planes
    first = pl.program_id(2) == 0

    def pos(i):
        return i if planes == 1 else (i % sub) * planes + i // sub

    row = lax.broadcasted_iota(jnp.int32, (blk, 2 * blk), 0)
    key = lax.broadcasted_iota(jnp.int32, (blk, 2 * blk), 1)
    rel = jnp.where(key >= blk, blk + pos(key - blk), pos(key)) - pos(row)
    band = (rel >= 0) & (rel <= blk)
    lo_key = jnp.where(first, blk, 0)
    bias_head = jnp.where(band & (key >= lo_key), 0.0, -jnp.inf)
    bias_head = jnp.concatenate([bias_head, bias_head], axis=0)
    bias_rest = jnp.where(band, 0.0, -jnp.inf)
    bias_rest = jnp.concatenate([bias_rest, bias_rest], axis=0)
    lane = lax.broadcasted_iota(jnp.int32, (blk, LANES), 1)
    lo_half = lane < HEAD_DIM

    n_res = q_ref.shape[1] if planes == 1 else 1

    half = 2 * sub

    def load(ref, r, g, ps):
        if planes == 1:
            return ref[0, r, g * blk:(g + 1) * blk, ps]
        if planes == 4:
            return ref[0, :, 0, g * sub:(g + 1) * sub, ps].reshape(blk, ps.stop - ps.start)
        rows = ref[0, :, (g // 2) * half:(g // 2 + 1) * half, ps].astype(F32)
        return rows[:, (g % 2) * sub:(g % 2 + 1) * sub, :].reshape(blk, LANES).astype(BF16)

    def store_cols(ref, r, g, ps, val):
        if planes == 1:
            ref[0, r, g * blk:(g + 1) * blk, ps] = val
        elif planes == 4:
            ref[0, :, 0, g * sub:(g + 1) * sub, ps] = val.reshape(planes, sub, val.shape[-1])
        else:
            ref[0, :, g * sub:(g + 1) * sub, ps] = val.reshape(planes, sub, val.shape[-1])

    pending = {}

    def store_out(r, g, ps, val):
        if planes == 1:
            o_ref[0, r, g * blk:(g + 1) * blk, ps] = val.astype(BF16)
        elif planes == 4:
            o_ref[0, :, 0, g * sub:(g + 1) * sub, ps] = (
                val.reshape(planes, sub, val.shape[-1]).astype(BF16))
        elif g % 2 == 0:
            pending[ps.start] = val.reshape(planes, sub, LANES)
        else:
            both = jnp.concatenate(
                [pending.pop(ps.start), val.reshape(planes, sub, LANES)], axis=1)
            o_ref[0, :, (g // 2) * half:(g // 2 + 1) * half, ps] = both.astype(BF16)

    def pair(bounded, r, g, p):
        ps = slice(p * LANES, (p + 1) * LANES)
        qp = load(q_ref, r, g, ps)
        zero = jnp.zeros_like(qp)
        qs = jnp.concatenate(
            [jnp.where(lo_half, qp, zero), jnp.where(lo_half, zero, qp)], axis=0)
        if g == 0:
            last = kp_ref.shape[-2] // sub - 1
            k_prev, v_prev = load(kp_ref, r, last, ps), load(vp_ref, r, last, ps)
        else:
            k_prev, v_prev = load(k_ref, r, g - 1, ps), load(v_ref, r, g - 1, ps)
        k2 = jnp.concatenate([k_prev, load(k_ref, r, g, ps)], axis=0)
        v2 = jnp.concatenate([v_prev, load(v_ref, r, g, ps)], axis=0)
        s = lax.dot_general(qs, k2, NT_DIMS, preferred_element_type=F32)
        s = s + (bias_head if g == 0 else bias_rest)
        if bounded:
            e = jnp.exp2(s)
        else:
            m = jnp.max(s, axis=-1, keepdims=True)
            e = jnp.exp2(s - m)
        l = jnp.sum(e, axis=-1, keepdims=True)
        pv = _dot(e.astype(BF16), v2)
        out = jnp.where(lo_half, pv[0:blk], pv[blk:])
        if chain and bounded:
            out = out + load(acc_in_ref, r, g, ps).astype(F32)
        store_out(r, g, ps, out)
        store_cols(lse_ref, r, g, slice(2 * p, 2 * p + 1), l[0:blk])
        store_cols(lse_ref, r, g, slice(2 * p + 1, 2 * p + 2), l[blk:])
        if not bounded:
            c0 = N_HEADS + 2 * p
            store_cols(lse_ref, r, g, slice(c0, c0 + 1), m[0:blk])
            store_cols(lse_ref, r, g, slice(c0 + 1, c0 + 2), m[blk:])

    def merge_block(r, g):
        everything = slice(0, LANES)
        head_lane = lane < N_HEADS
        s_own, s_prev = load(lse_ref, r, g, everything), load(st_in_ref, r, g, everything)
        m_own = pltpu.roll(s_own, LANES - N_HEADS, axis=1)
        m_prev = pltpu.roll(s_prev, LANES - N_HEADS, axis=1)
        m_new = jnp.maximum(m_prev, m_own)
        a = jnp.where(head_lane, jnp.exp2(m_prev - m_new), 0.0)
        b = jnp.where(head_lane, jnp.exp2(m_own - m_new), 0.0)
        stats = (jnp.where(head_lane, a * s_prev + b * s_own, 0.0)
                 + pltpu.roll(jnp.where(head_lane, m_new, 0.0), N_HEADS, axis=1))
        store_cols(lse_ref, r, g, everything, stats)

        def expand(v):
            hi, lo = _split_bf16(v)
            return _dot(jnp.concatenate([hi, lo], axis=1), e2_ref[...])

        wide = slice(0, o_ref.shape[-1])
        merged = (expand(a) * load(acc_in_ref, r, g, wide).astype(F32)
                  + expand(b) * load(o_ref, r, g, wide).astype(F32))
        store_out(r, g, wide, merged)

    def body(bounded):
        lse_ref[...] = jnp.zeros(lse_ref.shape, F32)
        for r in range(n_res):
            if planes == 16:
                order = [(g0 + h, p) for g0 in range(0, group, 2)
                         for p in range(N_HEADS // 2) for h in range(2)]
            else:
                order = [(g, p) for g in range(group) for p in range(N_HEADS // 2)]
            for g, p in order:
                pair(bounded, r, g, p)
                if chain and not bounded and p == N_HEADS // 2 - 1:
                    merge_block(r, g)
        if chain and bounded:
            lse_ref[...] = lse_ref[...] + st_in_ref[...]

    bounded = bounded_ref[0] != 0

    @pl.when(bounded)
    def _():
        body(True)

    @pl.when(jnp.logical_not(bounded))
    def _():
        body(False)


def _dilated_attention(bounded, q, k, v, dilation, e2, running=None):
    bsz, n_planes, rows, w = q.shape
    planes = n_planes // dilation
    assert planes in (1, 4, 16)
    sub = ATT_BLK // planes
    nb = rows // sub
    group = min(ATT_GROUP, nb)
    assert rows % sub == 0 and nb % group == 0
    n_res = 1
    window = 1
    if planes == 1:
        n_res = min(ATT_GROUP // group, dilation)
        view = lambda t: t
        blk_shape = lambda n, width: (1, n_res, n * ATT_BLK, width)
        cur = lambda b, res, j: (b, res, j, 0)
        prev = lambda b, res, j: (b, res, jnp.maximum(j * group - 1, 0), 0)
    elif planes == 4:
        view = lambda t: t.reshape(bsz, planes, dilation, rows, t.shape[-1])
        blk_shape = lambda n, width: (1, planes, 1, n * sub, width)
        cur = lambda b, res, j: (b, 0, res, j, 0)
        prev = lambda b, res, j: (b, 0, res, jnp.maximum(j * group - 1, 0), 0)
    else:
        assert group % 2 == 0
        window = 2
        view = lambda t: t
        blk_shape = lambda n, width: (1, planes, n * sub, width)
        cur = lambda b, res, j: (b, 0, j, 0)
        prev = lambda b, res, j: (b, 0, jnp.maximum(j * (group // 2) - 1, 0), 0)
    tile = lambda width=w: pl.BlockSpec(blk_shape(group, width), cur)
    single = pl.BlockSpec(blk_shape(window, w), prev)
    lse_shape = (bsz, n_planes, rows, LANES)
    chain = running is not None
    extra_specs = [tile(), tile(LANES), _const_spec(e2.shape)] if chain else []
    extra_args = [view(running[0]), view(running[1]), e2] if chain else []
    o, lse = pl.pallas_call(
        functools.partial(_attn_kernel, planes, group, chain),
        grid=(bsz, dilation // n_res, nb // group),
        in_specs=[pl.BlockSpec(memory_space=pltpu.SMEM), tile(), tile(), tile(), single, single,
                  *extra_specs],
        out_specs=[tile(), tile(LANES)],
        out_shape=[jax.ShapeDtypeStruct(_view_shape(q.shape, planes, dilation), BF16),
                   jax.ShapeDtypeStruct(_view_shape(lse_shape, planes, dilation), F32)],
        compiler_params=pltpu.CompilerParams(
            dimension_semantics=("parallel", "parallel", "arbitrary"),
            vmem_limit_bytes=VMEM_LIMIT),
        name=f"dilated_attn_r{dilation}",
    )(bounded, view(q), view(k), view(v), view(k), view(v), *extra_args)
    return o.reshape(q.shape), lse.reshape(lse_shape)


def _view_shape(shape, planes, dilation):
    bsz, _, rows, width = shape
    return (bsz, planes, dilation, rows, width) if planes == 4 else shape


def _outproj_kernel(x_ref, mod_ref, ys_ref, o_ref, st_ref, nw_ref, e2_ref, w_ref, out_ref,
                    wide_ref, wide2_ref, acc_ref):
    tm = x_ref.shape[1]
    cls_rows = tm // CLS_STRIDE
    acc_ref[...] = _dot(ys_ref[0], w_ref[0:D_INNER, :])

    def regrouped(cls_ref):
        return cls_ref[0].reshape(tm, cls_ref.shape[-1])

    def natural(val):
        tiles = wide_ref.shape[0]
        quarter = cls_rows * 4
        for res in range(CLS_STRIDE):
            rho, a = res % 4, res // 4
            for t in range(tiles):
                wide2_ref[t, pl.ds(rho * quarter + a, cls_rows, stride=4), :] = (
                    val[res * cls_rows:(res + 1) * cls_rows, t * LANES:(t + 1) * LANES])
        for t in range(tiles):
            for rho in range(4):
                wide_ref[t, pl.ds(rho, quarter, stride=4), :] = (
                    wide2_ref[t, rho * quarter:(rho + 1) * quarter, :])
        return jnp.concatenate([wide_ref[t] for t in range(tiles)], axis=1)

    st = regrouped(st_ref)
    head_lane = lax.broadcasted_iota(jnp.int32, st.shape, 1) < N_HEADS
    hi, lo = _split_bf16(jnp.where(head_lane, 1.0 / st, 0.0))
    inv_x = _dot(jnp.concatenate([hi, lo], axis=1), e2_ref[...])
    o = inv_x * regrouped(o_ref).astype(F32)
    ms = jnp.mean(o * o, axis=-1, keepdims=True)
    y_att = (o * lax.rsqrt(ms + EPS) * nw_ref[...]).astype(BF16)
    mix = acc_ref[...] + natural(_dot(y_att, w_ref[D_INNER:, :]))
    out_ref[0] = x_ref[0] + mod_ref[0, 2:3, :] * mix


def _out_projection(x, mod3, y_ssd, att_acc, att_stats, norm_w, e2, w_out, tm):
    bsz, s, d = x.shape
    row = lambda b, i: (b, i, 0)
    wide = pl.BlockSpec((1, tm, D_INNER), row)
    cls = lambda width: pl.BlockSpec((1, CLS_STRIDE, tm // CLS_STRIDE, width),
                                     lambda b, i: (b, 0, i, 0))
    return pl.pallas_call(
        _outproj_kernel,
        grid=(bsz, s // tm),
        in_specs=[
            pl.BlockSpec((1, tm, d), row),
            pl.BlockSpec((1, N_MOD, d), lambda b, i: (b, 0, 0)),
            wide, cls(D_INNER), cls(LANES),
            _const_spec((1, D_INNER)),
            _const_spec((2 * LANES, D_INNER)),
            _const_spec((2 * D_INNER, d)),
        ],
        out_specs=pl.BlockSpec((1, tm, d), row),
        out_shape=jax.ShapeDtypeStruct((bsz, s, d), F32),
        scratch_shapes=[pltpu.VMEM((d // LANES, tm, LANES), F32),
                        pltpu.VMEM((d // LANES, tm, LANES), F32),
                        pltpu.VMEM((tm, d), F32)],
        compiler_params=pltpu.CompilerParams(
            dimension_semantics=("parallel", "arbitrary"), vmem_limit_bytes=VMEM_LIMIT),
        name="out_proj",
    )(x, mod3, y_ssd, att_acc, att_stats, norm_w, e2, w_out)


FF_TN = 1024


def _mlp_kernel(x_ref, mod_ref, nw_ref, w1_ref, w2_ref, out_ref, h_ref):
    x = x_ref[0]
    h = _norm_mod(x, nw_ref[...], mod_ref[0, 4:5, :], mod_ref[0, 3:4, :])
    h_ref[...] = h.astype(BF16)
    acc = jnp.zeros(x.shape, F32)
    for j in range(D_FF // FF_TN):
        fs = slice(j * FF_TN, (j + 1) * FF_TN)
        u = jnp.maximum(_dot(h_ref[...], w1_ref[:, fs]), 0.0)
        acc = acc + _dot((u * u).astype(BF16), w2_ref[fs, :])
    out_ref[0] = x + mod_ref[0, 5:6, :] * acc


def _mlp(x, mod3, norm_w, w1, w2, tm):
    bsz, s, d = x.shape
    row = lambda b, i: (b, i, 0)
    return pl.pallas_call(
        _mlp_kernel,
        grid=(bsz, s // tm),
        in_specs=[
            pl.BlockSpec((1, tm, d), row),
            pl.BlockSpec((1, N_MOD, d), lambda b, i: (b, 0, 0)),
            _const_spec((1, d)),
            _const_spec((d, D_FF)),
            _const_spec((D_FF, d)),
        ],
        out_specs=pl.BlockSpec((1, tm, d), row),
        out_shape=jax.ShapeDtypeStruct((bsz, s, d), F32),
        scratch_shapes=[pltpu.VMEM((tm, d), BF16)],
        compiler_params=pltpu.CompilerParams(
            dimension_semantics=("parallel", "arbitrary"), vmem_limit_bytes=VMEM_LIMIT),
        name="mlp",
    )(x, mod3, norm_w, w1, w2)


def _head_expand_matrix():
    head_of_lane = jnp.arange(D_INNER) // HEAD_DIM
    e = (jnp.arange(LANES)[:, None] == head_of_lane[None, :]).astype(BF16)
    return jnp.concatenate([e, e], axis=0)


def _pad_lanes(v, n=LANES):
    return jnp.pad(v.astype(F32), (0, n - v.shape[0])).reshape(1, n)


def kernel(x, c, norm1_w, norm2_w, w_ada, b_ada, w_in, conv_w, conv_b, dt_bias, a_log, d_skip,
           ssd_norm_w, q_norm_w, k_norm_w, attn_norm_w, w_out, w_ff1, w_ff2):
    bsz, s, d = x.shape
    depth = w_ada.shape[0]
    tm = ROW_TILE
    assert d == D_MODEL and s % ROW_TILE == 0 and s % SSD_ROWS == 0
    assert s % (CLS_STRIDE * ATT_BLK) == 0
    e2 = _head_expand_matrix()
    idx = jnp.arange(MXU_TILE) // HEAD_DIM
    bd = (idx[:, None] == idx[None, :]).astype(BF16)
    t = (jnp.arange(SSD_CHUNK)[:, None] >= jnp.arange(SSD_CHUNK)[None, :]).astype(BF16)
    tril2 = jnp.concatenate([t, t], axis=1)
    o_xbc = D_INNER + CONV_CH
    o_dt = o_xbc + N_HEADS

    for l in range(depth):
        mod3 = _modulation(c, w_ada[l], b_ada[l]).reshape(bsz, N_MOD, d)
        wl = w_in[l]
        w_dt = jnp.pad(wl[:, o_xbc:o_dt], ((0, 0), (0, LANES - N_HEADS))).astype(BF16)
        z, xbc, dt, q16, k16, v16 = _in_projection(
            x, mod3, norm1_w[l].reshape(1, d), wl[:, :o_xbc].astype(BF16),
            wl[:, o_dt:].astype(BF16), w_dt, _pad_lanes(dt_bias[l]),
            jnp.tile(q_norm_w[l], N_HEADS).reshape(1, D_INNER),
            jnp.tile(k_norm_w[l], N_HEADS).reshape(1, D_INNER), bd, tm)
        y_ssd = _ssd_mixer(
            xbc, z, dt, conv_w[l], conv_b[l].reshape(1, CONV_CH),
            _pad_lanes(a_log[l]), jnp.repeat(d_skip[l], HEAD_DIM).reshape(1, D_INNER),
            ssd_norm_w[l].reshape(1, D_INNER), tril2, e2)
        score_bound = (HEAD_DIM ** 0.5 * LOG2E * jnp.max(jnp.abs(q_norm_w[l]))
                       * jnp.max(jnp.abs(k_norm_w[l])))
        bounded = (score_bound * BF16_SLACK <= SCORE_BOUND).astype(jnp.int32).reshape(1)
        running = None
        for window, dilation in DILATED_PATTERNS:
            assert window // dilation == ATT_BLK
            running = _dilated_attention(bounded, q16, k16, v16, dilation, e2, running)
        x = _out_projection(x, mod3, y_ssd, *running, attn_norm_w[l].reshape(1, D_INNER),
                            e2, w_out[l].astype(BF16), tm)
        x = _mlp(x, mod3, norm2_w[l].reshape(1, d), w_ff1[l].astype(BF16),
                 w_ff2[l].astype(BF16), tm)
    return x.astype(c.dtype)
```

```python
import functools

import jax
import jax.numpy as jnp
from jax import lax
from jax.experimental import pallas as pl
from jax.experimental.pallas import tpu as pltpu

F32 = jnp.float32
BF16 = jnp.bfloat16

D_MODEL = 1024
HEAD_DIM = 64
N_HEADS = 16
SSD_GROUPS = 4
SSD_STATE = 128
SSD_CONV = 4
SSD_CHUNK = 128
D_INNER = N_HEADS * HEAD_DIM
CONV_CH = D_INNER + 2 * SSD_GROUPS * SSD_STATE
D_FF = 4 * D_MODEL
N_MOD = 6
EPS = 1e-6
DILATED_PATTERNS = ((128, 1), (512, 4), (2048, 16))
ATT_BLK = 128
ATT_GROUP = 8
CLS_STRIDE = 16
ROW_TILE = 512
LANES = 128
MXU_TILE = 256
VMEM_LIMIT = 56 * 1024 * 1024

LOG2E = 1.4426950408889634
SCORE_BOUND = 60.0
BF16_SLACK = 1.02

NT_DIMS = (((1,), (1,)), ((), ()))
TN_DIMS = (((0,), (0,)), ((), ()))


def _dot(a, b):
    return jnp.dot(a, b, preferred_element_type=F32)


def _split_bf16(v):
    hi = v.astype(BF16)
    lo = (v - hi.astype(F32)).astype(BF16)
    return hi, lo


def _silu(v):
    h = 0.5 * v
    return h + h * jnp.tanh(h)


def _const_spec(shape):
    nd = len(shape)
    return pl.BlockSpec(shape, lambda *_: (0,) * nd, pipeline_mode=pl.Buffered(1))


def _mod_kernel(c_ref, w_ref, b_ref, o_ref):
    c = c_ref[...]
    ca = _silu(c)
    c_hi, c_lo = _split_bf16(ca)
    w = w_ref[...]
    w_hi, w_lo = _split_bf16(w)
    acc = _dot(c_hi, w_hi) + _dot(c_lo, w_hi) + _dot(c_hi, w_lo)
    o_ref[...] = acc + b_ref[...]


def _modulation(c, w_ada, b_ada):
    bsz, d = c.shape
    n = w_ada.shape[1]
    tn = n // 4
    return pl.pallas_call(
        _mod_kernel,
        grid=(n // tn,),
        in_specs=[
            pl.BlockSpec((bsz, d), lambda j: (0, 0)),
            pl.BlockSpec((d, tn), lambda j: (0, j)),
            pl.BlockSpec((1, tn), lambda j: (0, j)),
        ],
        out_specs=pl.BlockSpec((bsz, tn), lambda j: (0, j)),
        out_shape=jax.ShapeDtypeStruct((bsz, n), F32),
        compiler_params=pltpu.CompilerParams(
            dimension_semantics=("arbitrary",), vmem_limit_bytes=VMEM_LIMIT),
        name="adaln_mod",
    )(c, w_ada, b_ada.reshape(1, n))


IN_TN = 512


def _norm_mod(x, nw, scale, shift):
    ms = jnp.mean(x * x, axis=-1, keepdims=True)
    return (x * lax.rsqrt(ms + EPS) * nw) * (1.0 + scale) + shift


def _inproj_kernel(x_ref, mod_ref, nw_ref, wzx_ref, wqkv_ref, wdt_ref, dtb_ref, qw_ref, kw_ref,
                   bd_ref, z_ref, xbc_ref, dt_ref, q16_ref, k16_ref, v16_ref,
                   h_ref, stage_ref, stage2_ref):
    x = x_ref[0]
    tm = x.shape[0]
    h = _norm_mod(x, nw_ref[...], mod_ref[0, 1:2, :], mod_ref[0, 0:1, :])
    h_ref[...] = h.astype(BF16)

    def proj(w_ref, c0, width):
        return _dot(h_ref[...], w_ref[:, c0:c0 + width])

    def qk_norm(acc, w):
        sq = (acc * acc).astype(BF16)
        parts = [_dot(sq[:, i:i + MXU_TILE], bd_ref[...]) for i in range(0, IN_TN, MXU_TILE)]
        ss = jnp.concatenate(parts, axis=1)
        return acc * lax.rsqrt(ss * (1.0 / HEAD_DIM) + EPS) * w

    cls_rows = x.shape[0] // CLS_STRIDE

    def emit(cls_ref, sl, val):
        tiles = IN_TN // LANES
        quarter = tm // 4
        for t in range(tiles):
            stage_ref[t] = val[:, t * LANES:(t + 1) * LANES]
        for t in range(tiles):
            for rho in range(4):
                stage2_ref[t, rho * quarter:(rho + 1) * quarter, :] = (
                    stage_ref[t, pl.ds(rho, quarter, stride=4), :])
        for res in range(CLS_STRIDE):
            rho, a = res % 4, res // 4
            rows = [stage2_ref[t, pl.ds(rho * quarter + a, cls_rows, stride=4), :]
                    for t in range(tiles)]
            cls_ref[0, res, :, sl] = jnp.concatenate(rows, axis=1).astype(BF16)

    col = 0
    for j in range(D_INNER // IN_TN):
        sl = slice(j * IN_TN, (j + 1) * IN_TN)
        emit(q16_ref, sl,
             qk_norm(proj(wqkv_ref, col, IN_TN), qw_ref[:, sl]) * (HEAD_DIM ** -0.5 * LOG2E))
        col += IN_TN
    for j in range(D_INNER // IN_TN):
        sl = slice(j * IN_TN, (j + 1) * IN_TN)
        emit(k16_ref, sl, qk_norm(proj(wqkv_ref, col, IN_TN), kw_ref[:, sl]))
        col += IN_TN
    for j in range(D_INNER // IN_TN):
        emit(v16_ref, slice(j * IN_TN, (j + 1) * IN_TN), proj(wqkv_ref, col, IN_TN))
        col += IN_TN
    dt_raw = proj(wdt_ref, 0, LANES) + dtb_ref[...]
    dt_ref[0] = jnp.maximum(dt_raw, 0.0) + jnp.log(1.0 + jnp.exp(-jnp.abs(dt_raw)))
    col = D_INNER
    for j in range(CONV_CH // IN_TN):
        xbc_ref[0, :, j * IN_TN:(j + 1) * IN_TN] = proj(wzx_ref, col, IN_TN).astype(BF16)
        col += IN_TN
    col = 0
    for j in range(D_INNER // IN_TN):
        z_ref[0, :, j * IN_TN:(j + 1) * IN_TN] = proj(wzx_ref, col, IN_TN).astype(BF16)
        col += IN_TN


def _in_projection(x, mod3, norm_w, w_zx, w_qkv, w_dt, dt_bias, qw, kw, bd, tm):
    bsz, s, d = x.shape
    row = lambda b, i: (b, i, 0)
    out_bf = lambda n: jax.ShapeDtypeStruct((bsz, s, n), BF16)
    cls_spec = pl.BlockSpec((1, CLS_STRIDE, tm // CLS_STRIDE, D_INNER), lambda b, i: (b, 0, i, 0))
    cls_shape = jax.ShapeDtypeStruct((bsz, CLS_STRIDE, s // CLS_STRIDE, D_INNER), BF16)
    return pl.pallas_call(
        _inproj_kernel,
        grid=(bsz, s // tm),
        in_specs=[
            pl.BlockSpec((1, tm, d), row),
            pl.BlockSpec((1, N_MOD, d), lambda b, i: (b, 0, 0)),
            _const_spec((1, d)),
            _const_spec(w_zx.shape),
            _const_spec(w_qkv.shape),
            _const_spec(w_dt.shape),
            _const_spec((1, LANES)),
            _const_spec((1, D_INNER)),
            _const_spec((1, D_INNER)),
            _const_spec((MXU_TILE, MXU_TILE)),
        ],
        out_specs=[
            pl.BlockSpec((1, tm, D_INNER), row),
            pl.BlockSpec((1, tm, CONV_CH), row),
            pl.BlockSpec((1, tm, LANES), row),
            cls_spec, cls_spec, cls_spec,
        ],
        out_shape=[out_bf(D_INNER), out_bf(CONV_CH),
                   jax.ShapeDtypeStruct((bsz, s, LANES), F32),
                   cls_shape, cls_shape, cls_shape],
        scratch_shapes=[pltpu.VMEM((tm, d), BF16),
                        pltpu.VMEM((IN_TN // LANES, tm, LANES), F32),
                        pltpu.VMEM((IN_TN // LANES, tm, LANES), F32)],
        compiler_params=pltpu.CompilerParams(
            dimension_semantics=("parallel", "arbitrary"), vmem_limit_bytes=VMEM_LIMIT),
        name="in_proj",
    )(x, mod3, norm_w, w_zx, w_qkv, w_dt, dt_bias, qw, kw, bd)


SSD_ROWS = 1024
CONV_HALO = 8


def _ssd_kernel(xbc_ref, z_ref, dt_ref, cw_ref, cb_ref, alog_ref, dskip_ref,
                nw_ref, tril2_ref, e2_ref, shift_ref, o_ref, halo_ref, xc_ref, state_ref):
    rows = xbc_ref.shape[1]
    L = SSD_CHUNK
    taps = SSD_CONV - 1

    @pl.when(pl.program_id(1) == 0)
    def _():
        halo_ref[...] = jnp.zeros_like(halo_ref)
        state_ref[...] = jnp.zeros_like(state_ref)

    hsub = lax.broadcasted_iota(jnp.int32, (CONV_HALO, MXU_TILE), 0)
    group_w = D_INNER // SSD_GROUPS
    heads_per_group = N_HEADS // SSD_GROUPS

    def conv_chunk(c):
        r0 = c * L
        for c0 in range(0, CONV_CH, MXU_TILE):
            cs = slice(c0, c0 + MXU_TILE)
            u = xbc_ref[0, r0:r0 + L, cs]
            shifted = _dot(shift_ref[...], u)
            uf = u.astype(F32)
            acc = cb_ref[:, cs] + cw_ref[taps:taps + 1, cs] * uf
            for k in range(1, taps + 1):
                acc = acc + cw_ref[taps - k:taps - k + 1, cs] * shifted[(k - 1) * L:k * L]
            xc_ref[r0:r0 + L, cs] = _silu(acc)
            if c == 0:
                halo = halo_ref[:, cs]
            else:
                halo = xbc_ref[0, r0 - 2 * CONV_HALO:r0, cs].astype(F32)[CONV_HALO:]
            head = acc[0:CONV_HALO]
            for k in range(1, taps + 1):
                prev = jnp.where(hsub < k, pltpu.roll(halo, k, axis=0), 0.0)
                head = head + cw_ref[taps - k:taps - k + 1, cs] * prev
            xc_ref[r0:r0 + CONV_HALO, cs] = _silu(head)
            if r0 + L == rows:
                halo_ref[:, cs] = uf[L - CONV_HALO:L]

    lane = lax.broadcasted_iota(jnp.int32, (L, LANES), 1)
    sub = lax.broadcasted_iota(jnp.int32, (L, LANES), 0)
    tril = sub >= lane
    lo_half = lane < HEAD_DIM
    a_neg = jnp.where(lane[0:1] < N_HEADS, -jnp.exp(alog_ref[...]), 0.0)

    def split_cat(v):
        hi, lo = _split_bf16(v)
        return jnp.concatenate([hi, lo], axis=1)

    def expand(v_cat, gs):
        return _dot(v_cat, e2_ref[:, gs])

    conv_chunk(0)
    for c in range(rows // L):
        if c + 1 < rows // L:
            conv_chunk(c + 1)
        r0 = c * L
        rs = slice(r0, r0 + L)
        dt = dt_ref[0, rs, :]
        d_a = dt * a_neg
        da_hi, da_lo = _split_bf16(d_a)
        a_cs = _dot(tril2_ref[...], jnp.concatenate([da_hi, da_lo], axis=0))
        a_cs_t = a_cs.T
        a_last = a_cs[L - 1:L, :]
        dt_cat = split_cat(dt)
        w_cat = split_cat(dt * jnp.exp(a_last - a_cs))
        do_cat = split_cat(jnp.exp(a_cs))

        for g in range(SSD_GROUPS):
            gs = slice(g * group_w, (g + 1) * group_w)
            xs = xc_ref[rs, gs]
            do_x = expand(do_cat, gs)
            xdt = (xs * expand(dt_cat, gs)).astype(BF16)
            xw = (xs * expand(w_cat, gs)).astype(BF16)
            b0 = D_INNER + g * SSD_STATE
            c0 = D_INNER + SSD_GROUPS * SSD_STATE + g * SSD_STATE
            bg = xc_ref[rs, b0:b0 + SSD_STATE].astype(BF16)
            cg = xc_ref[rs, c0:c0 + SSD_STATE].astype(BF16)
            cb = lax.dot_general(cg, bg, NT_DIMS, preferred_element_type=F32)
            yd_parts = []
            for pair in range(heads_per_group // 2):
                ms = []
                for e in range(2):
                    hd = g * heads_per_group + pair * 2 + e
                    colb = jnp.broadcast_to(a_cs[:, hd:hd + 1], (L, L))
                    rowb = jnp.broadcast_to(a_cs_t[hd:hd + 1, :], (L, L))
                    lmat = jnp.exp(jnp.where(tril, colb - rowb, -jnp.inf))
                    ms.append((cb * lmat).astype(BF16))
                mcat = jnp.concatenate(ms, axis=1)
                xp = xdt[:, pair * LANES:(pair + 1) * LANES]
                zero = jnp.zeros_like(xp)
                xstack = jnp.concatenate(
                    [jnp.where(lo_half, xp, zero), jnp.where(lo_half, zero, xp)], axis=0)
                yd_parts.append(_dot(mcat, xstack))
            y_diag = jnp.concatenate(yd_parts, axis=1)
            st = state_ref[:, gs]
            y_off = _dot(cg, st.astype(BF16)) * do_x
            upd = lax.dot_general(bg, xw, TN_DIMS, preferred_element_type=F32)
            state_ref[:, gs] = st * do_x[L - 1:L] + upd
            gate = _silu(z_ref[0, rs, gs].astype(F32))
            yg = (dskip_ref[:, gs] * xs + y_diag + y_off) * gate
            ss = jnp.mean(yg * yg, axis=-1, keepdims=True)
            o_ref[0, rs, gs] = (yg * lax.rsqrt(ss + EPS) * nw_ref[:, gs]).astype(BF16)


def _ssd_mixer(xbc, z, dt, conv_w, conv_b, a_log, d_skip, norm_w, tril2, e2):
    bsz, s, _ = xbc.shape
    rows = SSD_ROWS
    row = lambda b, i: (b, i, 0)
    t = jnp.arange(SSD_CHUNK)
    shift = jnp.concatenate(
        [(t[:, None] - k == t[None, :]).astype(BF16) for k in range(1, SSD_CONV)], axis=0)
    return pl.pallas_call(
        _ssd_kernel,
        grid=(bsz, s // rows),
        in_specs=[
            pl.BlockSpec((1, rows, CONV_CH), row),
            pl.BlockSpec((1, rows, D_INNER), row),
            pl.BlockSpec((1, rows, LANES), row),
            _const_spec((SSD_CONV, CONV_CH)),
            _const_spec((1, CONV_CH)),
            _const_spec((1, LANES)),
            _const_spec((1, D_INNER)),
            _const_spec((1, D_INNER)),
            _const_spec((SSD_CHUNK, 2 * SSD_CHUNK)),
            _const_spec((2 * LANES, D_INNER)),
            _const_spec(((SSD_CONV - 1) * SSD_CHUNK, SSD_CHUNK)),
        ],
        out_specs=pl.BlockSpec((1, rows, D_INNER), row),
        out_shape=jax.ShapeDtypeStruct((bsz, s, D_INNER), BF16),
        scratch_shapes=[
            pltpu.VMEM((CONV_HALO, CONV_CH), F32),
            pltpu.VMEM((rows, CONV_CH), F32),
            pltpu.VMEM((SSD_STATE, D_INNER), F32),
        ],
        compiler_params=pltpu.CompilerParams(
            dimension_semantics=("parallel", "arbitrary"), vmem_limit_bytes=VMEM_LIMIT),
        name="ssd_mixer",
    )(xbc, z, dt, conv_w, conv_b, a_log, d_skip, norm_w, tril2, e2, shift)


def _attn_kernel(planes, group, chain, bounded_ref, q_ref, k_ref, v_ref, kp_ref, vp_ref, *refs):
    if chain:
        acc_in_ref, st_in_ref, e2_ref, o_ref, lse_ref = refs
        assert planes in (1, 4)
    else:
        o_ref, lse_ref = refs
    blk = ATT_BLK
    sub = blk // planes
    first = pl.program_id(2) == 0

    def pos(i):
        return i if planes == 1 else (i % sub) * planes + i // sub

    row = lax.broadcasted_iota(jnp.int32, (blk, 2 * blk), 0)
    key = lax.broadcasted_iota(jnp.int32, (blk, 2 * blk), 1)
    rel = jnp.where(key >= blk, blk + pos(key - blk), pos(key)) - pos(row)
    band = (rel >= 0) & (rel <= blk)
    lo_key = jnp.where(first, blk, 0)
    bias_head = jnp.where(band & (key >= lo_key), 0.0, -jnp.inf)
    bias_head = jnp.concatenate([bias_head, bias_head], axis=0)
    bias_rest = jnp.where(band, 0.0, -jnp.inf)
    bias_rest = jnp.concatenate([bias_rest, bias_rest], axis=0)
    lane = lax.broadcasted_iota(jnp.int32, (blk, LANES), 1)
    lo_half = lane < HEAD_DIM

    n_res = q_ref.shape[1] if planes == 1 else 1

    half = 2 * sub

    def load(ref, r, g, ps):
        if planes == 1:
            return ref[0, r, g * blk:(g + 1) * blk, ps]
        if planes == 4:
            return ref[0, :, 0, g * sub:(g + 1) * sub, ps].reshape(blk, ps.stop - ps.start)
        rows = ref[0, :, (g // 2) * half:(g // 2 + 1) * half, ps].astype(F32)
        return rows[:, (g % 2) * sub:(g % 2 + 1) * sub, :].reshape(blk, LANES).astype(BF16)

    def store_cols(ref, r, g, ps, val):
        if planes == 1:
            ref[0, r, g * blk:(g + 1) * blk, ps] = val
        elif planes == 4:
            ref[0, :, 0, g * sub:(g + 1) * sub, ps] = val.reshape(planes, sub, val.shape[-1])
        else:
            ref[0, :, g * sub:(g + 1) * sub, ps] = val.reshape(planes, sub, val.shape[-1])

    pending = {}

    def store_out(r, g, ps, val):
        if planes == 1:
            o_ref[0, r, g * blk:(g + 1) * blk, ps] = val.astype(BF16)
        elif planes == 4:
            o_ref[0, :, 0, g * sub:(g + 1) * sub, ps] = (
                val.reshape(planes, sub, val.shape[-1]).astype(BF16))
        elif g % 2 == 0:
            pending[ps.start] = val.reshape(planes, sub, LANES)
        else:
            both = jnp.concatenate(
                [pending.pop(ps.start), val.reshape(planes, sub, LANES)], axis=1)
            o_ref[0, :, (g // 2) * half:(g // 2 + 1) * half, ps] = both.astype(BF16)

    def pair(bounded, r, g, p):
        ps = slice(p * LANES, (p + 1) * LANES)
        qp = load(q_ref, r, g, ps)
        zero = jnp.zeros_like(qp)
        qs = jnp.concatenate(
            [jnp.where(lo_half, qp, zero), jnp.where(lo_half, zero, qp)], axis=0)
        if g == 0:
            last = kp_ref.shape[-2] // sub - 1
            k_prev, v_prev = load(kp_ref, r, last, ps), load(vp_ref, r, last, ps)
        else:
            k_prev, v_prev = load(k_ref, r, g - 1, ps), load(v_ref, r, g - 1, ps)
        k2 = jnp.concatenate([k_prev, load(k_ref, r, g, ps)], axis=0)
        v2 = jnp.concatenate([v_prev, load(v_ref, r, g, ps)], axis=0)
        s = lax.dot_general(qs, k2, NT_DIMS, preferred_element_type=F32)
        s = s + (bias_head if g == 0 else bias_rest)
        if bounded:
            e = jnp.exp2(s)
        else:
            m = jnp.max(s, axis=-1, keepdims=True)
            e = jnp.exp2(s - m)
        l = jnp.sum(e, axis=-1, keepdims=True)
        pv = _dot(e.astype(BF16), v2)
        out = jnp.where(lo_half, pv[0:blk], pv[blk:])
        if chain and bounded:
            out = out + load(acc_in_ref, r, g, ps).astype(F32)
        store_out(r, g, ps, out)
        store_cols(lse_ref, r, g, slice(2 * p, 2 * p + 1), l[0:blk])
        store_cols(lse_ref, r, g, slice(2 * p + 1, 2 * p + 2), l[blk:])
        if not bounded:
            c0 = N_HEADS + 2 * p
            store_cols(lse_ref, r, g, slice(c0, c0 + 1), m[0:blk])
            store_cols(lse_ref, r, g, slice(c0 + 1, c0 + 2), m[blk:])

    def merge_block(r, g):
        everything = slice(0, LANES)
        head_lane = lane < N_HEADS
        s_own, s_prev = load(lse_ref, r, g, everything), load(st_in_ref, r, g, everything)
        m_own = pltpu.roll(s_own, LANES - N_HEADS, axis=1)
        m_prev = pltpu.roll(s_prev, LANES - N_HEADS, axis=1)
        m_new = jnp.maximum(m_prev, m_own)
        a = jnp.where(head_lane, jnp.exp2(m_prev - m_new), 0.0)
        b = jnp.where(head_lane, jnp.exp2(m_own - m_new), 0.0)
        stats = (jnp.where(head_lane, a * s_prev + b * s_own, 0.0)
                 + pltpu.roll(jnp.where(head_lane, m_new, 0.0), N_HEADS, axis=1))
        store_cols(lse_ref, r, g, everything, stats)

        def expand(v):
            hi, lo = _split_bf16(v)
            return _dot(jnp.concatenate([hi, lo], axis=1), e2_ref[...])

        wide = slice(0, o_ref.shape[-1])
        merged = (expand(a) * load(acc_in_ref, r, g, wide).astype(F32)
                  + expand(b) * load(o_ref, r, g, wide).astype(F32))
        store_out(r, g, wide, merged)

    def body(bounded):
        lse_ref[...] = jnp.zeros(lse_ref.shape, F32)
        for r in range(n_res):
            if planes == 16:
                order = [(g0 + h, p) for g0 in range(0, group, 2)
                         for p in range(N_HEADS // 2) for h in range(2)]
            else:
                order = [(g, p) for g in range(group) for p in range(N_HEADS // 2)]
            for g, p in order:
                pair(bounded, r, g, p)
                if chain and not bounded and p == N_HEADS // 2 - 1:
                    merge_block(r, g)
        if chain and bounded:
            lse_ref[...] = lse_ref[...] + st_in_ref[...]

    bounded = bounded_ref[0] != 0

    @pl.when(bounded)
    def _():
        body(True)

    @pl.when(jnp.logical_not(bounded))
    def _():
        body(False)


def _dilated_attention(bounded, q, k, v, dilation, e2, running=None):
    bsz, n_planes, rows, w = q.shape
    planes = n_planes // dilation
    assert planes in (1, 4, 16)
    sub = ATT_BLK // planes
    nb = rows // sub
    group = min(ATT_GROUP, nb)
    assert rows % sub == 0 and nb % group == 0
    n_res = 1
    window = 1
    if planes == 1:
        n_res = min(ATT_GROUP // group, dilation)
        view = lambda t: t
        blk_shape = lambda n, width: (1, n_res, n * ATT_BLK, width)
        cur = lambda b, res, j: (b, res, j, 0)
        prev = lambda b, res, j: (b, res, jnp.maximum(j * group - 1, 0), 0)
    elif planes == 4:
        view = lambda t: t.reshape(bsz, planes, dilation, rows, t.shape[-1])
        blk_shape = lambda n, width: (1, planes, 1, n * sub, width)
        cur = lambda b, res, j: (b, 0, res, j, 0)
        prev = lambda b, res, j: (b, 0, res, jnp.maximum(j * group - 1, 0), 0)
    else:
        assert group % 2 == 0
        window = 2
        view = lambda t: t
        blk_shape = lambda n, width: (1, planes, n * sub, width)
        cur = lambda b, res, j: (b, 0, j, 0)
        prev = lambda b, res, j: (b, 0, jnp.maximum(j * (group // 2) - 1, 0), 0)
    tile = lambda width=w: pl.BlockSpec(blk_shape(group, width), cur)
    single = pl.BlockSpec(blk_shape(window, w), prev)
    lse_shape = (bsz, n_planes, rows, LANES)
    chain = running is not None
    extra_specs = [tile(), tile(LANES), _const_spec(e2.shape)] if chain else []
    extra_args = [view(running[0]), view(running[1]), e2] if chain else []
    o, lse = pl.pallas_call(
        functools.partial(_attn_kernel, planes, group, chain),
        grid=(bsz, dilation // n_res, nb // group),
        in_specs=[pl.BlockSpec(memory_space=pltpu.SMEM), tile(), tile(), tile(), single, single,
                  *extra_specs],
        out_specs=[tile(), tile(LANES)],
        out_shape=[jax.ShapeDtypeStruct(_view_shape(q.shape, planes, dilation), BF16),
                   jax.ShapeDtypeStruct(_view_shape(lse_shape, planes, dilation), F32)],
        compiler_params=pltpu.CompilerParams(
            dimension_semantics=("parallel", "parallel", "arbitrary"),
            vmem_limit_bytes=VMEM_LIMIT),
        name=f"dilated_attn_r{dilation}",
    )(bounded, view(q), view(k), view(v), view(k), view(v), *extra_args)
    return o.reshape(q.shape), lse.reshape(lse_shape)


def _view_shape(shape, planes, dilation):
    bsz, _, rows, width = shape
    return (bsz, planes, dilation, rows, width) if planes == 4 else shape


def _outproj_kernel(x_ref, mod_ref, ys_ref, o_ref, st_ref, nw_ref, e2_ref, w_ref, out_ref,
                    wide_ref, wide2_ref, acc_ref):
    tm = x_ref.shape[1]
    cls_rows = tm // CLS_STRIDE
    k_early = D_INNER // 2
    acc_ref[...] = _dot(ys_ref[0, :, 0:k_early], w_ref[0:k_early, :])

    def regrouped(cls_ref):
        return cls_ref[0].reshape(tm, cls_ref.shape[-1])

    def natural(val):
        tiles = wide_ref.shape[0]
        quarter = cls_rows * 4
        for res in range(CLS_STRIDE):
            rho, a = res % 4, res // 4
            for t in range(tiles):
                wide2_ref[t, pl.ds(rho * quarter + a, cls_rows, stride=4), :] = (
                    val[res * cls_rows:(res + 1) * cls_rows, t * LANES:(t + 1) * LANES])
        for t in range(tiles):
            for rho in range(4):
                wide_ref[t, pl.ds(rho, quarter, stride=4), :] = (
                    wide2_ref[t, rho * quarter:(rho + 1) * quarter, :])
        return jnp.concatenate([wide_ref[t] for t in range(tiles)], axis=1)

    st = regrouped(st_ref)
    head_lane = lax.broadcasted_iota(jnp.int32, st.shape, 1) < N_HEADS
    hi, lo = _split_bf16(jnp.where(head_lane, 1.0 / st, 0.0))
    inv_x = _dot(jnp.concatenate([hi, lo], axis=1), e2_ref[...])
    o = inv_x * regrouped(o_ref).astype(F32)
    ms = jnp.mean(o * o, axis=-1, keepdims=True)
    y_att = (o * lax.rsqrt(ms + EPS) * nw_ref[...]).astype(BF16)
    att = _dot(y_att, w_ref[D_INNER:, :])
    late = _dot(ys_ref[0, :, k_early:D_INNER], w_ref[k_early:D_INNER, :])
    mix = acc_ref[...] + late + natural(att)
    out_ref[0] = x_ref[0] + mod_ref[0, 2:3, :] * mix


def _out_projection(x, mod3, y_ssd, att_acc, att_stats, norm_w, e2, w_out, tm):
    bsz, s, d = x.shape
    row = lambda b, i: (b, i, 0)
    wide = pl.BlockSpec((1, tm, D_INNER), row)
    cls = lambda width: pl.BlockSpec((1, CLS_STRIDE, tm // CLS_STRIDE, width),
                                     lambda b, i: (b, 0, i, 0))
    return pl.pallas_call(
        _outproj_kernel,
        grid=(bsz, s // tm),
        in_specs=[
            pl.BlockSpec((1, tm, d), row),
            pl.BlockSpec((1, N_MOD, d), lambda b, i: (b, 0, 0)),
            wide, cls(D_INNER), cls(LANES),
            _const_spec((1, D_INNER)),
            _const_spec((2 * LANES, D_INNER)),
            _const_spec((2 * D_INNER, d)),
        ],
        out_specs=pl.BlockSpec((1, tm, d), row),
        out_shape=jax.ShapeDtypeStruct((bsz, s, d), F32),
        scratch_shapes=[pltpu.VMEM((d // LANES, tm, LANES), F32),
                        pltpu.VMEM((d // LANES, tm, LANES), F32),
                        pltpu.VMEM((tm, d), F32)],
        compiler_params=pltpu.CompilerParams(
            dimension_semantics=("parallel", "arbitrary"), vmem_limit_bytes=VMEM_LIMIT),
        name="out_proj",
    )(x, mod3, y_ssd, att_acc, att_stats, norm_w, e2, w_out)


FF_TN = 1024


def _mlp_kernel(x_ref, mod_ref, nw_ref, w1_ref, w2_ref, out_ref, h_ref):
    x = x_ref[0]
    h = _norm_mod(x, nw_ref[...], mod_ref[0, 4:5, :], mod_ref[0, 3:4, :])
    h_ref[...] = h.astype(BF16)
    acc = jnp.zeros(x.shape, F32)
    for j in range(D_FF // FF_TN):
        fs = slice(j * FF_TN, (j + 1) * FF_TN)
        u = jnp.maximum(_dot(h_ref[...], w1_ref[:, fs]), 0.0)
        acc = acc + _dot((u * u).astype(BF16), w2_ref[fs, :])
    out_ref[0] = x + mod_ref[0, 5:6, :] * acc


def _mlp(x, mod3, norm_w, w1, w2, tm):
    bsz, s, d = x.shape
    row = lambda b, i: (b, i, 0)
    return pl.pallas_call(
        _mlp_kernel,
        grid=(bsz, s // tm),
        in_specs=[
            pl.BlockSpec((1, tm, d), row),
            pl.BlockSpec((1, N_MOD, d), lambda b, i: (b, 0, 0)),
            _const_spec((1, d)),
            _const_spec((d, D_FF)),
            _const_spec((D_FF, d)),
        ],
        out_specs=pl.BlockSpec((1, tm, d), row),
        out_shape=jax.ShapeDtypeStruct((bsz, s, d), F32),
        scratch_shapes=[pltpu.VMEM((tm, d), BF16)],
        compiler_params=pltpu.CompilerParams(
            dimension_semantics=("parallel", "arbitrary"), vmem_limit_bytes=VMEM_LIMIT),
        name="mlp",
    )(x, mod3, norm_w, w1, w2)


def _head_expand_matrix():
    head_of_lane = jnp.arange(D_INNER) // HEAD_DIM
    e = (jnp.arange(LANES)[:, None] == head_of_lane[None, :]).astype(BF16)
    return jnp.concatenate([e, e], axis=0)


def _pad_lanes(v, n=LANES):
    return jnp.pad(v.astype(F32), (0, n - v.shape[0])).reshape(1, n)


def kernel(x, c, norm1_w, norm2_w, w_ada, b_ada, w_in, conv_w, conv_b, dt_bias, a_log, d_skip,
           ssd_norm_w, q_norm_w, k_norm_w, attn_norm_w, w_out, w_ff1, w_ff2):
    bsz, s, d = x.shape
    depth = w_ada.shape[0]
    tm = ROW_TILE
    assert d == D_MODEL and s % ROW_TILE == 0 and s % SSD_ROWS == 0
    assert s % (CLS_STRIDE * ATT_BLK) == 0
    e2 = _head_expand_matrix()
    idx = jnp.arange(MXU_TILE) // HEAD_DIM
    bd = (idx[:, None] == idx[None, :]).astype(BF16)
    t = (jnp.arange(SSD_CHUNK)[:, None] >= jnp.arange(SSD_CHUNK)[None, :]).astype(BF16)
    tril2 = jnp.concatenate([t, t], axis=1)
    o_xbc = D_INNER + CONV_CH
    o_dt = o_xbc + N_HEADS

    for l in range(depth):
        mod3 = _modulation(c, w_ada[l], b_ada[l]).reshape(bsz, N_MOD, d)
        wl = w_in[l]
        w_dt = jnp.pad(wl[:, o_xbc:o_dt], ((0, 0), (0, LANES - N_HEADS))).astype(BF16)
        z, xbc, dt, q16, k16, v16 = _in_projection(
            x, mod3, norm1_w[l].reshape(1, d), wl[:, :o_xbc].astype(BF16),
            wl[:, o_dt:].astype(BF16), w_dt, _pad_lanes(dt_bias[l]),
            jnp.tile(q_norm_w[l], N_HEADS).reshape(1, D_INNER),
            jnp.tile(k_norm_w[l], N_HEADS).reshape(1, D_INNER), bd, tm)
        y_ssd = _ssd_mixer(
            xbc, z, dt, conv_w[l], conv_b[l].reshape(1, CONV_CH),
            _pad_lanes(a_log[l]), jnp.repeat(d_skip[l], HEAD_DIM).reshape(1, D_INNER),
            ssd_norm_w[l].reshape(1, D_INNER), tril2, e2)
        score_bound = (HEAD_DIM ** 0.5 * LOG2E * jnp.max(jnp.abs(q_norm_w[l]))
                       * jnp.max(jnp.abs(k_norm_w[l])))
        bounded = (score_bound * BF16_SLACK <= SCORE_BOUND).astype(jnp.int32).reshape(1)
        running = None
        for window, dilation in DILATED_PATTERNS:
            assert window // dilation == ATT_BLK
            running = _dilated_attention(bounded, q16, k16, v16, dilation, e2, running)
        x = _out_projection(x, mod3, y_ssd, *running, attn_norm_w[l].reshape(1, D_INNER),
                            e2, w_out[l].astype(BF16), tm)
        x = _mlp(x, mod3, norm2_w[l].reshape(1, d), w_ff1[l].astype(BF16),
                 w_ff2[l].astype(BF16), tm)
    return x.astype(c.dtype)
```

```python
import functools

import jax
import jax.numpy as jnp
from jax import lax
from jax.experimental import pallas as pl
from jax.experimental.pallas import tpu as pltpu

F32 = jnp.float32
BF16 = jnp.bfloat16

D_MODEL = 1024
HEAD_DIM = 64
N_HEADS = 16
SSD_GROUPS = 4
SSD_STATE = 128
SSD_CONV = 4
SSD_CHUNK = 128
D_INNER = N_HEADS * HEAD_DIM
CONV_CH = D_INNER + 2 * SSD_GROUPS * SSD_STATE
D_FF = 4 * D_MODEL
N_MOD = 6
EPS = 1e-6
DILATED_PATTERNS = ((128, 1), (512, 4), (2048, 16))
ATT_BLK = 128
ATT_GROUP = 8
CLS_STRIDE = 16
ROW_TILE = 512
TAIL_ROW_TILE = 1024
LANES = 128
MXU_TILE = 256
VMEM_LIMIT = 56 * 1024 * 1024

LOG2E = 1.4426950408889634
SCORE_BOUND = 60.0
BF16_SLACK = 1.02

NT_DIMS = (((1,), (1,)), ((), ()))
TN_DIMS = (((0,), (0,)), ((), ()))


def _dot(a, b):
    return jnp.dot(a, b, preferred_element_type=F32)


def _split_bf16(v):
    hi = v.astype(BF16)
    lo = (v - hi.astype(F32)).astype(BF16)
    return hi, lo


def _silu(v):
    h = 0.5 * v
    return h + h * jnp.tanh(h)


def _const_spec(shape):
    nd = len(shape)
    return pl.BlockSpec(shape, lambda *_: (0,) * nd, pipeline_mode=pl.Buffered(1))


def _mod_kernel(c_ref, w_ref, b_ref, o_ref):
    c = c_ref[...]
    ca = _silu(c)
    c_hi, c_lo = _split_bf16(ca)
    w = w_ref[...]
    w_hi, w_lo = _split_bf16(w)
    acc = _dot(c_hi, w_hi) + _dot(c_lo, w_hi) + _dot(c_hi, w_lo)
    o_ref[...] = acc + b_ref[...]


def _modulation(c, w_ada, b_ada):
    bsz, d = c.shape
    n = w_ada.shape[1]
    tn = n // 4
    return pl.pallas_call(
        _mod_kernel,
        grid=(n // tn,),
        in_specs=[
            pl.BlockSpec((bsz, d), lambda j: (0, 0)),
            pl.BlockSpec((d, tn), lambda j: (0, j)),
            pl.BlockSpec((1, tn), lambda j: (0, j)),
        ],
        out_specs=pl.BlockSpec((bsz, tn), lambda j: (0, j)),
        out_shape=jax.ShapeDtypeStruct((bsz, n), F32),
        compiler_params=pltpu.CompilerParams(
            dimension_semantics=("arbitrary",), vmem_limit_bytes=VMEM_LIMIT),
        name="adaln_mod",
    )(c, w_ada, b_ada.reshape(1, n))


IN_TN = 512


def _norm_mod(x, nw, scale, shift):
    ms = jnp.mean(x * x, axis=-1, keepdims=True)
    return (x * lax.rsqrt(ms + EPS) * nw) * (1.0 + scale) + shift


def _inproj_kernel(x_ref, mod_ref, nw_ref, wzx_ref, wqkv_ref, wdt_ref, dtb_ref, qw_ref, kw_ref,
                   bd_ref, z_ref, xbc_ref, dt_ref, q16_ref, k16_ref, v16_ref,
                   h_ref, stage_ref, stage2_ref):
    x = x_ref[0]
    tm = x.shape[0]
    h = _norm_mod(x, nw_ref[...], mod_ref[0, 1:2, :], mod_ref[0, 0:1, :])
    h_ref[...] = h.astype(BF16)

    def proj(w_ref, c0, width):
        return _dot(h_ref[...], w_ref[:, c0:c0 + width])

    def qk_norm(acc, w):
        sq = (acc * acc).astype(BF16)
        parts = [_dot(sq[:, i:i + MXU_TILE], bd_ref[...]) for i in range(0, IN_TN, MXU_TILE)]
        ss = jnp.concatenate(parts, axis=1)
        return acc * lax.rsqrt(ss * (1.0 / HEAD_DIM) + EPS) * w

    cls_rows = x.shape[0] // CLS_STRIDE

    def emit(cls_ref, sl, val):
        tiles = IN_TN // LANES
        quarter = tm // 4
        for t in range(tiles):
            stage_ref[t] = val[:, t * LANES:(t + 1) * LANES]
        for t in range(tiles):
            for rho in range(4):
                stage2_ref[t, rho * quarter:(rho + 1) * quarter, :] = (
                    stage_ref[t, pl.ds(rho, quarter, stride=4), :])
        for res in range(CLS_STRIDE):
            rho, a = res % 4, res // 4
            rows = [stage2_ref[t, pl.ds(rho * quarter + a, cls_rows, stride=4), :]
                    for t in range(tiles)]
            cls_ref[0, res, :, sl] = jnp.concatenate(rows, axis=1).astype(BF16)

    col = 0
    for j in range(D_INNER // IN_TN):
        sl = slice(j * IN_TN, (j + 1) * IN_TN)
        emit(q16_ref, sl,
             qk_norm(proj(wqkv_ref, col, IN_TN), qw_ref[:, sl]) * (HEAD_DIM ** -0.5 * LOG2E))
        col += IN_TN
    for j in range(D_INNER // IN_TN):
        sl = slice(j * IN_TN, (j + 1) * IN_TN)
        emit(k16_ref, sl, qk_norm(proj(wqkv_ref, col, IN_TN), kw_ref[:, sl]))
        col += IN_TN
    for j in range(D_INNER // IN_TN):
        emit(v16_ref, slice(j * IN_TN, (j + 1) * IN_TN), proj(wqkv_ref, col, IN_TN))
        col += IN_TN
    dt_raw = proj(wdt_ref, 0, LANES) + dtb_ref[...]
    dt_ref[0] = jnp.maximum(dt_raw, 0.0) + jnp.log(1.0 + jnp.exp(-jnp.abs(dt_raw)))
    col = D_INNER
    for j in range(CONV_CH // IN_TN):
        xbc_ref[0, :, j * IN_TN:(j + 1) * IN_TN] = proj(wzx_ref, col, IN_TN).astype(BF16)
        col += IN_TN
    col = 0
    for j in range(D_INNER // IN_TN):
        z_ref[0, :, j * IN_TN:(j + 1) * IN_TN] = proj(wzx_ref, col, IN_TN).astype(BF16)
        col += IN_TN


def _in_projection(x, mod3, norm_w, w_zx, w_qkv, w_dt, dt_bias, qw, kw, bd, tm):
    bsz, s, d = x.shape
    row = lambda b, i: (b, i, 0)
    out_bf = lambda n: jax.ShapeDtypeStruct((bsz, s, n), BF16)
    cls_spec = pl.BlockSpec((1, CLS_STRIDE, tm // CLS_STRIDE, D_INNER), lambda b, i: (b, 0, i, 0))
    cls_shape = jax.ShapeDtypeStruct((bsz, CLS_STRIDE, s // CLS_STRIDE, D_INNER), BF16)
    return pl.pallas_call(
        _inproj_kernel,
        grid=(bsz, s // tm),
        in_specs=[
            pl.BlockSpec((1, tm, d), row),
            pl.BlockSpec((1, N_MOD, d), lambda b, i: (b, 0, 0)),
            _const_spec((1, d)),
            _const_spec(w_zx.shape),
            _const_spec(w_qkv.shape),
            _const_spec(w_dt.shape),
            _const_spec((1, LANES)),
            _const_spec((1, D_INNER)),
            _const_spec((1, D_INNER)),
            _const_spec((MXU_TILE, MXU_TILE)),
        ],
        out_specs=[
            pl.BlockSpec((1, tm, D_INNER), row),
            pl.BlockSpec((1, tm, CONV_CH), row),
            pl.BlockSpec((1, tm, LANES), row),
            cls_spec, cls_spec, cls_spec,
        ],
        out_shape=[out_bf(D_INNER), out_bf(CONV_CH),
                   jax.ShapeDtypeStruct((bsz, s, LANES), F32),
                   cls_shape, cls_shape, cls_shape],
        scratch_shapes=[pltpu.VMEM((tm, d), BF16),
                        pltpu.VMEM((IN_TN // LANES, tm, LANES), F32),
                        pltpu.VMEM((IN_TN // LANES, tm, LANES), F32)],
        compiler_params=pltpu.CompilerParams(
            dimension_semantics=("parallel", "arbitrary"), vmem_limit_bytes=VMEM_LIMIT),
        name="in_proj",
    )(x, mod3, norm_w, w_zx, w_qkv, w_dt, dt_bias, qw, kw, bd)


SSD_ROWS = 1024
CONV_HALO = 8


def _ssd_kernel(xbc_ref, z_ref, dt_ref, cw_ref, cb_ref, alog_ref, dskip_ref,
                nw_ref, tril2_ref, e2_ref, shift_ref, o_ref, halo_ref, xc_ref, state_ref):
    rows = xbc_ref.shape[1]
    L = SSD_CHUNK
    taps = SSD_CONV - 1

    @pl.when(pl.program_id(1) == 0)
    def _():
        halo_ref[...] = jnp.zeros_like(halo_ref)
        state_ref[...] = jnp.zeros_like(state_ref)

    hsub = lax.broadcasted_iota(jnp.int32, (CONV_HALO, MXU_TILE), 0)
    group_w = D_INNER // SSD_GROUPS
    heads_per_group = N_HEADS // SSD_GROUPS

    def conv_chunk(c):
        r0 = c * L
        for c0 in range(0, CONV_CH, MXU_TILE):
            cs = slice(c0, c0 + MXU_TILE)
            u = xbc_ref[0, r0:r0 + L, cs]
            shifted = _dot(shift_ref[...], u)
            uf = u.astype(F32)
            acc = cb_ref[:, cs] + cw_ref[taps:taps + 1, cs] * uf
            for k in range(1, taps + 1):
                acc = acc + cw_ref[taps - k:taps - k + 1, cs] * shifted[(k - 1) * L:k * L]
            xc_ref[r0:r0 + L, cs] = _silu(acc)
            if c == 0:
                halo = halo_ref[:, cs]
            else:
                halo = xbc_ref[0, r0 - 2 * CONV_HALO:r0, cs].astype(F32)[CONV_HALO:]
            head = acc[0:CONV_HALO]
            for k in range(1, taps + 1):
                prev = jnp.where(hsub < k, pltpu.roll(halo, k, axis=0), 0.0)
                head = head + cw_ref[taps - k:taps - k + 1, cs] * prev
            xc_ref[r0:r0 + CONV_HALO, cs] = _silu(head)
            if r0 + L == rows:
                halo_ref[:, cs] = uf[L - CONV_HALO:L]

    lane = lax.broadcasted_iota(jnp.int32, (L, LANES), 1)
    sub = lax.broadcasted_iota(jnp.int32, (L, LANES), 0)
    tril = sub >= lane
    lo_half = lane < HEAD_DIM
    a_neg = jnp.where(lane[0:1] < N_HEADS, -jnp.exp(alog_ref[...]), 0.0)

    def split_cat(v):
        hi, lo = _split_bf16(v)
        return jnp.concatenate([hi, lo], axis=1)

    def expand(v_cat, gs):
        return _dot(v_cat, e2_ref[:, gs])

    conv_chunk(0)
    for c in range(rows // L):
        if c + 1 < rows // L:
            conv_chunk(c + 1)
        r0 = c * L
        rs = slice(r0, r0 + L)
        dt = dt_ref[0, rs, :]
        d_a = dt * a_neg
        da_hi, da_lo = _split_bf16(d_a)
        a_cs = _dot(tril2_ref[...], jnp.concatenate([da_hi, da_lo], axis=0))
        a_cs_t = a_cs.T
        a_last = a_cs[L - 1:L, :]
        dt_cat = split_cat(dt)
        w_cat = split_cat(dt * jnp.exp(a_last - a_cs))
        do_cat = split_cat(jnp.exp(a_cs))

        for g in range(SSD_GROUPS):
            gs = slice(g * group_w, (g + 1) * group_w)
            xs = xc_ref[rs, gs]
            do_x = expand(do_cat, gs)
            xdt = (xs * expand(dt_cat, gs)).astype(BF16)
            xw = (xs * expand(w_cat, gs)).astype(BF16)
            b0 = D_INNER + g * SSD_STATE
            c0 = D_INNER + SSD_GROUPS * SSD_STATE + g * SSD_STATE
            bg = xc_ref[rs, b0:b0 + SSD_STATE].astype(BF16)
            cg = xc_ref[rs, c0:c0 + SSD_STATE].astype(BF16)
            cb = lax.dot_general(cg, bg, NT_DIMS, preferred_element_type=F32)
            yd_parts = []
            for pair in range(heads_per_group // 2):
                ms = []
                for e in range(2):
                    hd = g * heads_per_group + pair * 2 + e
                    colb = jnp.broadcast_to(a_cs[:, hd:hd + 1], (L, L))
                    rowb = jnp.broadcast_to(a_cs_t[hd:hd + 1, :], (L, L))
                    lmat = jnp.exp(jnp.where(tril, colb - rowb, -jnp.inf))
                    ms.append((cb * lmat).astype(BF16))
                mcat = jnp.concatenate(ms, axis=1)
                xp = xdt[:, pair * LANES:(pair + 1) * LANES]
                zero = jnp.zeros_like(xp)
                xstack = jnp.concatenate(
                    [jnp.where(lo_half, xp, zero), jnp.where(lo_half, zero, xp)], axis=0)
                yd_parts.append(_dot(mcat, xstack))
            y_diag = jnp.concatenate(yd_parts, axis=1)
            st = state_ref[:, gs]
            y_off = _dot(cg, st.astype(BF16)) * do_x
            upd = lax.dot_general(bg, xw, TN_DIMS, preferred_element_type=F32)
            state_ref[:, gs] = st * do_x[L - 1:L] + upd
            gate = _silu(z_ref[0, rs, gs].astype(F32))
            yg = (dskip_ref[:, gs] * xs + y_diag + y_off) * gate
            ss = jnp.mean(yg * yg, axis=-1, keepdims=True)
            o_ref[0, rs, gs] = (yg * lax.rsqrt(ss + EPS) * nw_ref[:, gs]).astype(BF16)


def _ssd_mixer(xbc, z, dt, conv_w, conv_b, a_log, d_skip, norm_w, tril2, e2):
    bsz, s, _ = xbc.shape
    rows = SSD_ROWS
    row = lambda b, i: (b, i, 0)
    t = jnp.arange(SSD_CHUNK)
    shift = jnp.concatenate(
        [(t[:, None] - k == t[None, :]).astype(BF16) for k in range(1, SSD_CONV)], axis=0)
    return pl.pallas_call(
        _ssd_kernel,
        grid=(bsz, s // rows),
        in_specs=[
            pl.BlockSpec((1, rows, CONV_CH), row),
            pl.BlockSpec((1, rows, D_INNER), row),
            pl.BlockSpec((1, rows, LANES), row),
            _const_spec((SSD_CONV, CONV_CH)),
            _const_spec((1, CONV_CH)),
            _const_spec((1, LANES)),
            _const_spec((1, D_INNER)),
            _const_spec((1, D_INNER)),
            _const_spec((SSD_CHUNK, 2 * SSD_CHUNK)),
            _const_spec((2 * LANES, D_INNER)),
            _const_spec(((SSD_CONV - 1) * SSD_CHUNK, SSD_CHUNK)),
        ],
        out_specs=pl.BlockSpec((1, rows, D_INNER), row),
        out_shape=jax.ShapeDtypeStruct((bsz, s, D_INNER), BF16),
        scratch_shapes=[
            pltpu.VMEM((CONV_HALO, CONV_CH), F32),
            pltpu.VMEM((rows, CONV_CH), F32),
            pltpu.VMEM((SSD_STATE, D_INNER), F32),
        ],
        compiler_params=pltpu.CompilerParams(
            dimension_semantics=("parallel", "arbitrary"), vmem_limit_bytes=VMEM_LIMIT),
        name="ssd_mixer",
    )(xbc, z, dt, conv_w, conv_b, a_log, d_skip, norm_w, tril2, e2, shift)


def _attn_kernel(planes, group, chain, bounded_ref, q_ref, k_ref, v_ref, kp_ref, vp_ref, *refs):
    if chain:
        acc_in_ref, st_in_ref, e2_ref, o_ref, lse_ref = refs
        assert planes in (1, 4)
    else:
        o_ref, lse_ref = refs
    blk = ATT_BLK
    sub = blk // planes
    first = pl.program_id(2) == 0

    def pos(i):
        return i if planes == 1 else (i % sub) * planes + i // sub

    row = lax.broadcasted_iota(jnp.int32, (blk, 2 * blk), 0)
    key = lax.broadcasted_iota(jnp.int32, (blk, 2 * blk), 1)
    rel = jnp.where(key >= blk, blk + pos(key - blk), pos(key)) - pos(row)
    band = (rel >= 0) & (rel <= blk)
    lo_key = jnp.where(first, blk, 0)
    bias_head = jnp.where(band & (key >= lo_key), 0.0, -jnp.inf)
    bias_head = jnp.concatenate([bias_head, bias_head], axis=0)
    bias_rest = jnp.where(band, 0.0, -jnp.inf)
    bias_rest = jnp.concatenate([bias_rest, bias_rest], axis=0)
    lane = lax.broadcasted_iota(jnp.int32, (blk, LANES), 1)
    lo_half = lane < HEAD_DIM

    n_res = q_ref.shape[1] if planes == 1 else 1

    half = 2 * sub

    def load(ref, r, g, ps):
        if planes == 1:
            return ref[0, r, g * blk:(g + 1) * blk, ps]
        if planes == 4:
            return ref[0, :, 0, g * sub:(g + 1) * sub, ps].reshape(blk, ps.stop - ps.start)
        rows = ref[0, :, (g // 2) * half:(g // 2 + 1) * half, ps].astype(F32)
        return rows[:, (g % 2) * sub:(g % 2 + 1) * sub, :].reshape(blk, LANES).astype(BF16)

    def store_cols(ref, r, g, ps, val):
        if planes == 1:
            ref[0, r, g * blk:(g + 1) * blk, ps] = val
        elif planes == 4:
            ref[0, :, 0, g * sub:(g + 1) * sub, ps] = val.reshape(planes, sub, val.shape[-1])
        else:
            ref[0, :, g * sub:(g + 1) * sub, ps] = val.reshape(planes, sub, val.shape[-1])

    pending = {}

    def store_out(r, g, ps, val):
        if planes == 1:
            o_ref[0, r, g * blk:(g + 1) * blk, ps] = val.astype(BF16)
        elif planes == 4:
            o_ref[0, :, 0, g * sub:(g + 1) * sub, ps] = (
                val.reshape(planes, sub, val.shape[-1]).astype(BF16))
        elif g % 2 == 0:
            pending[ps.start] = val.reshape(planes, sub, LANES)
        else:
            both = jnp.concatenate(
                [pending.pop(ps.start), val.reshape(planes, sub, LANES)], axis=1)
            o_ref[0, :, (g // 2) * half:(g // 2 + 1) * half, ps] = both.astype(BF16)

    def pair(bounded, r, g, p):
        ps = slice(p * LANES, (p + 1) * LANES)
        qp = load(q_ref, r, g, ps)
        zero = jnp.zeros_like(qp)
        qs = jnp.concatenate(
            [jnp.where(lo_half, qp, zero), jnp.where(lo_half, zero, qp)], axis=0)
        if g == 0:
            last = kp_ref.shape[-2] // sub - 1
            k_prev, v_prev = load(kp_ref, r, last, ps), load(vp_ref, r, last, ps)
        else:
            k_prev, v_prev = load(k_ref, r, g - 1, ps), load(v_ref, r, g - 1, ps)
        k2 = jnp.concatenate([k_prev, load(k_ref, r, g, ps)], axis=0)
        v2 = jnp.concatenate([v_prev, load(v_ref, r, g, ps)], axis=0)
        s = lax.dot_general(qs, k2, NT_DIMS, preferred_element_type=F32)
        s = s + (bias_head if g == 0 else bias_rest)
        if bounded:
            e = jnp.exp2(s)
        else:
            m = jnp.max(s, axis=-1, keepdims=True)
            e = jnp.exp2(s - m)
        l = jnp.sum(e, axis=-1, keepdims=True)
        pv = _dot(e.astype(BF16), v2)
        out = jnp.where(lo_half, pv[0:blk], pv[blk:])
        if chain and bounded:
            out = out + load(acc_in_ref, r, g, ps).astype(F32)
        store_out(r, g, ps, out)
        store_cols(lse_ref, r, g, slice(2 * p, 2 * p + 1), l[0:blk])
        store_cols(lse_ref, r, g, slice(2 * p + 1, 2 * p + 2), l[blk:])
        if not bounded:
            c0 = N_HEADS + 2 * p
            store_cols(lse_ref, r, g, slice(c0, c0 + 1), m[0:blk])
            store_cols(lse_ref, r, g, slice(c0 + 1, c0 + 2), m[blk:])

    def merge_block(r, g):
        everything = slice(0, LANES)
        head_lane = lane < N_HEADS
        s_own, s_prev = load(lse_ref, r, g, everything), load(st_in_ref, r, g, everything)
        m_own = pltpu.roll(s_own, LANES - N_HEADS, axis=1)
        m_prev = pltpu.roll(s_prev, LANES - N_HEADS, axis=1)
        m_new = jnp.maximum(m_prev, m_own)
        a = jnp.where(head_lane, jnp.exp2(m_prev - m_new), 0.0)
        b = jnp.where(head_lane, jnp.exp2(m_own - m_new), 0.0)
        stats = (jnp.where(head_lane, a * s_prev + b * s_own, 0.0)
                 + pltpu.roll(jnp.where(head_lane, m_new, 0.0), N_HEADS, axis=1))
        store_cols(lse_ref, r, g, everything, stats)

        def expand(v):
            hi, lo = _split_bf16(v)
            return _dot(jnp.concatenate([hi, lo], axis=1), e2_ref[...])

        wide = slice(0, o_ref.shape[-1])
        merged = (expand(a) * load(acc_in_ref, r, g, wide).astype(F32)
                  + expand(b) * load(o_ref, r, g, wide).astype(F32))
        store_out(r, g, wide, merged)

    def body(bounded):
        lse_ref[...] = jnp.zeros(lse_ref.shape, F32)
        for r in range(n_res):
            if planes == 16:
                order = [(g0 + h, p) for g0 in range(0, group, 2)
                         for p in range(N_HEADS // 2) for h in range(2)]
            else:
                order = [(g, p) for g in range(group) for p in range(N_HEADS // 2)]
            for g, p in order:
                pair(bounded, r, g, p)
                if chain and not bounded and p == N_HEADS // 2 - 1:
                    merge_block(r, g)
        if chain and bounded:
            lse_ref[...] = lse_ref[...] + st_in_ref[...]

    bounded = bounded_ref[0] != 0

    @pl.when(bounded)
    def _():
        body(True)

    @pl.when(jnp.logical_not(bounded))
    def _():
        body(False)


def _dilated_attention(bounded, q, k, v, dilation, e2, running=None):
    bsz, n_planes, rows, w = q.shape
    planes = n_planes // dilation
    assert planes in (1, 4, 16)
    sub = ATT_BLK // planes
    nb = rows // sub
    group = min(ATT_GROUP, nb)
    assert rows % sub == 0 and nb % group == 0
    n_res = 1
    window = 1
    if planes == 1:
        n_res = min(ATT_GROUP // group, dilation)
        view = lambda t: t
        blk_shape = lambda n, width: (1, n_res, n * ATT_BLK, width)
        cur = lambda b, res, j: (b, res, j, 0)
        prev = lambda b, res, j: (b, res, jnp.maximum(j * group - 1, 0), 0)
    elif planes == 4:
        view = lambda t: t.reshape(bsz, planes, dilation, rows, t.shape[-1])
        blk_shape = lambda n, width: (1, planes, 1, n * sub, width)
        cur = lambda b, res, j: (b, 0, res, j, 0)
        prev = lambda b, res, j: (b, 0, res, jnp.maximum(j * group - 1, 0), 0)
    else:
        assert group % 2 == 0
        window = 2
        view = lambda t: t
        blk_shape = lambda n, width: (1, planes, n * sub, width)
        cur = lambda b, res, j: (b, 0, j, 0)
        prev = lambda b, res, j: (b, 0, jnp.maximum(j * (group // 2) - 1, 0), 0)
    tile = lambda width=w: pl.BlockSpec(blk_shape(group, width), cur)
    single = pl.BlockSpec(blk_shape(window, w), prev)
    lse_shape = (bsz, n_planes, rows, LANES)
    chain = running is not None
    extra_specs = [tile(), tile(LANES), _const_spec(e2.shape)] if chain else []
    extra_args = [view(running[0]), view(running[1]), e2] if chain else []
    o, lse = pl.pallas_call(
        functools.partial(_attn_kernel, planes, group, chain),
        grid=(bsz, dilation // n_res, nb // group),
        in_specs=[pl.BlockSpec(memory_space=pltpu.SMEM), tile(), tile(), tile(), single, single,
                  *extra_specs],
        out_specs=[tile(), tile(LANES)],
        out_shape=[jax.ShapeDtypeStruct(_view_shape(q.shape, planes, dilation), BF16),
                   jax.ShapeDtypeStruct(_view_shape(lse_shape, planes, dilation), F32)],
        compiler_params=pltpu.CompilerParams(
            dimension_semantics=("parallel", "parallel", "arbitrary"),
            vmem_limit_bytes=VMEM_LIMIT),
        name=f"dilated_attn_r{dilation}",
    )(bounded, view(q), view(k), view(v), view(k), view(v), *extra_args)
    return o.reshape(q.shape), lse.reshape(lse_shape)


def _view_shape(shape, planes, dilation):
    bsz, _, rows, width = shape
    return (bsz, planes, dilation, rows, width) if planes == 4 else shape


def _outproj_kernel(x_ref, mod_ref, ys_ref, o_ref, st_ref, nw_ref, e2_ref, w_ref, out_ref,
                    wide_ref, wide2_ref, acc_ref):
    tm = x_ref.shape[1]
    cls_rows = tm // CLS_STRIDE
    k_early = D_INNER // 2
    acc_ref[...] = _dot(ys_ref[0, :, 0:k_early], w_ref[0:k_early, :])

    def regrouped(cls_ref):
        return cls_ref[0].reshape(tm, cls_ref.shape[-1])

    def natural(val):
        tiles = wide_ref.shape[0]
        quarter = cls_rows * 4
        for res in range(CLS_STRIDE):
            rho, a = res % 4, res // 4
            for t in range(tiles):
                wide2_ref[t, pl.ds(rho * quarter + a, cls_rows, stride=4), :] = (
                    val[res * cls_rows:(res + 1) * cls_rows, t * LANES:(t + 1) * LANES])
        for t in range(tiles):
            for rho in range(4):
                wide_ref[t, pl.ds(rho, quarter, stride=4), :] = (
                    wide2_ref[t, rho * quarter:(rho + 1) * quarter, :])
        return jnp.concatenate([wide_ref[t] for t in range(tiles)], axis=1)

    st = regrouped(st_ref)
    head_lane = lax.broadcasted_iota(jnp.int32, st.shape, 1) < N_HEADS
    hi, lo = _split_bf16(jnp.where(head_lane, 1.0 / st, 0.0))
    inv_x = _dot(jnp.concatenate([hi, lo], axis=1), e2_ref[...])
    o = inv_x * regrouped(o_ref).astype(F32)
    ms = jnp.mean(o * o, axis=-1, keepdims=True)
    y_att = (o * lax.rsqrt(ms + EPS) * nw_ref[...]).astype(BF16)
    att = _dot(y_att, w_ref[D_INNER:, :])
    late = _dot(ys_ref[0, :, k_early:D_INNER], w_ref[k_early:D_INNER, :])
    mix = acc_ref[...] + late + natural(att)
    out_ref[0] = x_ref[0] + mod_ref[0, 2:3, :] * mix


def _out_projection(x, mod3, y_ssd, att_acc, att_stats, norm_w, e2, w_out, tm):
    bsz, s, d = x.shape
    row = lambda b, i: (b, i, 0)
    wide = pl.BlockSpec((1, tm, D_INNER), row)
    cls = lambda width: pl.BlockSpec((1, CLS_STRIDE, tm // CLS_STRIDE, width),
                                     lambda b, i: (b, 0, i, 0))
    return pl.pallas_call(
        _outproj_kernel,
        grid=(bsz, s // tm),
        in_specs=[
            pl.BlockSpec((1, tm, d), row),
            pl.BlockSpec((1, N_MOD, d), lambda b, i: (b, 0, 0)),
            wide, cls(D_INNER), cls(LANES),
            _const_spec((1, D_INNER)),
            _const_spec((2 * LANES, D_INNER)),
            _const_spec((2 * D_INNER, d)),
        ],
        out_specs=pl.BlockSpec((1, tm, d), row),
        out_shape=jax.ShapeDtypeStruct((bsz, s, d), F32),
        scratch_shapes=[pltpu.VMEM((d // LANES, tm, LANES), F32),
                        pltpu.VMEM((d // LANES, tm, LANES), F32),
                        pltpu.VMEM((tm, d), F32)],
        compiler_params=pltpu.CompilerParams(
            dimension_semantics=("parallel", "arbitrary"), vmem_limit_bytes=VMEM_LIMIT),
        name="out_proj",
    )(x, mod3, y_ssd, att_acc, att_stats, norm_w, e2, w_out)


FF_TN = 1024


def _mlp_kernel(x_ref, mod_ref, nw_ref, w1_ref, w2_ref, out_ref, h_ref):
    x = x_ref[0]
    h = _norm_mod(x, nw_ref[...], mod_ref[0, 4:5, :], mod_ref[0, 3:4, :])
    h_ref[...] = h.astype(BF16)
    acc = jnp.zeros(x.shape, F32)
    for j in range(D_FF // FF_TN):
        fs = slice(j * FF_TN, (j + 1) * FF_TN)
        u = jnp.maximum(_dot(h_ref[...], w1_ref[:, fs]), 0.0)
        acc = acc + _dot((u * u).astype(BF16), w2_ref[fs, :])
    out_ref[0] = x + mod_ref[0, 5:6, :] * acc


def _mlp(x, mod3, norm_w, w1, w2, tm):
    bsz, s, d = x.shape
    row = lambda b, i: (b, i, 0)
    return pl.pallas_call(
        _mlp_kernel,
        grid=(bsz, s // tm),
        in_specs=[
            pl.BlockSpec((1, tm, d), row),
            pl.BlockSpec((1, N_MOD, d), lambda b, i: (b, 0, 0)),
            _const_spec((1, d)),
            _const_spec((d, D_FF)),
            _const_spec((D_FF, d)),
        ],
        out_specs=pl.BlockSpec((1, tm, d), row),
        out_shape=jax.ShapeDtypeStruct((bsz, s, d), F32),
        scratch_shapes=[pltpu.VMEM((tm, d), BF16)],
        compiler_params=pltpu.CompilerParams(
            dimension_semantics=("parallel", "arbitrary"), vmem_limit_bytes=VMEM_LIMIT),
        name="mlp",
    )(x, mod3, norm_w, w1, w2)


def _head_expand_matrix():
    head_of_lane = jnp.arange(D_INNER) // HEAD_DIM
    e = (jnp.arange(LANES)[:, None] == head_of_lane[None, :]).astype(BF16)
    return jnp.concatenate([e, e], axis=0)


def _pad_lanes(v, n=LANES):
    return jnp.pad(v.astype(F32), (0, n - v.shape[0])).reshape(1, n)


def kernel(x, c, norm1_w, norm2_w, w_ada, b_ada, w_in, conv_w, conv_b, dt_bias, a_log, d_skip,
           ssd_norm_w, q_norm_w, k_norm_w, attn_norm_w, w_out, w_ff1, w_ff2):
    bsz, s, d = x.shape
    depth = w_ada.shape[0]
    tm = ROW_TILE
    assert d == D_MODEL and s % ROW_TILE == 0 and s % TAIL_ROW_TILE == 0 and s % SSD_ROWS == 0
    assert s % (CLS_STRIDE * ATT_BLK) == 0
    e2 = _head_expand_matrix()
    idx = jnp.arange(MXU_TILE) // HEAD_DIM
    bd = (idx[:, None] == idx[None, :]).astype(BF16)
    t = (jnp.arange(SSD_CHUNK)[:, None] >= jnp.arange(SSD_CHUNK)[None, :]).astype(BF16)
    tril2 = jnp.concatenate([t, t], axis=1)
    o_xbc = D_INNER + CONV_CH
    o_dt = o_xbc + N_HEADS

    for l in range(depth):
        mod3 = _modulation(c, w_ada[l], b_ada[l]).reshape(bsz, N_MOD, d)
        wl = w_in[l]
        w_dt = jnp.pad(wl[:, o_xbc:o_dt], ((0, 0), (0, LANES - N_HEADS))).astype(BF16)
        z, xbc, dt, q16, k16, v16 = _in_projection(
            x, mod3, norm1_w[l].reshape(1, d), wl[:, :o_xbc].astype(BF16),
            wl[:, o_dt:].astype(BF16), w_dt, _pad_lanes(dt_bias[l]),
            jnp.tile(q_norm_w[l], N_HEADS).reshape(1, D_INNER),
            jnp.tile(k_norm_w[l], N_HEADS).reshape(1, D_INNER), bd, tm)
        y_ssd = _ssd_mixer(
            xbc, z, dt, conv_w[l], conv_b[l].reshape(1, CONV_CH),
            _pad_lanes(a_log[l]), jnp.repeat(d_skip[l], HEAD_DIM).reshape(1, D_INNER),
            ssd_norm_w[l].reshape(1, D_INNER), tril2, e2)
        score_bound = (HEAD_DIM ** 0.5 * LOG2E * jnp.max(jnp.abs(q_norm_w[l]))
                       * jnp.max(jnp.abs(k_norm_w[l])))
        bounded = (score_bound * BF16_SLACK <= SCORE_BOUND).astype(jnp.int32).reshape(1)
        running = None
        for window, dilation in DILATED_PATTERNS:
            assert window // dilation == ATT_BLK
            running = _dilated_attention(bounded, q16, k16, v16, dilation, e2, running)
        x = _out_projection(x, mod3, y_ssd, *running, attn_norm_w[l].reshape(1, D_INNER),
                            e2, w_out[l].astype(BF16), TAIL_ROW_TILE)
        x = _mlp(x, mod3, norm2_w[l].reshape(1, d), w_ff1[l].astype(BF16),
                 w_ff2[l].astype(BF16), TAIL_ROW_TILE)
    return x.astype(c.dtype)
```

```python
import functools

import jax
import jax.numpy as jnp
from jax import lax
from jax.experimental import pallas as pl
from jax.experimental.pallas import tpu as pltpu

F32 = jnp.float32
BF16 = jnp.bfloat16

D_MODEL = 1024
HEAD_DIM = 64
N_HEADS = 16
SSD_GROUPS = 4
SSD_STATE = 128
SSD_CONV = 4
SSD_CHUNK = 128
D_INNER = N_HEADS * HEAD_DIM
CONV_CH = D_INNER + 2 * SSD_GROUPS * SSD_STATE
D_FF = 4 * D_MODEL
N_MOD = 6
EPS = 1e-6
DILATED_PATTERNS = ((128, 1), (512, 4), (2048, 16))
ATT_BLK = 128
ATT_GROUP = 8
CLS_STRIDE = 16
ROW_TILE = 512
TAIL_ROW_TILE = 1024
LANES = 128
MXU_TILE = 256
VMEM_LIMIT = 56 * 1024 * 1024

LOG2E = 1.4426950408889634
SCORE_BOUND = 60.0
BF16_SLACK = 1.02

NT_DIMS = (((1,), (1,)), ((), ()))
TN_DIMS = (((0,), (0,)), ((), ()))


def _dot(a, b):
    return jnp.dot(a, b, preferred_element_type=F32)


def _split_bf16(v):
    hi = v.astype(BF16)
    lo = (v - hi.astype(F32)).astype(BF16)
    return hi, lo


def _silu(v):
    h = 0.5 * v
    return h + h * jnp.tanh(h)


def _const_spec(shape):
    nd = len(shape)
    return pl.BlockSpec(shape, lambda *_: (0,) * nd, pipeline_mode=pl.Buffered(1))


def _mod_kernel(c_ref, w_ref, b_ref, o_ref):
    c = c_ref[...]
    ca = _silu(c)
    c_hi, c_lo = _split_bf16(ca)
    w = w_ref[...]
    w_hi, w_lo = _split_bf16(w)
    acc = _dot(c_hi, w_hi) + _dot(c_lo, w_hi) + _dot(c_hi, w_lo)
    o_ref[...] = acc + b_ref[...]


def _modulation(c, w_ada, b_ada):
    bsz, d = c.shape
    n = w_ada.shape[1]
    tn = n // 4
    return pl.pallas_call(
        _mod_kernel,
        grid=(n // tn,),
        in_specs=[
            pl.BlockSpec((bsz, d), lambda j: (0, 0)),
            pl.BlockSpec((d, tn), lambda j: (0, j)),
            pl.BlockSpec((1, tn), lambda j: (0, j)),
        ],
        out_specs=pl.BlockSpec((bsz, tn), lambda j: (0, j)),
        out_shape=jax.ShapeDtypeStruct((bsz, n), F32),
        compiler_params=pltpu.CompilerParams(
            dimension_semantics=("arbitrary",), vmem_limit_bytes=VMEM_LIMIT),
        name="adaln_mod",
    )(c, w_ada, b_ada.reshape(1, n))


IN_TN = 512


def _norm_mod(x, nw, scale, shift):
    ms = jnp.mean(x * x, axis=-1, keepdims=True)
    return (x * lax.rsqrt(ms + EPS) * nw) * (1.0 + scale) + shift


def _inproj_kernel(x_ref, mod_ref, nw_ref, wzx_ref, wqkv_ref, wdt_ref, dtb_ref, qw_ref, kw_ref,
                   bd_ref, z_ref, xbc_ref, dt_ref, q16_ref, k16_ref, v16_ref,
                   h_ref, h16_ref, stage_ref, stage2_ref):
    x = x_ref[0]
    tm = x.shape[0]
    cls_rows = tm // CLS_STRIDE
    h = _norm_mod(x, nw_ref[...], mod_ref[0, 1:2, :], mod_ref[0, 0:1, :])
    h_ref[...] = h.astype(BF16)

    tiles = h.shape[1] // LANES
    quarter = tm // 4
    for t in range(tiles):
        stage_ref[t] = h[:, t * LANES:(t + 1) * LANES]
    for t in range(tiles):
        for rho in range(4):
            stage2_ref[t, rho * quarter:(rho + 1) * quarter, :] = (
                stage_ref[t, pl.ds(rho, quarter, stride=4), :])
    for res in range(CLS_STRIDE):
        rho, a = res % 4, res // 4
        rows = [stage2_ref[t, pl.ds(rho * quarter + a, cls_rows, stride=4), :]
                for t in range(tiles)]
        h16_ref[res * cls_rows:(res + 1) * cls_rows, :] = (
            jnp.concatenate(rows, axis=1).astype(BF16))

    def proj(w_ref, c0, width, lhs_ref=h_ref):
        return _dot(lhs_ref[...], w_ref[:, c0:c0 + width])

    def qk_norm(acc, w):
        sq = (acc * acc).astype(BF16)
        parts = [_dot(sq[:, i:i + MXU_TILE], bd_ref[...]) for i in range(0, IN_TN, MXU_TILE)]
        ss = jnp.concatenate(parts, axis=1)
        return acc * lax.rsqrt(ss * (1.0 / HEAD_DIM) + EPS) * w

    def emit(cls_ref, sl, val):
        for res in range(CLS_STRIDE):
            cls_ref[0, res, :, sl] = val[res * cls_rows:(res + 1) * cls_rows].astype(BF16)

    col = 0
    for j in range(D_INNER // IN_TN):
        sl = slice(j * IN_TN, (j + 1) * IN_TN)
        emit(q16_ref, sl, qk_norm(proj(wqkv_ref, col, IN_TN, h16_ref), qw_ref[:, sl])
             * (HEAD_DIM ** -0.5 * LOG2E))
        col += IN_TN
    for j in range(D_INNER // IN_TN):
        sl = slice(j * IN_TN, (j + 1) * IN_TN)
        emit(k16_ref, sl, qk_norm(proj(wqkv_ref, col, IN_TN, h16_ref), kw_ref[:, sl]))
        col += IN_TN
    for j in range(D_INNER // IN_TN):
        emit(v16_ref, slice(j * IN_TN, (j + 1) * IN_TN), proj(wqkv_ref, col, IN_TN, h16_ref))
        col += IN_TN
    dt_raw = proj(wdt_ref, 0, LANES) + dtb_ref[...]
    dt_ref[0] = jnp.maximum(dt_raw, 0.0) + jnp.log(1.0 + jnp.exp(-jnp.abs(dt_raw)))
    col = D_INNER
    for j in range(CONV_CH // IN_TN):
        xbc_ref[0, :, j * IN_TN:(j + 1) * IN_TN] = proj(wzx_ref, col, IN_TN).astype(BF16)
        col += IN_TN
    col = 0
    for j in range(D_INNER // IN_TN):
        z_ref[0, :, j * IN_TN:(j + 1) * IN_TN] = proj(wzx_ref, col, IN_TN).astype(BF16)
        col += IN_TN


def _in_projection(x, mod3, norm_w, w_zx, w_qkv, w_dt, dt_bias, qw, kw, bd, tm):
    bsz, s, d = x.shape
    row = lambda b, i: (b, i, 0)
    out_bf = lambda n: jax.ShapeDtypeStruct((bsz, s, n), BF16)
    cls_spec = pl.BlockSpec((1, CLS_STRIDE, tm // CLS_STRIDE, D_INNER), lambda b, i: (b, 0, i, 0))
    cls_shape = jax.ShapeDtypeStruct((bsz, CLS_STRIDE, s // CLS_STRIDE, D_INNER), BF16)
    return pl.pallas_call(
        _inproj_kernel,
        grid=(bsz, s // tm),
        in_specs=[
            pl.BlockSpec((1, tm, d), row),
            pl.BlockSpec((1, N_MOD, d), lambda b, i: (b, 0, 0)),
            _const_spec((1, d)),
            _const_spec(w_zx.shape),
            _const_spec(w_qkv.shape),
            _const_spec(w_dt.shape),
            _const_spec((1, LANES)),
            _const_spec((1, D_INNER)),
            _const_spec((1, D_INNER)),
            _const_spec((MXU_TILE, MXU_TILE)),
        ],
        out_specs=[
            pl.BlockSpec((1, tm, D_INNER), row),
            pl.BlockSpec((1, tm, CONV_CH), row),
            pl.BlockSpec((1, tm, LANES), row),
            cls_spec, cls_spec, cls_spec,
        ],
        out_shape=[out_bf(D_INNER), out_bf(CONV_CH),
                   jax.ShapeDtypeStruct((bsz, s, LANES), F32),
                   cls_shape, cls_shape, cls_shape],
        scratch_shapes=[pltpu.VMEM((tm, d), BF16),
                        pltpu.VMEM((tm, d), BF16),
                        pltpu.VMEM((d // LANES, tm, LANES), F32),
                        pltpu.VMEM((d // LANES, tm, LANES), F32)],
        compiler_params=pltpu.CompilerParams(
            dimension_semantics=("parallel", "arbitrary"), vmem_limit_bytes=VMEM_LIMIT),
        name="in_proj",
    )(x, mod3, norm_w, w_zx, w_qkv, w_dt, dt_bias, qw, kw, bd)


SSD_ROWS = 1024
CONV_HALO = 8


def _ssd_kernel(xbc_ref, z_ref, dt_ref, cw_ref, cb_ref, alog_ref, dskip_ref,
                nw_ref, tril2_ref, e2_ref, shift_ref, o_ref, halo_ref, xc_ref, state_ref):
    rows = xbc_ref.shape[1]
    L = SSD_CHUNK
    taps = SSD_CONV - 1

    @pl.when(pl.program_id(1) == 0)
    def _():
        halo_ref[...] = jnp.zeros_like(halo_ref)
        state_ref[...] = jnp.zeros_like(state_ref)

    hsub = lax.broadcasted_iota(jnp.int32, (CONV_HALO, MXU_TILE), 0)
    group_w = D_INNER // SSD_GROUPS
    heads_per_group = N_HEADS // SSD_GROUPS

    def conv_chunk(c):
        r0 = c * L
        for c0 in range(0, CONV_CH, MXU_TILE):
            cs = slice(c0, c0 + MXU_TILE)
            u = xbc_ref[0, r0:r0 + L, cs]
            shifted = _dot(shift_ref[...], u)
            uf = u.astype(F32)
            acc = cb_ref[:, cs] + cw_ref[taps:taps + 1, cs] * uf
            for k in range(1, taps + 1):
                acc = acc + cw_ref[taps - k:taps - k + 1, cs] * shifted[(k - 1) * L:k * L]
            xc_ref[r0:r0 + L, cs] = _silu(acc)
            if c == 0:
                halo = halo_ref[:, cs]
            else:
                halo = xbc_ref[0, r0 - 2 * CONV_HALO:r0, cs].astype(F32)[CONV_HALO:]
            head = acc[0:CONV_HALO]
            for k in range(1, taps + 1):
                prev = jnp.where(hsub < k, pltpu.roll(halo, k, axis=0), 0.0)
                head = head + cw_ref[taps - k:taps - k + 1, cs] * prev
            xc_ref[r0:r0 + CONV_HALO, cs] = _silu(head)
            if r0 + L == rows:
                halo_ref[:, cs] = uf[L - CONV_HALO:L]

    lane = lax.broadcasted_iota(jnp.int32, (L, LANES), 1)
    sub = lax.broadcasted_iota(jnp.int32, (L, LANES), 0)
    tril = sub >= lane
    lo_half = lane < HEAD_DIM
    a_neg = jnp.where(lane[0:1] < N_HEADS, -jnp.exp(alog_ref[...]), 0.0)

    def split_cat(v):
        hi, lo = _split_bf16(v)
        return jnp.concatenate([hi, lo], axis=1)

    def expand(v_cat, gs):
        return _dot(v_cat, e2_ref[:, gs])

    conv_chunk(0)
    for c in range(rows // L):
        if c + 1 < rows // L:
            conv_chunk(c + 1)
        r0 = c * L
        rs = slice(r0, r0 + L)
        dt = dt_ref[0, rs, :]
        d_a = dt * a_neg
        da_hi, da_lo = _split_bf16(d_a)
        a_cs = _dot(tril2_ref[...], jnp.concatenate([da_hi, da_lo], axis=0))
        a_cs_t = a_cs.T
        a_last = a_cs[L - 1:L, :]
        dt_cat = split_cat(dt)
        w_cat = split_cat(dt * jnp.exp(a_last - a_cs))
        do_cat = split_cat(jnp.exp(a_cs))

        for g in range(SSD_GROUPS):
            gs = slice(g * group_w, (g + 1) * group_w)
            xs = xc_ref[rs, gs]
            do_x = expand(do_cat, gs)
            xdt = (xs * expand(dt_cat, gs)).astype(BF16)
            xw = (xs * expand(w_cat, gs)).astype(BF16)
            b0 = D_INNER + g * SSD_STATE
            c0 = D_INNER + SSD_GROUPS * SSD_STATE + g * SSD_STATE
            bg = xc_ref[rs, b0:b0 + SSD_STATE].astype(BF16)
            cg = xc_ref[rs, c0:c0 + SSD_STATE].astype(BF16)
            cb = lax.dot_general(cg, bg, NT_DIMS, preferred_element_type=F32)
            yd_parts = []
            for pair in range(heads_per_group // 2):
                ms = []
                for e in range(2):
                    hd = g * heads_per_group + pair * 2 + e
                    colb = jnp.broadcast_to(a_cs[:, hd:hd + 1], (L, L))
                    rowb = jnp.broadcast_to(a_cs_t[hd:hd + 1, :], (L, L))
                    lmat = jnp.exp(jnp.where(tril, colb - rowb, -jnp.inf))
                    ms.append((cb * lmat).astype(BF16))
                mcat = jnp.concatenate(ms, axis=1)
                xp = xdt[:, pair * LANES:(pair + 1) * LANES]
                zero = jnp.zeros_like(xp)
                xstack = jnp.concatenate(
                    [jnp.where(lo_half, xp, zero), jnp.where(lo_half, zero, xp)], axis=0)
                yd_parts.append(_dot(mcat, xstack))
            y_diag = jnp.concatenate(yd_parts, axis=1)
            st = state_ref[:, gs]
            y_off = _dot(cg, st.astype(BF16)) * do_x
            upd = lax.dot_general(bg, xw, TN_DIMS, preferred_element_type=F32)
            state_ref[:, gs] = st * do_x[L - 1:L] + upd
            gate = _silu(z_ref[0, rs, gs].astype(F32))
            yg = (dskip_ref[:, gs] * xs + y_diag + y_off) * gate
            ss = jnp.mean(yg * yg, axis=-1, keepdims=True)
            o_ref[0, rs, gs] = (yg * lax.rsqrt(ss + EPS) * nw_ref[:, gs]).astype(BF16)


def _ssd_mixer(xbc, z, dt, conv_w, conv_b, a_log, d_skip, norm_w, tril2, e2):
    bsz, s, _ = xbc.shape
    rows = SSD_ROWS
    row = lambda b, i: (b, i, 0)
    t = jnp.arange(SSD_CHUNK)
    shift = jnp.concatenate(
        [(t[:, None] - k == t[None, :]).astype(BF16) for k in range(1, SSD_CONV)], axis=0)
    return pl.pallas_call(
        _ssd_kernel,
        grid=(bsz, s // rows),
        in_specs=[
            pl.BlockSpec((1, rows, CONV_CH), row),
            pl.BlockSpec((1, rows, D_INNER), row),
            pl.BlockSpec((1, rows, LANES), row),
            _const_spec((SSD_CONV, CONV_CH)),
            _const_spec((1, CONV_CH)),
            _const_spec((1, LANES)),
            _const_spec((1, D_INNER)),
            _const_spec((1, D_INNER)),
            _const_spec((SSD_CHUNK, 2 * SSD_CHUNK)),
            _const_spec((2 * LANES, D_INNER)),
            _const_spec(((SSD_CONV - 1) * SSD_CHUNK, SSD_CHUNK)),
        ],
        out_specs=pl.BlockSpec((1, rows, D_INNER), row),
        out_shape=jax.ShapeDtypeStruct((bsz, s, D_INNER), BF16),
        scratch_shapes=[
            pltpu.VMEM((CONV_HALO, CONV_CH), F32),
            pltpu.VMEM((rows, CONV_CH), F32),
            pltpu.VMEM((SSD_STATE, D_INNER), F32),
        ],
        compiler_params=pltpu.CompilerParams(
            dimension_semantics=("parallel", "arbitrary"), vmem_limit_bytes=VMEM_LIMIT),
        name="ssd_mixer",
    )(xbc, z, dt, conv_w, conv_b, a_log, d_skip, norm_w, tril2, e2, shift)


def _attn_kernel(planes, group, chain, bounded_ref, q_ref, k_ref, v_ref, kp_ref, vp_ref, *refs):
    if chain:
        acc_in_ref, st_in_ref, e2_ref, o_ref, lse_ref = refs
        assert planes in (1, 4)
    else:
        o_ref, lse_ref = refs
    blk = ATT_BLK
    sub = blk // planes
    first = pl.program_id(2) == 0

    def pos(i):
        return i if planes == 1 else (i % sub) * planes + i // sub

    row = lax.broadcasted_iota(jnp.int32, (blk, 2 * blk), 0)
    key = lax.broadcasted_iota(jnp.int32, (blk, 2 * blk), 1)
    rel = jnp.where(key >= blk, blk + pos(key - blk), pos(key)) - pos(row)
    band = (rel >= 0) & (rel <= blk)
    lo_key = jnp.where(first, blk, 0)
    bias_head = jnp.where(band & (key >= lo_key), 0.0, -jnp.inf)
    bias_head = jnp.concatenate([bias_head, bias_head], axis=0)
    bias_rest = jnp.where(band, 0.0, -jnp.inf)
    bias_rest = jnp.concatenate([bias_rest, bias_rest], axis=0)
    lane = lax.broadcasted_iota(jnp.int32, (blk, LANES), 1)
    lo_half = lane < HEAD_DIM

    n_res = q_ref.shape[1] if planes == 1 else 1

    half = 2 * sub

    def load(ref, r, g, ps):
        if planes == 1:
            return ref[0, r, g * blk:(g + 1) * blk, ps]
        if planes == 4:
            return ref[0, :, 0, g * sub:(g + 1) * sub, ps].reshape(blk, ps.stop - ps.start)
        rows = ref[0, :, (g // 2) * half:(g // 2 + 1) * half, ps].astype(F32)
        return rows[:, (g % 2) * sub:(g % 2 + 1) * sub, :].reshape(blk, LANES).astype(BF16)

    def store_cols(ref, r, g, ps, val):
        if planes == 1:
            ref[0, r, g * blk:(g + 1) * blk, ps] = val
        elif planes == 4:
            ref[0, :, 0, g * sub:(g + 1) * sub, ps] = val.reshape(planes, sub, val.shape[-1])
        else:
            ref[0, :, g * sub:(g + 1) * sub, ps] = val.reshape(planes, sub, val.shape[-1])

    pending = {}

    def store_out(r, g, ps, val):
        if planes == 1:
            o_ref[0, r, g * blk:(g + 1) * blk, ps] = val.astype(BF16)
        elif planes == 4:
            o_ref[0, :, 0, g * sub:(g + 1) * sub, ps] = (
                val.reshape(planes, sub, val.shape[-1]).astype(BF16))
        elif g % 2 == 0:
            pending[ps.start] = val.reshape(planes, sub, LANES)
        else:
            both = jnp.concatenate(
                [pending.pop(ps.start), val.reshape(planes, sub, LANES)], axis=1)
            o_ref[0, :, (g // 2) * half:(g // 2 + 1) * half, ps] = both.astype(BF16)

    def pair(bounded, r, g, p):
        ps = slice(p * LANES, (p + 1) * LANES)
        qp = load(q_ref, r, g, ps)
        zero = jnp.zeros_like(qp)
        qs = jnp.concatenate(
            [jnp.where(lo_half, qp, zero), jnp.where(lo_half, zero, qp)], axis=0)
        if g == 0:
            last = kp_ref.shape[-2] // sub - 1
            k_prev, v_prev = load(kp_ref, r, last, ps), load(vp_ref, r, last, ps)
        else:
            k_prev, v_prev = load(k_ref, r, g - 1, ps), load(v_ref, r, g - 1, ps)
        k2 = jnp.concatenate([k_prev, load(k_ref, r, g, ps)], axis=0)
        v2 = jnp.concatenate([v_prev, load(v_ref, r, g, ps)], axis=0)
        s = lax.dot_general(qs, k2, NT_DIMS, preferred_element_type=F32)
        s = s + (bias_head if g == 0 else bias_rest)
        if bounded:
            e = jnp.exp2(s)
        else:
            m = jnp.max(s, axis=-1, keepdims=True)
            e = jnp.exp2(s - m)
        l = jnp.sum(e, axis=-1, keepdims=True)
        pv = _dot(e.astype(BF16), v2)
        out = jnp.where(lo_half, pv[0:blk], pv[blk:])
        if chain and bounded:
            out = out + load(acc_in_ref, r, g, ps).astype(F32)
        store_out(r, g, ps, out)
        store_cols(lse_ref, r, g, slice(2 * p, 2 * p + 1), l[0:blk])
        store_cols(lse_ref, r, g, slice(2 * p + 1, 2 * p + 2), l[blk:])
        if not bounded:
            c0 = N_HEADS + 2 * p
            store_cols(lse_ref, r, g, slice(c0, c0 + 1), m[0:blk])
            store_cols(lse_ref, r, g, slice(c0 + 1, c0 + 2), m[blk:])

    def merge_block(r, g):
        everything = slice(0, LANES)
        head_lane = lane < N_HEADS
        s_own, s_prev = load(lse_ref, r, g, everything), load(st_in_ref, r, g, everything)
        m_own = pltpu.roll(s_own, LANES - N_HEADS, axis=1)
        m_prev = pltpu.roll(s_prev, LANES - N_HEADS, axis=1)
        m_new = jnp.maximum(m_prev, m_own)
        a = jnp.where(head_lane, jnp.exp2(m_prev - m_new), 0.0)
        b = jnp.where(head_lane, jnp.exp2(m_own - m_new), 0.0)
        stats = (jnp.where(head_lane, a * s_prev + b * s_own, 0.0)
                 + pltpu.roll(jnp.where(head_lane, m_new, 0.0), N_HEADS, axis=1))
        store_cols(lse_ref, r, g, everything, stats)

        def expand(v):
            hi, lo = _split_bf16(v)
            return _dot(jnp.concatenate([hi, lo], axis=1), e2_ref[...])

        wide = slice(0, o_ref.shape[-1])
        merged = (expand(a) * load(acc_in_ref, r, g, wide).astype(F32)
                  + expand(b) * load(o_ref, r, g, wide).astype(F32))
        store_out(r, g, wide, merged)

    def body(bounded):
        lse_ref[...] = jnp.zeros(lse_ref.shape, F32)
        for r in range(n_res):
            if planes == 16:
                order = [(g0 + h, p) for g0 in range(0, group, 2)
                         for p in range(N_HEADS // 2) for h in range(2)]
            else:
                order = [(g, p) for g in range(group) for p in range(N_HEADS // 2)]
            for g, p in order:
                pair(bounded, r, g, p)
                if chain and not bounded and p == N_HEADS // 2 - 1:
                    merge_block(r, g)
        if chain and bounded:
            lse_ref[...] = lse_ref[...] + st_in_ref[...]

    bounded = bounded_ref[0] != 0

    @pl.when(bounded)
    def _():
        body(True)

    @pl.when(jnp.logical_not(bounded))
    def _():
        body(False)


def _dilated_attention(bounded, q, k, v, dilation, e2, running=None):
    bsz, n_planes, rows, w = q.shape
    planes = n_planes // dilation
    assert planes in (1, 4, 16)
    sub = ATT_BLK // planes
    nb = rows // sub
    group = min(ATT_GROUP, nb)
    assert rows % sub == 0 and nb % group == 0
    n_res = 1
    window = 1
    if planes == 1:
        n_res = min(ATT_GROUP // group, dilation)
        view = lambda t: t
        blk_shape = lambda n, width: (1, n_res, n * ATT_BLK, width)
        cur = lambda b, res, j: (b, res, j, 0)
        prev = lambda b, res, j: (b, res, jnp.maximum(j * group - 1, 0), 0)
    elif planes == 4:
        view = lambda t: t.reshape(bsz, planes, dilation, rows, t.shape[-1])
        blk_shape = lambda n, width: (1, planes, 1, n * sub, width)
        cur = lambda b, res, j: (b, 0, res, j, 0)
        prev = lambda b, res, j: (b, 0, res, jnp.maximum(j * group - 1, 0), 0)
    else:
        assert group % 2 == 0
        window = 2
        view = lambda t: t
        blk_shape = lambda n, width: (1, planes, n * sub, width)
        cur = lambda b, res, j: (b, 0, j, 0)
        prev = lambda b, res, j: (b, 0, jnp.maximum(j * (group // 2) - 1, 0), 0)
    tile = lambda width=w: pl.BlockSpec(blk_shape(group, width), cur)
    single = pl.BlockSpec(blk_shape(window, w), prev)
    lse_shape = (bsz, n_planes, rows, LANES)
    chain = running is not None
    extra_specs = [tile(), tile(LANES), _const_spec(e2.shape)] if chain else []
    extra_args = [view(running[0]), view(running[1]), e2] if chain else []
    o, lse = pl.pallas_call(
        functools.partial(_attn_kernel, planes, group, chain),
        grid=(bsz, dilation // n_res, nb // group),
        in_specs=[pl.BlockSpec(memory_space=pltpu.SMEM), tile(), tile(), tile(), single, single,
                  *extra_specs],
        out_specs=[tile(), tile(LANES)],
        out_shape=[jax.ShapeDtypeStruct(_view_shape(q.shape, planes, dilation), BF16),
                   jax.ShapeDtypeStruct(_view_shape(lse_shape, planes, dilation), F32)],
        compiler_params=pltpu.CompilerParams(
            dimension_semantics=("parallel", "parallel", "arbitrary"),
            vmem_limit_bytes=VMEM_LIMIT),
        name=f"dilated_attn_r{dilation}",
    )(bounded, view(q), view(k), view(v), view(k), view(v), *extra_args)
    return o.reshape(q.shape), lse.reshape(lse_shape)


def _view_shape(shape, planes, dilation):
    bsz, _, rows, width = shape
    return (bsz, planes, dilation, rows, width) if planes == 4 else shape


def _outproj_kernel(x_ref, mod_ref, ys_ref, o_ref, st_ref, nw_ref, e2_ref, w_ref, out_ref,
                    wide_ref, wide2_ref, acc_ref):
    tm = x_ref.shape[1]
    cls_rows = tm // CLS_STRIDE
    k_early = D_INNER // 2
    acc_ref[...] = _dot(ys_ref[0, :, 0:k_early], w_ref[0:k_early, :])

    def regrouped(cls_ref):
        return cls_ref[0].reshape(tm, cls_ref.shape[-1])

    def natural(val):
        tiles = wide_ref.shape[0]
        quarter = cls_rows * 4
        for res in range(CLS_STRIDE):
            rho, a = res % 4, res // 4
            for t in range(tiles):
                wide2_ref[t, pl.ds(rho * quarter + a, cls_rows, stride=4), :] = (
                    val[res * cls_rows:(res + 1) * cls_rows, t * LANES:(t + 1) * LANES])
        for t in range(tiles):
            for rho in range(4):
                wide_ref[t, pl.ds(rho, quarter, stride=4), :] = (
                    wide2_ref[t, rho * quarter:(rho + 1) * quarter, :])
        return jnp.concatenate([wide_ref[t] for t in range(tiles)], axis=1)

    st = regrouped(st_ref)
    head_lane = lax.broadcasted_iota(jnp.int32, st.shape, 1) < N_HEADS
    hi, lo = _split_bf16(jnp.where(head_lane, 1.0 / st, 0.0))
    inv_x = _dot(jnp.concatenate([hi, lo], axis=1), e2_ref[...])
    o = inv_x * regrouped(o_ref).astype(F32)
    ms = jnp.mean(o * o, axis=-1, keepdims=True)
    y_att = (o * lax.rsqrt(ms + EPS) * nw_ref[...]).astype(BF16)
    att = _dot(y_att, w_ref[D_INNER:, :])
    late = _dot(ys_ref[0, :, k_early:D_INNER], w_ref[k_early:D_INNER, :])
    mix = acc_ref[...] + late + natural(att)
    out_ref[0] = x_ref[0] + mod_ref[0, 2:3, :] * mix


def _out_projection(x, mod3, y_ssd, att_acc, att_stats, norm_w, e2, w_out, tm):
    bsz, s, d = x.shape
    row = lambda b, i: (b, i, 0)
    wide = pl.BlockSpec((1, tm, D_INNER), row)
    cls = lambda width: pl.BlockSpec((1, CLS_STRIDE, tm // CLS_STRIDE, width),
                                     lambda b, i: (b, 0, i, 0))
    return pl.pallas_call(
        _outproj_kernel,
        grid=(bsz, s // tm),
        in_specs=[
            pl.BlockSpec((1, tm, d), row),
            pl.BlockSpec((1, N_MOD, d), lambda b, i: (b, 0, 0)),
            wide, cls(D_INNER), cls(LANES),
            _const_spec((1, D_INNER)),
            _const_spec((2 * LANES, D_INNER)),
            _const_spec((2 * D_INNER, d)),
        ],
        out_specs=pl.BlockSpec((1, tm, d), row),
        out_shape=jax.ShapeDtypeStruct((bsz, s, d), F32),
        scratch_shapes=[pltpu.VMEM((d // LANES, tm, LANES), F32),
                        pltpu.VMEM((d // LANES, tm, LANES), F32),
                        pltpu.VMEM((tm, d), F32)],
        compiler_params=pltpu.CompilerParams(
            dimension_semantics=("parallel", "arbitrary"), vmem_limit_bytes=VMEM_LIMIT),
        name="out_proj",
    )(x, mod3, y_ssd, att_acc, att_stats, norm_w, e2, w_out)


FF_TN = 1024


def _mlp_kernel(x_ref, mod_ref, nw_ref, w1_ref, w2_ref, out_ref, h_ref):
    x = x_ref[0]
    h = _norm_mod(x, nw_ref[...], mod_ref[0, 4:5, :], mod_ref[0, 3:4, :])
    h_ref[...] = h.astype(BF16)
    acc = jnp.zeros(x.shape, F32)
    for j in range(D_FF // FF_TN):
        fs = slice(j * FF_TN, (j + 1) * FF_TN)
        u = jnp.maximum(_dot(h_ref[...], w1_ref[:, fs]), 0.0)
        acc = acc + _dot((u * u).astype(BF16), w2_ref[fs, :])
    out_ref[0] = x + mod_ref[0, 5:6, :] * acc


def _mlp(x, mod3, norm_w, w1, w2, tm):
    bsz, s, d = x.shape
    row = lambda b, i: (b, i, 0)
    return pl.pallas_call(
        _mlp_kernel,
        grid=(bsz, s // tm),
        in_specs=[
            pl.BlockSpec((1, tm, d), row),
            pl.BlockSpec((1, N_MOD, d), lambda b, i: (b, 0, 0)),
            _const_spec((1, d)),
            _const_spec((d, D_FF)),
            _const_spec((D_FF, d)),
        ],
        out_specs=pl.BlockSpec((1, tm, d), row),
        out_shape=jax.ShapeDtypeStruct((bsz, s, d), F32),
        scratch_shapes=[pltpu.VMEM((tm, d), BF16)],
        compiler_params=pltpu.CompilerParams(
            dimension_semantics=("parallel", "arbitrary"), vmem_limit_bytes=VMEM_LIMIT),
        name="mlp",
    )(x, mod3, norm_w, w1, w2)


def _head_expand_matrix():
    head_of_lane = jnp.arange(D_INNER) // HEAD_DIM
    e = (jnp.arange(LANES)[:, None] == head_of_lane[None, :]).astype(BF16)
    return jnp.concatenate([e, e], axis=0)


def _pad_lanes(v, n=LANES):
    return jnp.pad(v.astype(F32), (0, n - v.shape[0])).reshape(1, n)


def kernel(x, c, norm1_w, norm2_w, w_ada, b_ada, w_in, conv_w, conv_b, dt_bias, a_log, d_skip,
           ssd_norm_w, q_norm_w, k_norm_w, attn_norm_w, w_out, w_ff1, w_ff2):
    bsz, s, d = x.shape
    depth = w_ada.shape[0]
    tm = ROW_TILE
    assert d == D_MODEL and s % ROW_TILE == 0 and s % TAIL_ROW_TILE == 0 and s % SSD_ROWS == 0
    assert s % (CLS_STRIDE * ATT_BLK) == 0
    e2 = _head_expand_matrix()
    idx = jnp.arange(MXU_TILE) // HEAD_DIM
    bd = (idx[:, None] == idx[None, :]).astype(BF16)
    t = (jnp.arange(SSD_CHUNK)[:, None] >= jnp.arange(SSD_CHUNK)[None, :]).astype(BF16)
    tril2 = jnp.concatenate([t, t], axis=1)
    o_xbc = D_INNER + CONV_CH
    o_dt = o_xbc + N_HEADS

    for l in range(depth):
        mod3 = _modulation(c, w_ada[l], b_ada[l]).reshape(bsz, N_MOD, d)
        wl = w_in[l]
        w_dt = jnp.pad(wl[:, o_xbc:o_dt], ((0, 0), (0, LANES - N_HEADS))).astype(BF16)
        z, xbc, dt, q16, k16, v16 = _in_projection(
            x, mod3, norm1_w[l].reshape(1, d), wl[:, :o_xbc].astype(BF16),
            wl[:, o_dt:].astype(BF16), w_dt, _pad_lanes(dt_bias[l]),
            jnp.tile(q_norm_w[l], N_HEADS).reshape(1, D_INNER),
            jnp.tile(k_norm_w[l], N_HEADS).reshape(1, D_INNER), bd, tm)
        y_ssd = _ssd_mixer(
            xbc, z, dt, conv_w[l], conv_b[l].reshape(1, CONV_CH),
            _pad_lanes(a_log[l]), jnp.repeat(d_skip[l], HEAD_DIM).reshape(1, D_INNER),
            ssd_norm_w[l].reshape(1, D_INNER), tril2, e2)
        score_bound = (HEAD_DIM ** 0.5 * LOG2E * jnp.max(jnp.abs(q_norm_w[l]))
                       * jnp.max(jnp.abs(k_norm_w[l])))
        bounded = (score_bound * BF16_SLACK <= SCORE_BOUND).astype(jnp.int32).reshape(1)
        running = None
        for window, dilation in DILATED_PATTERNS:
            assert window // dilation == ATT_BLK
            running = _dilated_attention(bounded, q16, k16, v16, dilation, e2, running)
        x = _out_projection(x, mod3, y_ssd, *running, attn_norm_w[l].reshape(1, D_INNER),
                            e2, w_out[l].astype(BF16), TAIL_ROW_TILE)
        x = _mlp(x, mod3, norm2_w[l].reshape(1, d), w_ff1[l].astype(BF16),
                 w_ff2[l].astype(BF16), TAIL_ROW_TILE)
    return x.astype(c.dtype)
```

```python
import functools

import jax
import jax.numpy as jnp
from jax import lax
from jax.experimental import pallas as pl
from jax.experimental.pallas import tpu as pltpu

F32 = jnp.float32
BF16 = jnp.bfloat16

D_MODEL = 1024
HEAD_DIM = 64
N_HEADS = 16
SSD_GROUPS = 4
SSD_STATE = 128
SSD_CONV = 4
SSD_CHUNK = 128
D_INNER = N_HEADS * HEAD_DIM
CONV_CH = D_INNER + 2 * SSD_GROUPS * SSD_STATE
D_FF = 4 * D_MODEL
N_MOD = 6
EPS = 1e-6
DILATED_PATTERNS = ((128, 1), (512, 4), (2048, 16))
ATT_BLK = 128
ATT_GROUP = 8
CLS_STRIDE = 16
ROW_TILE = 512
TAIL_ROW_TILE = 1024
LANES = 128
MXU_TILE = 256
VMEM_LIMIT = 56 * 1024 * 1024

LOG2E = 1.4426950408889634
SCORE_BOUND = 60.0
BF16_SLACK = 1.02

NT_DIMS = (((1,), (1,)), ((), ()))
TN_DIMS = (((0,), (0,)), ((), ()))


def _dot(a, b):
    return jnp.dot(a, b, preferred_element_type=F32)


def _split_bf16(v):
    hi = v.astype(BF16)
    lo = (v - hi.astype(F32)).astype(BF16)
    return hi, lo


def _silu(v):
    h = 0.5 * v
    return h + h * jnp.tanh(h)


def _const_spec(shape):
    nd = len(shape)
    return pl.BlockSpec(shape, lambda *_: (0,) * nd, pipeline_mode=pl.Buffered(1))


def _mod_kernel(c_ref, w_ref, b_ref, o_ref):
    c = c_ref[...]
    ca = _silu(c)
    c_hi, c_lo = _split_bf16(ca)
    w = w_ref[...]
    w_hi, w_lo = _split_bf16(w)
    acc = _dot(c_hi, w_hi) + _dot(c_lo, w_hi) + _dot(c_hi, w_lo)
    o_ref[...] = acc + b_ref[...]


def _modulation(c, w_ada, b_ada):
    bsz, d = c.shape
    n = w_ada.shape[1]
    tn = n // 4
    return pl.pallas_call(
        _mod_kernel,
        grid=(n // tn,),
        in_specs=[
            pl.BlockSpec((bsz, d), lambda j: (0, 0)),
            pl.BlockSpec((d, tn), lambda j: (0, j)),
            pl.BlockSpec((1, tn), lambda j: (0, j)),
        ],
        out_specs=pl.BlockSpec((bsz, tn), lambda j: (0, j)),
        out_shape=jax.ShapeDtypeStruct((bsz, n), F32),
        compiler_params=pltpu.CompilerParams(
            dimension_semantics=("arbitrary",), vmem_limit_bytes=VMEM_LIMIT),
        name="adaln_mod",
    )(c, w_ada, b_ada.reshape(1, n))


def _wsplit_kernel(w_ref, wzx_ref, wqkv_ref, wdt_ref):
    o_xbc = D_INNER + CONV_CH
    o_dt = o_xbc + N_HEADS
    wzx_ref[...] = w_ref[:, 0:o_xbc].astype(BF16)
    wqkv_ref[...] = w_ref[:, o_dt:o_dt + 3 * D_INNER].astype(BF16)
    dt_cols = w_ref[:, o_xbc:o_dt]
    pad = jnp.zeros((dt_cols.shape[0], LANES - N_HEADS), F32)
    wdt_ref[...] = jnp.concatenate([dt_cols, pad], axis=1).astype(BF16)


def _split_in_weight(w):
    d, n = w.shape
    rows = d // 4
    o_xbc = D_INNER + CONV_CH
    return pl.pallas_call(
        _wsplit_kernel,
        grid=(d // rows,),
        in_specs=[pl.BlockSpec((rows, n), lambda i: (i, 0))],
        out_specs=[pl.BlockSpec((rows, o_xbc), lambda i: (i, 0)),
                   pl.BlockSpec((rows, 3 * D_INNER), lambda i: (i, 0)),
                   pl.BlockSpec((rows, LANES), lambda i: (i, 0))],
        out_shape=[jax.ShapeDtypeStruct((d, o_xbc), BF16),
                   jax.ShapeDtypeStruct((d, 3 * D_INNER), BF16),
                   jax.ShapeDtypeStruct((d, LANES), BF16)],
        compiler_params=pltpu.CompilerParams(
            dimension_semantics=("arbitrary",), vmem_limit_bytes=VMEM_LIMIT),
        name="w_in_split",
    )(w)


IN_TN = 512


def _norm_mod(x, nw, scale, shift):
    ms = jnp.mean(x * x, axis=-1, keepdims=True)
    return (x * lax.rsqrt(ms + EPS) * nw) * (1.0 + scale) + shift


def _inproj_kernel(x_ref, mod_ref, nw_ref, wzx_ref, wqkv_ref, wdt_ref, dtb_ref, qw_ref, kw_ref,
                   bd_ref, z_ref, xbc_ref, dt_ref, q16_ref, k16_ref, v16_ref,
                   h_ref, h16_ref, stage_ref, stage2_ref):
    x = x_ref[0]
    tm = x.shape[0]
    cls_rows = tm // CLS_STRIDE
    h = _norm_mod(x, nw_ref[...], mod_ref[0, 1:2, :], mod_ref[0, 0:1, :])
    h_ref[...] = h.astype(BF16)

    tiles = h.shape[1] // LANES
    quarter = tm // 4
    for t in range(tiles):
        stage_ref[t] = h[:, t * LANES:(t + 1) * LANES]
    for t in range(tiles):
        for rho in range(4):
            stage2_ref[t, rho * quarter:(rho + 1) * quarter, :] = (
                stage_ref[t, pl.ds(rho, quarter, stride=4), :])
    for res in range(CLS_STRIDE):
        rho, a = res % 4, res // 4
        rows = [stage2_ref[t, pl.ds(rho * quarter + a, cls_rows, stride=4), :]
                for t in range(tiles)]
        h16_ref[res * cls_rows:(res + 1) * cls_rows, :] = (
            jnp.concatenate(rows, axis=1).astype(BF16))

    def proj(w_ref, c0, width, lhs_ref=h_ref):
        return _dot(lhs_ref[...], w_ref[:, c0:c0 + width])

    def qk_norm(acc, w):
        sq = (acc * acc).astype(BF16)
        parts = [_dot(sq[:, i:i + MXU_TILE], bd_ref[...]) for i in range(0, IN_TN, MXU_TILE)]
        ss = jnp.concatenate(parts, axis=1)
        return acc * lax.rsqrt(ss * (1.0 / HEAD_DIM) + EPS) * w

    def emit(cls_ref, sl, val):
        for res in range(CLS_STRIDE):
            cls_ref[0, res, :, sl] = val[res * cls_rows:(res + 1) * cls_rows].astype(BF16)

    col = 0
    for j in range(D_INNER // IN_TN):
        sl = slice(j * IN_TN, (j + 1) * IN_TN)
        emit(q16_ref, sl, qk_norm(proj(wqkv_ref, col, IN_TN, h16_ref), qw_ref[:, sl])
             * (HEAD_DIM ** -0.5 * LOG2E))
        col += IN_TN
    for j in range(D_INNER // IN_TN):
        sl = slice(j * IN_TN, (j + 1) * IN_TN)
        emit(k16_ref, sl, qk_norm(proj(wqkv_ref, col, IN_TN, h16_ref), kw_ref[:, sl]))
        col += IN_TN
    for j in range(D_INNER // IN_TN):
        emit(v16_ref, slice(j * IN_TN, (j + 1) * IN_TN), proj(wqkv_ref, col, IN_TN, h16_ref))
        col += IN_TN
    dt_raw = proj(wdt_ref, 0, LANES) + dtb_ref[...]
    dt_ref[0] = jnp.maximum(dt_raw, 0.0) + jnp.log(1.0 + jnp.exp(-jnp.abs(dt_raw)))
    col = D_INNER
    for j in range(CONV_CH // IN_TN):
        xbc_ref[0, :, j * IN_TN:(j + 1) * IN_TN] = proj(wzx_ref, col, IN_TN).astype(BF16)
        col += IN_TN
    col = 0
    for j in range(D_INNER // IN_TN):
        z_ref[0, :, j * IN_TN:(j + 1) * IN_TN] = proj(wzx_ref, col, IN_TN).astype(BF16)
        col += IN_TN


def _in_projection(x, mod3, norm_w, w_zx, w_qkv, w_dt, dt_bias, qw, kw, bd, tm):
    bsz, s, d = x.shape
    row = lambda b, i: (b, i, 0)
    out_bf = lambda n: jax.ShapeDtypeStruct((bsz, s, n), BF16)
    cls_spec = pl.BlockSpec((1, CLS_STRIDE, tm // CLS_STRIDE, D_INNER), lambda b, i: (b, 0, i, 0))
    cls_shape = jax.ShapeDtypeStruct((bsz, CLS_STRIDE, s // CLS_STRIDE, D_INNER), BF16)
    return pl.pallas_call(
        _inproj_kernel,
        grid=(bsz, s // tm),
        in_specs=[
            pl.BlockSpec((1, tm, d), row),
            pl.BlockSpec((1, N_MOD, d), lambda b, i: (b, 0, 0)),
            _const_spec((1, d)),
            _const_spec(w_zx.shape),
            _const_spec(w_qkv.shape),
            _const_spec(w_dt.shape),
            _const_spec((1, LANES)),
            _const_spec((1, D_INNER)),
            _const_spec((1, D_INNER)),
            _const_spec((MXU_TILE, MXU_TILE)),
        ],
        out_specs=[
            pl.BlockSpec((1, tm, D_INNER), row),
            pl.BlockSpec((1, tm, CONV_CH), row),
            pl.BlockSpec((1, tm, LANES), row),
            cls_spec, cls_spec, cls_spec,
        ],
        out_shape=[out_bf(D_INNER), out_bf(CONV_CH),
                   jax.ShapeDtypeStruct((bsz, s, LANES), F32),
                   cls_shape, cls_shape, cls_shape],
        scratch_shapes=[pltpu.VMEM((tm, d), BF16),
                        pltpu.VMEM((tm, d), BF16),
                        pltpu.VMEM((d // LANES, tm, LANES), F32),
                        pltpu.VMEM((d // LANES, tm, LANES), F32)],
        compiler_params=pltpu.CompilerParams(
            dimension_semantics=("parallel", "arbitrary"), vmem_limit_bytes=VMEM_LIMIT),
        name="in_proj",
    )(x, mod3, norm_w, w_zx, w_qkv, w_dt, dt_bias, qw, kw, bd)


SSD_ROWS = 1024
CONV_HALO = 8


def _ssd_kernel(xbc_ref, z_ref, dt_ref, cw_ref, cb_ref, alog_ref, dskip_ref,
                nw_ref, tril2_ref, e2_ref, shift_ref, o_ref, halo_ref, xc_ref, state_ref):
    rows = xbc_ref.shape[1]
    L = SSD_CHUNK
    taps = SSD_CONV - 1

    @pl.when(pl.program_id(1) == 0)
    def _():
        halo_ref[...] = jnp.zeros_like(halo_ref)
        state_ref[...] = jnp.zeros_like(state_ref)

    hsub = lax.broadcasted_iota(jnp.int32, (CONV_HALO, MXU_TILE), 0)
    group_w = D_INNER // SSD_GROUPS
    heads_per_group = N_HEADS // SSD_GROUPS

    def conv_chunk(c):
        r0 = c * L
        for c0 in range(0, CONV_CH, MXU_TILE):
            cs = slice(c0, c0 + MXU_TILE)
            u = xbc_ref[0, r0:r0 + L, cs]
            shifted = _dot(shift_ref[...], u)
            uf = u.astype(F32)
            acc = cb_ref[:, cs] + cw_ref[taps:taps + 1, cs] * uf
            for k in range(1, taps + 1):
                acc = acc + cw_ref[taps - k:taps - k + 1, cs] * shifted[(k - 1) * L:k * L]
            xc_ref[r0:r0 + L, cs] = _silu(acc)
            if c == 0:
                halo = halo_ref[:, cs]
            else:
                halo = xbc_ref[0, r0 - 2 * CONV_HALO:r0, cs].astype(F32)[CONV_HALO:]
            head = acc[0:CONV_HALO]
            for k in range(1, taps + 1):
                prev = jnp.where(hsub < k, pltpu.roll(halo, k, axis=0), 0.0)
                head = head + cw_ref[taps - k:taps - k + 1, cs] * prev
            xc_ref[r0:r0 + CONV_HALO, cs] = _silu(head)
            if r0 + L == rows:
                halo_ref[:, cs] = uf[L - CONV_HALO:L]

    lane = lax.broadcasted_iota(jnp.int32, (L, LANES), 1)
    sub = lax.broadcasted_iota(jnp.int32, (L, LANES), 0)
    tril = sub >= lane
    lo_half = lane < HEAD_DIM
    a_neg = jnp.where(lane[0:1] < N_HEADS, -jnp.exp(alog_ref[...]), 0.0)

    def split_cat(v):
        hi, lo = _split_bf16(v)
        return jnp.concatenate([hi, lo], axis=1)

    def expand(v_cat, gs):
        return _dot(v_cat, e2_ref[:, gs])

    conv_chunk(0)
    for c in range(rows // L):
        if c + 1 < rows // L:
            conv_chunk(c + 1)
        r0 = c * L
        rs = slice(r0, r0 + L)
        dt = dt_ref[0, rs, :]
        d_a = dt * a_neg
        da_hi, da_lo = _split_bf16(d_a)
        a_cs = _dot(tril2_ref[...], jnp.concatenate([da_hi, da_lo], axis=0))
        a_cs_t = a_cs.T
        a_last = a_cs[L - 1:L, :]
        dt_cat = split_cat(dt)
        w_cat = split_cat(dt * jnp.exp(a_last - a_cs))
        do_cat = split_cat(jnp.exp(a_cs))

        for g in range(SSD_GROUPS):
            gs = slice(g * group_w, (g + 1) * group_w)
            xs = xc_ref[rs, gs]
            do_x = expand(do_cat, gs)
            xdt = (xs * expand(dt_cat, gs)).astype(BF16)
            xw = (xs * expand(w_cat, gs)).astype(BF16)
            b0 = D_INNER + g * SSD_STATE
            c0 = D_INNER + SSD_GROUPS * SSD_STATE + g * SSD_STATE
            bg = xc_ref[rs, b0:b0 + SSD_STATE].astype(BF16)
            cg = xc_ref[rs, c0:c0 + SSD_STATE].astype(BF16)
            cb = lax.dot_general(cg, bg, NT_DIMS, preferred_element_type=F32)
            yd_parts = []
            for pair in range(heads_per_group // 2):
                ms = []
                for e in range(2):
                    hd = g * heads_per_group + pair * 2 + e
                    colb = jnp.broadcast_to(a_cs[:, hd:hd + 1], (L, L))
                    rowb = jnp.broadcast_to(a_cs_t[hd:hd + 1, :], (L, L))
                    lmat = jnp.exp(jnp.where(tril, colb - rowb, -jnp.inf))
                    ms.append((cb * lmat).astype(BF16))
                mcat = jnp.concatenate(ms, axis=1)
                xp = xdt[:, pair * LANES:(pair + 1) * LANES]
                zero = jnp.zeros_like(xp)
                xstack = jnp.concatenate(
                    [jnp.where(lo_half, xp, zero), jnp.where(lo_half, zero, xp)], axis=0)
                yd_parts.append(_dot(mcat, xstack))
            y_diag = jnp.concatenate(yd_parts, axis=1)
            st = state_ref[:, gs]
            y_off = _dot(cg, st.astype(BF16)) * do_x
            upd = lax.dot_general(bg, xw, TN_DIMS, preferred_element_type=F32)
            state_ref[:, gs] = st * do_x[L - 1:L] + upd
            gate = _silu(z_ref[0, rs, gs].astype(F32))
            yg = (dskip_ref[:, gs] * xs + y_diag + y_off) * gate
            ss = jnp.mean(yg * yg, axis=-1, keepdims=True)
            o_ref[0, rs, gs] = (yg * lax.rsqrt(ss + EPS) * nw_ref[:, gs]).astype(BF16)


def _ssd_mixer(xbc, z, dt, conv_w, conv_b, a_log, d_skip, norm_w, tril2, e2):
    bsz, s, _ = xbc.shape
    rows = SSD_ROWS
    row = lambda b, i: (b, i, 0)
    t = jnp.arange(SSD_CHUNK)
    shift = jnp.concatenate(
        [(t[:, None] - k == t[None, :]).astype(BF16) for k in range(1, SSD_CONV)], axis=0)
    return pl.pallas_call(
        _ssd_kernel,
        grid=(bsz, s // rows),
        in_specs=[
            pl.BlockSpec((1, rows, CONV_CH), row),
            pl.BlockSpec((1, rows, D_INNER), row),
            pl.BlockSpec((1, rows, LANES), row),
            _const_spec((SSD_CONV, CONV_CH)),
            _const_spec((1, CONV_CH)),
            _const_spec((1, LANES)),
            _const_spec((1, D_INNER)),
            _const_spec((1, D_INNER)),
            _const_spec((SSD_CHUNK, 2 * SSD_CHUNK)),
            _const_spec((2 * LANES, D_INNER)),
            _const_spec(((SSD_CONV - 1) * SSD_CHUNK, SSD_CHUNK)),
        ],
        out_specs=pl.BlockSpec((1, rows, D_INNER), row),
        out_shape=jax.ShapeDtypeStruct((bsz, s, D_INNER), BF16),
        scratch_shapes=[
            pltpu.VMEM((CONV_HALO, CONV_CH), F32),
            pltpu.VMEM((rows, CONV_CH), F32),
            pltpu.VMEM((SSD_STATE, D_INNER), F32),
        ],
        compiler_params=pltpu.CompilerParams(
            dimension_semantics=("parallel", "arbitrary"), vmem_limit_bytes=VMEM_LIMIT),
        name="ssd_mixer",
    )(xbc, z, dt, conv_w, conv_b, a_log, d_skip, norm_w, tril2, e2, shift)


def _attn_kernel(planes, group, chain, bounded_ref, q_ref, k_ref, v_ref, kp_ref, vp_ref, *refs):
    if chain:
        acc_in_ref, st_in_ref, e2_ref, o_ref, lse_ref = refs
        assert planes in (1, 4)
    else:
        o_ref, lse_ref = refs
    blk = ATT_BLK
    sub = blk // planes
    first = pl.program_id(2) == 0

    def pos(i):
        return i if planes == 1 else (i % sub) * planes + i // sub

    row = lax.broadcasted_iota(jnp.int32, (blk, 2 * blk), 0)
    key = lax.broadcasted_iota(jnp.int32, (blk, 2 * blk), 1)
    rel = jnp.where(key >= blk, blk + pos(key - blk), pos(key)) - pos(row)
    band = (rel >= 0) & (rel <= blk)
    lo_key = jnp.where(first, blk, 0)
    bias_head = jnp.where(band & (key >= lo_key), 0.0, -jnp.inf)
    bias_head = jnp.concatenate([bias_head, bias_head], axis=0)
    bias_rest = jnp.where(band, 0.0, -jnp.inf)
    bias_rest = jnp.concatenate([bias_rest, bias_rest], axis=0)
    lane = lax.broadcasted_iota(jnp.int32, (blk, LANES), 1)
    lo_half = lane < HEAD_DIM

    n_res = q_ref.shape[1] if planes == 1 else 1

    half = 2 * sub

    def load(ref, r, g, ps):
        if planes == 1:
            return ref[0, r, g * blk:(g + 1) * blk, ps]
        if planes == 4:
            return ref[0, :, 0, g * sub:(g + 1) * sub, ps].reshape(blk, ps.stop - ps.start)
        rows = ref[0, :, (g // 2) * half:(g // 2 + 1) * half, ps].astype(F32)
        return rows[:, (g % 2) * sub:(g % 2 + 1) * sub, :].reshape(blk, LANES).astype(BF16)

    def store_cols(ref, r, g, ps, val):
        if planes == 1:
            ref[0, r, g * blk:(g + 1) * blk, ps] = val
        elif planes == 4:
            ref[0, :, 0, g * sub:(g + 1) * sub, ps] = val.reshape(planes, sub, val.shape[-1])
        else:
            ref[0, :, g * sub:(g + 1) * sub, ps] = val.reshape(planes, sub, val.shape[-1])

    pending = {}

    def store_out(r, g, ps, val):
        if planes == 1:
            o_ref[0, r, g * blk:(g + 1) * blk, ps] = val.astype(BF16)
        elif planes == 4:
            o_ref[0, :, 0, g * sub:(g + 1) * sub, ps] = (
                val.reshape(planes, sub, val.shape[-1]).astype(BF16))
        elif g % 2 == 0:
            pending[ps.start] = val.reshape(planes, sub, LANES)
        else:
            both = jnp.concatenate(
                [pending.pop(ps.start), val.reshape(planes, sub, LANES)], axis=1)
            o_ref[0, :, (g // 2) * half:(g // 2 + 1) * half, ps] = both.astype(BF16)

    def pair(bounded, r, g, p):
        ps = slice(p * LANES, (p + 1) * LANES)
        qp = load(q_ref, r, g, ps)
        zero = jnp.zeros_like(qp)
        qs = jnp.concatenate(
            [jnp.where(lo_half, qp, zero), jnp.where(lo_half, zero, qp)], axis=0)
        if g == 0:
            last = kp_ref.shape[-2] // sub - 1
            k_prev, v_prev = load(kp_ref, r, last, ps), load(vp_ref, r, last, ps)
        else:
            k_prev, v_prev = load(k_ref, r, g - 1, ps), load(v_ref, r, g - 1, ps)
        k2 = jnp.concatenate([k_prev, load(k_ref, r, g, ps)], axis=0)
        v2 = jnp.concatenate([v_prev, load(v_ref, r, g, ps)], axis=0)
        s = lax.dot_general(qs, k2, NT_DIMS, preferred_element_type=F32)
        s = s + (bias_head if g == 0 else bias_rest)
        if bounded:
            e = jnp.exp2(s)
        else:
            m = jnp.max(s, axis=-1, keepdims=True)
            e = jnp.exp2(s - m)
        l = jnp.sum(e, axis=-1, keepdims=True)
        pv = _dot(e.astype(BF16), v2)
        out = jnp.where(lo_half, pv[0:blk], pv[blk:])
        if chain and bounded:
            out = out + load(acc_in_ref, r, g, ps).astype(F32)
        store_out(r, g, ps, out)
        store_cols(lse_ref, r, g, slice(2 * p, 2 * p + 1), l[0:blk])
        store_cols(lse_ref, r, g, slice(2 * p + 1, 2 * p + 2), l[blk:])
        if not bounded:
            c0 = N_HEADS + 2 * p
            store_cols(lse_ref, r, g, slice(c0, c0 + 1), m[0:blk])
            store_cols(lse_ref, r, g, slice(c0 + 1, c0 + 2), m[blk:])

    def merge_block(r, g):
        everything = slice(0, LANES)
        head_lane = lane < N_HEADS
        s_own, s_prev = load(lse_ref, r, g, everything), load(st_in_ref, r, g, everything)
        m_own = pltpu.roll(s_own, LANES - N_HEADS, axis=1)
        m_prev = pltpu.roll(s_prev, LANES - N_HEADS, axis=1)
        m_new = jnp.maximum(m_prev, m_own)
        a = jnp.where(head_lane, jnp.exp2(m_prev - m_new), 0.0)
        b = jnp.where(head_lane, jnp.exp2(m_own - m_new), 0.0)
        stats = (jnp.where(head_lane, a * s_prev + b * s_own, 0.0)
                 + pltpu.roll(jnp.where(head_lane, m_new, 0.0), N_HEADS, axis=1))
        store_cols(lse_ref, r, g, everything, stats)

        def expand(v):
            hi, lo = _split_bf16(v)
            return _dot(jnp.concatenate([hi, lo], axis=1), e2_ref[...])

        wide = slice(0, o_ref.shape[-1])
        merged = (expand(a) * load(acc_in_ref, r, g, wide).astype(F32)
                  + expand(b) * load(o_ref, r, g, wide).astype(F32))
        store_out(r, g, wide, merged)

    def body(bounded):
        lse_ref[...] = jnp.zeros(lse_ref.shape, F32)
        for r in range(n_res):
            if planes == 16:
                order = [(g0 + h, p) for g0 in range(0, group, 2)
                         for p in range(N_HEADS // 2) for h in range(2)]
            else:
                order = [(g, p) for g in range(group) for p in range(N_HEADS // 2)]
            for g, p in order:
                pair(bounded, r, g, p)
                if chain and not bounded and p == N_HEADS // 2 - 1:
                    merge_block(r, g)
        if chain and bounded:
            lse_ref[...] = lse_ref[...] + st_in_ref[...]

    bounded = bounded_ref[0] != 0

    @pl.when(bounded)
    def _():
        body(True)

    @pl.when(jnp.logical_not(bounded))
    def _():
        body(False)


def _dilated_attention(bounded, q, k, v, dilation, e2, running=None):
    bsz, n_planes, rows, w = q.shape
    planes = n_planes // dilation
    assert planes in (1, 4, 16)
    sub = ATT_BLK // planes
    nb = rows // sub
    group = min(ATT_GROUP, nb)
    assert rows % sub == 0 and nb % group == 0
    n_res = 1
    window = 1
    if planes == 1:
        n_res = min(ATT_GROUP // group, dilation)
        view = lambda t: t
        blk_shape = lambda n, width: (1, n_res, n * ATT_BLK, width)
        cur = lambda b, res, j: (b, res, j, 0)
        prev = lambda b, res, j: (b, res, jnp.maximum(j * group - 1, 0), 0)
    elif planes == 4:
        view = lambda t: t.reshape(bsz, planes, dilation, rows, t.shape[-1])
        blk_shape = lambda n, width: (1, planes, 1, n * sub, width)
        cur = lambda b, res, j: (b, 0, res, j, 0)
        prev = lambda b, res, j: (b, 0, res, jnp.maximum(j * group - 1, 0), 0)
    else:
        assert group % 2 == 0
        window = 2
        view = lambda t: t
        blk_shape = lambda n, width: (1, planes, n * sub, width)
        cur = lambda b, res, j: (b, 0, j, 0)
        prev = lambda b, res, j: (b, 0, jnp.maximum(j * (group // 2) - 1, 0), 0)
    tile = lambda width=w: pl.BlockSpec(blk_shape(group, width), cur)
    single = pl.BlockSpec(blk_shape(window, w), prev)
    lse_shape = (bsz, n_planes, rows, LANES)
    chain = running is not None
    extra_specs = [tile(), tile(LANES), _const_spec(e2.shape)] if chain else []
    extra_args = [view(running[0]), view(running[1]), e2] if chain else []
    o, lse = pl.pallas_call(
        functools.partial(_attn_kernel, planes, group, chain),
        grid=(bsz, dilation // n_res, nb // group),
        in_specs=[pl.BlockSpec(memory_space=pltpu.SMEM), tile(), tile(), tile(), single, single,
                  *extra_specs],
        out_specs=[tile(), tile(LANES)],
        out_shape=[jax.ShapeDtypeStruct(_view_shape(q.shape, planes, dilation), BF16),
                   jax.ShapeDtypeStruct(_view_shape(lse_shape, planes, dilation), F32)],
        compiler_params=pltpu.CompilerParams(
            dimension_semantics=("parallel", "parallel", "arbitrary"),
            vmem_limit_bytes=VMEM_LIMIT),
        name=f"dilated_attn_r{dilation}",
    )(bounded, view(q), view(k), view(v), view(k), view(v), *extra_args)
    return o.reshape(q.shape), lse.reshape(lse_shape)


def _view_shape(shape, planes, dilation):
    bsz, _, rows, width = shape
    return (bsz, planes, dilation, rows, width) if planes == 4 else shape


def _outproj_kernel(x_ref, mod_ref, ys_ref, o_ref, st_ref, nw_ref, e2_ref, w_ref, out_ref,
                    wide_ref, wide2_ref, acc_ref):
    tm = x_ref.shape[1]
    cls_rows = tm // CLS_STRIDE
    k_early = D_INNER // 2
    acc_ref[...] = _dot(ys_ref[0, :, 0:k_early], w_ref[0:k_early, :])

    def regrouped(cls_ref):
        return cls_ref[0].reshape(tm, cls_ref.shape[-1])

    def natural(val):
        tiles = wide_ref.shape[0]
        quarter = cls_rows * 4
        for res in range(CLS_STRIDE):
            rho, a = res % 4, res // 4
            for t in range(tiles):
                wide2_ref[t, pl.ds(rho * quarter + a, cls_rows, stride=4), :] = (
                    val[res * cls_rows:(res + 1) * cls_rows, t * LANES:(t + 1) * LANES])
        for t in range(tiles):
            for rho in range(4):
                wide_ref[t, pl.ds(rho, quarter, stride=4), :] = (
                    wide2_ref[t, rho * quarter:(rho + 1) * quarter, :])
        return jnp.concatenate([wide_ref[t] for t in range(tiles)], axis=1)

    st = regrouped(st_ref)
    head_lane = lax.broadcasted_iota(jnp.int32, st.shape, 1) < N_HEADS
    hi, lo = _split_bf16(jnp.where(head_lane, 1.0 / st, 0.0))
    inv_x = _dot(jnp.concatenate([hi, lo], axis=1), e2_ref[...])
    o = inv_x * regrouped(o_ref).astype(F32)
    ms = jnp.mean(o * o, axis=-1, keepdims=True)
    y_att = (o * lax.rsqrt(ms + EPS) * nw_ref[...]).astype(BF16)
    att = _dot(y_att, w_ref[D_INNER:, :])
    late = _dot(ys_ref[0, :, k_early:D_INNER], w_ref[k_early:D_INNER, :])
    mix = acc_ref[...] + late + natural(att)
    out_ref[0] = x_ref[0] + mod_ref[0, 2:3, :] * mix


def _out_projection(x, mod3, y_ssd, att_acc, att_stats, norm_w, e2, w_out, tm):
    bsz, s, d = x.shape
    row = lambda b, i: (b, i, 0)
    wide = pl.BlockSpec((1, tm, D_INNER), row)
    cls = lambda width: pl.BlockSpec((1, CLS_STRIDE, tm // CLS_STRIDE, width),
                                     lambda b, i: (b, 0, i, 0))
    return pl.pallas_call(
        _outproj_kernel,
        grid=(bsz, s // tm),
        in_specs=[
            pl.BlockSpec((1, tm, d), row),
            pl.BlockSpec((1, N_MOD, d), lambda b, i: (b, 0, 0)),
            wide, cls(D_INNER), cls(LANES),
            _const_spec((1, D_INNER)),
            _const_spec((2 * LANES, D_INNER)),
            _const_spec((2 * D_INNER, d)),
        ],
        out_specs=pl.BlockSpec((1, tm, d), row),
        out_shape=jax.ShapeDtypeStruct((bsz, s, d), F32),
        scratch_shapes=[pltpu.VMEM((d // LANES, tm, LANES), F32),
                        pltpu.VMEM((d // LANES, tm, LANES), F32),
                        pltpu.VMEM((tm, d), F32)],
        compiler_params=pltpu.CompilerParams(
            dimension_semantics=("parallel", "arbitrary"), vmem_limit_bytes=VMEM_LIMIT),
        name="out_proj",
    )(x, mod3, y_ssd, att_acc, att_stats, norm_w, e2, w_out)


FF_TN = 1024


def _mlp_kernel(x_ref, mod_ref, nw_ref, w1_ref, w2_ref, out_ref, h_ref):
    x = x_ref[0]
    h = _norm_mod(x, nw_ref[...], mod_ref[0, 4:5, :], mod_ref[0, 3:4, :])
    h_ref[...] = h.astype(BF16)
    acc = jnp.zeros(x.shape, F32)
    for j in range(D_FF // FF_TN):
        fs = slice(j * FF_TN, (j + 1) * FF_TN)
        u = jnp.maximum(_dot(h_ref[...], w1_ref[:, fs]), 0.0)
        acc = acc + _dot((u * u).astype(BF16), w2_ref[fs, :])
    out_ref[0] = x + mod_ref[0, 5:6, :] * acc


def _mlp(x, mod3, norm_w, w1, w2, tm):
    bsz, s, d = x.shape
    row = lambda b, i: (b, i, 0)
    return pl.pallas_call(
        _mlp_kernel,
        grid=(bsz, s // tm),
        in_specs=[
            pl.BlockSpec((1, tm, d), row),
            pl.BlockSpec((1, N_MOD, d), lambda b, i: (b, 0, 0)),
            _const_spec((1, d)),
            _const_spec((d, D_FF)),
            _const_spec((D_FF, d)),
        ],
        out_specs=pl.BlockSpec((1, tm, d), row),
        out_shape=jax.ShapeDtypeStruct((bsz, s, d), F32),
        scratch_shapes=[pltpu.VMEM((tm, d), BF16)],
        compiler_params=pltpu.CompilerParams(
            dimension_semantics=("parallel", "arbitrary"), vmem_limit_bytes=VMEM_LIMIT),
        name="mlp",
    )(x, mod3, norm_w, w1, w2)


def _head_expand_matrix():
    head_of_lane = jnp.arange(D_INNER) // HEAD_DIM
    e = (jnp.arange(LANES)[:, None] == head_of_lane[None, :]).astype(BF16)
    return jnp.concatenate([e, e], axis=0)


def _pad_lanes(v, n=LANES):
    return jnp.pad(v.astype(F32), (0, n - v.shape[0])).reshape(1, n)


def kernel(x, c, norm1_w, norm2_w, w_ada, b_ada, w_in, conv_w, conv_b, dt_bias, a_log, d_skip,
           ssd_norm_w, q_norm_w, k_norm_w, attn_norm_w, w_out, w_ff1, w_ff2):
    bsz, s, d = x.shape
    depth = w_ada.shape[0]
    tm = ROW_TILE
    assert d == D_MODEL and s % ROW_TILE == 0 and s % TAIL_ROW_TILE == 0 and s % SSD_ROWS == 0
    assert s % (CLS_STRIDE * ATT_BLK) == 0
    e2 = _head_expand_matrix()
    idx = jnp.arange(MXU_TILE) // HEAD_DIM
    bd = (idx[:, None] == idx[None, :]).astype(BF16)
    t = (jnp.arange(SSD_CHUNK)[:, None] >= jnp.arange(SSD_CHUNK)[None, :]).astype(BF16)
    tril2 = jnp.concatenate([t, t], axis=1)
    o_xbc = D_INNER + CONV_CH
    o_dt = o_xbc + N_HEADS

    for l in range(depth):
        mod3 = _modulation(c, w_ada[l], b_ada[l]).reshape(bsz, N_MOD, d)
        w_zx, w_qkv, w_dt = _split_in_weight(w_in[l])
        z, xbc, dt, q16, k16, v16 = _in_projection(
            x, mod3, norm1_w[l].reshape(1, d), w_zx, w_qkv, w_dt, _pad_lanes(dt_bias[l]),
            jnp.tile(q_norm_w[l], N_HEADS).reshape(1, D_INNER),
            jnp.tile(k_norm_w[l], N_HEADS).reshape(1, D_INNER), bd, tm)
        y_ssd = _ssd_mixer(
            xbc, z, dt, conv_w[l], conv_b[l].reshape(1, CONV_CH),
            _pad_lanes(a_log[l]), jnp.repeat(d_skip[l], HEAD_DIM).reshape(1, D_INNER),
            ssd_norm_w[l].reshape(1, D_INNER), tril2, e2)
        score_bound = (HEAD_DIM ** 0.5 * LOG2E * jnp.max(jnp.abs(q_norm_w[l]))
                       * jnp.max(jnp.abs(k_norm_w[l])))
        bounded = (score_bound * BF16_SLACK <= SCORE_BOUND).astype(jnp.int32).reshape(1)
        running = None
        for window, dilation in DILATED_PATTERNS:
            assert window // dilation == ATT_BLK
            running = _dilated_attention(bounded, q16, k16, v16, dilation, e2, running)
        x = _out_projection(x, mod3, y_ssd, *running, attn_norm_w[l].reshape(1, D_INNER),
                            e2, w_out[l].astype(BF16), TAIL_ROW_TILE)
        x = _mlp(x, mod3, norm2_w[l].reshape(1, d), w_ff1[l].astype(BF16),
                 w_ff2[l].astype(BF16), TAIL_ROW_TILE)
    return x.astype(c.dtype)
```

```python
import functools

import jax
import jax.numpy as jnp
from jax import lax
from jax.experimental import pallas as pl
from jax.experimental.pallas import tpu as pltpu

F32 = jnp.float32
BF16 = jnp.bfloat16

D_MODEL = 1024
HEAD_DIM = 64
N_HEADS = 16
SSD_GROUPS = 4
SSD_STATE = 128
SSD_CONV = 4
SSD_CHUNK = 128
D_INNER = N_HEADS * HEAD_DIM
CONV_CH = D_INNER + 2 * SSD_GROUPS * SSD_STATE
D_FF = 4 * D_MODEL
N_MOD = 6
EPS = 1e-6
DILATED_PATTERNS = ((128, 1), (512, 4), (2048, 16))
ATT_BLK = 128
ATT_GROUP = 8
CLS_STRIDE = 16
ROW_TILE = 512
TAIL_ROW_TILE = 1024
LANES = 128
MXU_TILE = 256
VMEM_LIMIT = 56 * 1024 * 1024

LOG2E = 1.4426950408889634
SCORE_BOUND = 60.0
BF16_SLACK = 1.02

NT_DIMS = (((1,), (1,)), ((), ()))
TN_DIMS = (((0,), (0,)), ((), ()))


def _dot(a, b):
    return jnp.dot(a, b, preferred_element_type=F32)


def _split_bf16(v):
    hi = v.astype(BF16)
    lo = (v - hi.astype(F32)).astype(BF16)
    return hi, lo


def _silu(v):
    h = 0.5 * v
    return h + h * jnp.tanh(h)


def _const_spec(shape):
    nd = len(shape)
    return pl.BlockSpec(shape, lambda *_: (0,) * nd, pipeline_mode=pl.Buffered(1))


def _mod_kernel(c_ref, w_ref, b_ref, o_ref):
    c = c_ref[...]
    ca = _silu(c)
    c_hi, c_lo = _split_bf16(ca)
    w = w_ref[...]
    w_hi, w_lo = _split_bf16(w)
    acc = _dot(c_hi, w_hi) + _dot(c_lo, w_hi) + _dot(c_hi, w_lo)
    o_ref[...] = acc + b_ref[...]


def _modulation(c, w_ada, b_ada):
    bsz, d = c.shape
    n = w_ada.shape[1]
    tn = n // 4
    return pl.pallas_call(
        _mod_kernel,
        grid=(n // tn,),
        in_specs=[
            pl.BlockSpec((bsz, d), lambda j: (0, 0)),
            pl.BlockSpec((d, tn), lambda j: (0, j)),
            pl.BlockSpec((1, tn), lambda j: (0, j)),
        ],
        out_specs=pl.BlockSpec((bsz, tn), lambda j: (0, j)),
        out_shape=jax.ShapeDtypeStruct((bsz, n), F32),
        compiler_params=pltpu.CompilerParams(
            dimension_semantics=("arbitrary",), vmem_limit_bytes=VMEM_LIMIT),
        name="adaln_mod",
    )(c, w_ada, b_ada.reshape(1, n))


W_SPLIT_TN = 512


def _wsplit_kernel(wt_ref, wall_ref, wdt_ref):
    j = pl.program_id(0)
    o_xbc = D_INNER + CONV_CH
    start = j * W_SPLIT_TN + jnp.where(j * W_SPLIT_TN >= o_xbc, N_HEADS, 0)
    rows = wt_ref[pl.ds(pl.multiple_of(start, 8), W_SPLIT_TN), :]
    wall_ref[...] = rows.T.astype(BF16)
    dt_rows = jnp.concatenate(
        [wt_ref[o_xbc:o_xbc + N_HEADS, :], jnp.zeros((LANES - N_HEADS, wt_ref.shape[1]), F32)],
        axis=0)
    wdt_ref[...] = dt_rows.T.astype(BF16)


def _split_in_weight(wt):
    n, d = wt.shape
    n_out = n - N_HEADS
    return pl.pallas_call(
        _wsplit_kernel,
        grid=(n_out // W_SPLIT_TN,),
        in_specs=[_const_spec((n, d))],
        out_specs=[pl.BlockSpec((d, W_SPLIT_TN), lambda j: (0, j)),
                   pl.BlockSpec((d, LANES), lambda j: (0, 0))],
        out_shape=[jax.ShapeDtypeStruct((d, n_out), BF16),
                   jax.ShapeDtypeStruct((d, LANES), BF16)],
        compiler_params=pltpu.CompilerParams(
            dimension_semantics=("arbitrary",), vmem_limit_bytes=VMEM_LIMIT),
        name="w_in_split",
    )(wt)


IN_TN = 512


def _norm_mod(x, nw, scale, shift):
    ms = jnp.mean(x * x, axis=-1, keepdims=True)
    return (x * lax.rsqrt(ms + EPS) * nw) * (1.0 + scale) + shift


def _inproj_kernel(x_ref, mod_ref, nw_ref, wzx_ref, wqkv_ref, wdt_ref, dtb_ref, qw_ref, kw_ref,
                   bd_ref, z_ref, xbc_ref, dt_ref, q16_ref, k16_ref, v16_ref,
                   h_ref, h16_ref, stage_ref, stage2_ref):
    x = x_ref[0]
    tm = x.shape[0]
    cls_rows = tm // CLS_STRIDE
    h = _norm_mod(x, nw_ref[...], mod_ref[0, 1:2, :], mod_ref[0, 0:1, :])
    h_ref[...] = h.astype(BF16)

    tiles = h.shape[1] // LANES
    quarter = tm // 4
    for t in range(tiles):
        stage_ref[t] = h[:, t * LANES:(t + 1) * LANES]
    for t in range(tiles):
        for rho in range(4):
            stage2_ref[t, rho * quarter:(rho + 1) * quarter, :] = (
                stage_ref[t, pl.ds(rho, quarter, stride=4), :])
    for res in range(CLS_STRIDE):
        rho, a = res % 4, res // 4
        rows = [stage2_ref[t, pl.ds(rho * quarter + a, cls_rows, stride=4), :]
                for t in range(tiles)]
        h16_ref[res * cls_rows:(res + 1) * cls_rows, :] = (
            jnp.concatenate(rows, axis=1).astype(BF16))

    def proj(w_ref, c0, width, lhs_ref=h_ref):
        return _dot(lhs_ref[...], w_ref[:, c0:c0 + width])

    def qk_norm(acc, w):
        sq = (acc * acc).astype(BF16)
        parts = [_dot(sq[:, i:i + MXU_TILE], bd_ref[...]) for i in range(0, IN_TN, MXU_TILE)]
        ss = jnp.concatenate(parts, axis=1)
        return acc * lax.rsqrt(ss * (1.0 / HEAD_DIM) + EPS) * w

    def emit(cls_ref, sl, val):
        for res in range(CLS_STRIDE):
            cls_ref[0, res, :, sl] = val[res * cls_rows:(res + 1) * cls_rows].astype(BF16)

    col = 0
    for j in range(D_INNER // IN_TN):
        sl = slice(j * IN_TN, (j + 1) * IN_TN)
        emit(q16_ref, sl, qk_norm(proj(wqkv_ref, col, IN_TN, h16_ref), qw_ref[:, sl])
             * (HEAD_DIM ** -0.5 * LOG2E))
        col += IN_TN
    for j in range(D_INNER // IN_TN):
        sl = slice(j * IN_TN, (j + 1) * IN_TN)
        emit(k16_ref, sl, qk_norm(proj(wqkv_ref, col, IN_TN, h16_ref), kw_ref[:, sl]))
        col += IN_TN
    for j in range(D_INNER // IN_TN):
        emit(v16_ref, slice(j * IN_TN, (j + 1) * IN_TN), proj(wqkv_ref, col, IN_TN, h16_ref))
        col += IN_TN
    dt_raw = proj(wdt_ref, 0, LANES) + dtb_ref[...]
    dt_ref[0] = jnp.maximum(dt_raw, 0.0) + jnp.log(1.0 + jnp.exp(-jnp.abs(dt_raw)))
    col = D_INNER
    for j in range(CONV_CH // IN_TN):
        xbc_ref[0, :, j * IN_TN:(j + 1) * IN_TN] = proj(wzx_ref, col, IN_TN).astype(BF16)
        col += IN_TN
    col = 0
    for j in range(D_INNER // IN_TN):
        z_ref[0, :, j * IN_TN:(j + 1) * IN_TN] = proj(wzx_ref, col, IN_TN).astype(BF16)
        col += IN_TN


def _in_projection(x, mod3, norm_w, w_zx, w_qkv, w_dt, dt_bias, qw, kw, bd, tm):
    bsz, s, d = x.shape
    row = lambda b, i: (b, i, 0)
    out_bf = lambda n: jax.ShapeDtypeStruct((bsz, s, n), BF16)
    cls_spec = pl.BlockSpec((1, CLS_STRIDE, tm // CLS_STRIDE, D_INNER), lambda b, i: (b, 0, i, 0))
    cls_shape = jax.ShapeDtypeStruct((bsz, CLS_STRIDE, s // CLS_STRIDE, D_INNER), BF16)
    return pl.pallas_call(
        _inproj_kernel,
        grid=(bsz, s // tm),
        in_specs=[
            pl.BlockSpec((1, tm, d), row),
            pl.BlockSpec((1, N_MOD, d), lambda b, i: (b, 0, 0)),
            _const_spec((1, d)),
            pl.BlockSpec((d, D_INNER + CONV_CH), lambda b, i: (0, 0),
                         pipeline_mode=pl.Buffered(1)),
            pl.BlockSpec((d, 3 * D_INNER), lambda b, i: (0, 1), pipeline_mode=pl.Buffered(1)),
            _const_spec(w_dt.shape),
            _const_spec((1, LANES)),
            _const_spec((1, D_INNER)),
            _const_spec((1, D_INNER)),
            _const_spec((MXU_TILE, MXU_TILE)),
        ],
        out_specs=[
            pl.BlockSpec((1, tm, D_INNER), row),
            pl.BlockSpec((1, tm, CONV_CH), row),
            pl.BlockSpec((1, tm, LANES), row),
            cls_spec, cls_spec, cls_spec,
        ],
        out_shape=[out_bf(D_INNER), out_bf(CONV_CH),
                   jax.ShapeDtypeStruct((bsz, s, LANES), F32),
                   cls_shape, cls_shape, cls_shape],
        scratch_shapes=[pltpu.VMEM((tm, d), BF16),
                        pltpu.VMEM((tm, d), BF16),
                        pltpu.VMEM((d // LANES, tm, LANES), F32),
                        pltpu.VMEM((d // LANES, tm, LANES), F32)],
        compiler_params=pltpu.CompilerParams(
            dimension_semantics=("parallel", "arbitrary"), vmem_limit_bytes=VMEM_LIMIT),
        name="in_proj",
    )(x, mod3, norm_w, w_zx, w_qkv, w_dt, dt_bias, qw, kw, bd)


SSD_ROWS = 1024
CONV_HALO = 8


def _ssd_kernel(xbc_ref, z_ref, dt_ref, cw_ref, cb_ref, alog_ref, dskip_ref,
                nw_ref, tril2_ref, e2_ref, shift_ref, o_ref, halo_ref, xc_ref, state_ref):
    rows = xbc_ref.shape[1]
    L = SSD_CHUNK
    taps = SSD_CONV - 1

    @pl.when(pl.program_id(1) == 0)
    def _():
        halo_ref[...] = jnp.zeros_like(halo_ref)
        state_ref[...] = jnp.zeros_like(state_ref)

    hsub = lax.broadcasted_iota(jnp.int32, (CONV_HALO, MXU_TILE), 0)
    group_w = D_INNER // SSD_GROUPS
    heads_per_group = N_HEADS // SSD_GROUPS

    def conv_chunk(c):
        r0 = c * L
        for c0 in range(0, CONV_CH, MXU_TILE):
            cs = slice(c0, c0 + MXU_TILE)
            u = xbc_ref[0, r0:r0 + L, cs]
            shifted = _dot(shift_ref[...], u)
            uf = u.astype(F32)
            acc = cb_ref[:, cs] + cw_ref[taps:taps + 1, cs] * uf
            for k in range(1, taps + 1):
                acc = acc + cw_ref[taps - k:taps - k + 1, cs] * shifted[(k - 1) * L:k * L]
            xc_ref[r0:r0 + L, cs] = _silu(acc)
            if c == 0:
                halo = halo_ref[:, cs]
            else:
                halo = xbc_ref[0, r0 - 2 * CONV_HALO:r0, cs].astype(F32)[CONV_HALO:]
            head = acc[0:CONV_HALO]
            for k in range(1, taps + 1):
                prev = jnp.where(hsub < k, pltpu.roll(halo, k, axis=0), 0.0)
                head = head + cw_ref[taps - k:taps - k + 1, cs] * prev
            xc_ref[r0:r0 + CONV_HALO, cs] = _silu(head)
            if r0 + L == rows:
                halo_ref[:, cs] = uf[L - CONV_HALO:L]

    lane = lax.broadcasted_iota(jnp.int32, (L, LANES), 1)
    sub = lax.broadcasted_iota(jnp.int32, (L, LANES), 0)
    tril = sub >= lane
    lo_half = lane < HEAD_DIM
    a_neg = jnp.where(lane[0:1] < N_HEADS, -jnp.exp(alog_ref[...]), 0.0)

    def split_cat(v):
        hi, lo = _split_bf16(v)
        return jnp.concatenate([hi, lo], axis=1)

    def expand(v_cat, gs):
        return _dot(v_cat, e2_ref[:, gs])

    conv_chunk(0)
    for c in range(rows // L):
        if c + 1 < rows // L:
            conv_chunk(c + 1)
        r0 = c * L
        rs = slice(r0, r0 + L)
        dt = dt_ref[0, rs, :]
        d_a = dt * a_neg
        da_hi, da_lo = _split_bf16(d_a)
        a_cs = _dot(tril2_ref[...], jnp.concatenate([da_hi, da_lo], axis=0))
        a_cs_t = a_cs.T
        a_last = a_cs[L - 1:L, :]
        dt_cat = split_cat(dt)
        w_cat = split_cat(dt * jnp.exp(a_last - a_cs))
        do_cat = split_cat(jnp.exp(a_cs))

        for g in range(SSD_GROUPS):
            gs = slice(g * group_w, (g + 1) * group_w)
            xs = xc_ref[rs, gs]
            do_x = expand(do_cat, gs)
            xdt = (xs * expand(dt_cat, gs)).astype(BF16)
            xw = (xs * expand(w_cat, gs)).astype(BF16)
            b0 = D_INNER + g * SSD_STATE
            c0 = D_INNER + SSD_GROUPS * SSD_STATE + g * SSD_STATE
            bg = xc_ref[rs, b0:b0 + SSD_STATE].astype(BF16)
            cg = xc_ref[rs, c0:c0 + SSD_STATE].astype(BF16)
            cb = lax.dot_general(cg, bg, NT_DIMS, preferred_element_type=F32)
            yd_parts = []
            for pair in range(heads_per_group // 2):
                ms = []
                for e in range(2):
                    hd = g * heads_per_group + pair * 2 + e
                    colb = jnp.broadcast_to(a_cs[:, hd:hd + 1], (L, L))
                    rowb = jnp.broadcast_to(a_cs_t[hd:hd + 1, :], (L, L))
                    lmat = jnp.exp(jnp.where(tril, colb - rowb, -jnp.inf))
                    ms.append((cb * lmat).astype(BF16))
                mcat = jnp.concatenate(ms, axis=1)
                xp = xdt[:, pair * LANES:(pair + 1) * LANES]
                zero = jnp.zeros_like(xp)
                xstack = jnp.concatenate(
                    [jnp.where(lo_half, xp, zero), jnp.where(lo_half, zero, xp)], axis=0)
                yd_parts.append(_dot(mcat, xstack))
            y_diag = jnp.concatenate(yd_parts, axis=1)
            st = state_ref[:, gs]
            y_off = _dot(cg, st.astype(BF16)) * do_x
            upd = lax.dot_general(bg, xw, TN_DIMS, preferred_element_type=F32)
            state_ref[:, gs] = st * do_x[L - 1:L] + upd
            gate = _silu(z_ref[0, rs, gs].astype(F32))
            yg = (dskip_ref[:, gs] * xs + y_diag + y_off) * gate
            ss = jnp.mean(yg * yg, axis=-1, keepdims=True)
            o_ref[0, rs, gs] = (yg * lax.rsqrt(ss + EPS) * nw_ref[:, gs]).astype(BF16)


def _ssd_mixer(xbc, z, dt, conv_w, conv_b, a_log, d_skip, norm_w, tril2, e2):
    bsz, s, _ = xbc.shape
    rows = SSD_ROWS
    row = lambda b, i: (b, i, 0)
    t = jnp.arange(SSD_CHUNK)
    shift = jnp.concatenate(
        [(t[:, None] - k == t[None, :]).astype(BF16) for k in range(1, SSD_CONV)], axis=0)
    return pl.pallas_call(
        _ssd_kernel,
        grid=(bsz, s // rows),
        in_specs=[
            pl.BlockSpec((1, rows, CONV_CH), row),
            pl.BlockSpec((1, rows, D_INNER), row),
            pl.BlockSpec((1, rows, LANES), row),
            _const_spec((SSD_CONV, CONV_CH)),
            _const_spec((1, CONV_CH)),
            _const_spec((1, LANES)),
            _const_spec((1, D_INNER)),
            _const_spec((1, D_INNER)),
            _const_spec((SSD_CHUNK, 2 * SSD_CHUNK)),
            _const_spec((2 * LANES, D_INNER)),
            _const_spec(((SSD_CONV - 1) * SSD_CHUNK, SSD_CHUNK)),
        ],
        out_specs=pl.BlockSpec((1, rows, D_INNER), row),
        out_shape=jax.ShapeDtypeStruct((bsz, s, D_INNER), BF16),
        scratch_shapes=[
            pltpu.VMEM((CONV_HALO, CONV_CH), F32),
            pltpu.VMEM((rows, CONV_CH), F32),
            pltpu.VMEM((SSD_STATE, D_INNER), F32),
        ],
        compiler_params=pltpu.CompilerParams(
            dimension_semantics=("parallel", "arbitrary"), vmem_limit_bytes=VMEM_LIMIT),
        name="ssd_mixer",
    )(xbc, z, dt, conv_w, conv_b, a_log, d_skip, norm_w, tril2, e2, shift)


def _attn_kernel(planes, group, chain, bounded_ref, q_ref, k_ref, v_ref, kp_ref, vp_ref, *refs):
    if chain:
        acc_in_ref, st_in_ref, e2_ref, o_ref, lse_ref = refs
        assert planes in (1, 4)
    else:
        o_ref, lse_ref = refs
    blk = ATT_BLK
    sub = blk // planes
    first = pl.program_id(2) == 0

    def pos(i):
        return i if planes == 1 else (i % sub) * planes + i // sub

    row = lax.broadcasted_iota(jnp.int32, (blk, 2 * blk), 0)
    key = lax.broadcasted_iota(jnp.int32, (blk, 2 * blk), 1)
    rel = jnp.where(key >= blk, blk + pos(key - blk), pos(key)) - pos(row)
    band = (rel >= 0) & (rel <= blk)
    lo_key = jnp.where(first, blk, 0)
    bias_head = jnp.where(band & (key >= lo_key), 0.0, -jnp.inf)
    bias_head = jnp.concatenate([bias_head, bias_head], axis=0)
    bias_rest = jnp.where(band, 0.0, -jnp.inf)
    bias_rest = jnp.concatenate([bias_rest, bias_rest], axis=0)
    lane = lax.broadcasted_iota(jnp.int32, (blk, LANES), 1)
    lo_half = lane < HEAD_DIM

    n_res = q_ref.shape[1] if planes == 1 else 1

    half = 2 * sub

    def load(ref, r, g, ps):
        if planes == 1:
            return ref[0, r, g * blk:(g + 1) * blk, ps]
        if planes == 4:
            return ref[0, :, 0, g * sub:(g + 1) * sub, ps].reshape(blk, ps.stop - ps.start)
        rows = ref[0, :, (g // 2) * half:(g // 2 + 1) * half, ps].astype(F32)
        return rows[:, (g % 2) * sub:(g % 2 + 1) * sub, :].reshape(blk, LANES).astype(BF16)

    def store_cols(ref, r, g, ps, val):
        if planes == 1:
            ref[0, r, g * blk:(g + 1) * blk, ps] = val
        elif planes == 4:
            ref[0, :, 0, g * sub:(g + 1) * sub, ps] = val.reshape(planes, sub, val.shape[-1])
        else:
            ref[0, :, g * sub:(g + 1) * sub, ps] = val.reshape(planes, sub, val.shape[-1])

    pending = {}

    def store_out(r, g, ps, val):
        if planes == 1:
            o_ref[0, r, g * blk:(g + 1) * blk, ps] = val.astype(BF16)
        elif planes == 4:
            o_ref[0, :, 0, g * sub:(g + 1) * sub, ps] = (
                val.reshape(planes, sub, val.shape[-1]).astype(BF16))
        elif g % 2 == 0:
            pending[ps.start] = val.reshape(planes, sub, LANES)
        else:
            both = jnp.concatenate(
                [pending.pop(ps.start), val.reshape(planes, sub, LANES)], axis=1)
            o_ref[0, :, (g // 2) * half:(g // 2 + 1) * half, ps] = both.astype(BF16)

    def pair(bounded, r, g, p):
        ps = slice(p * LANES, (p + 1) * LANES)
        qp = load(q_ref, r, g, ps)
        zero = jnp.zeros_like(qp)
        qs = jnp.concatenate(
            [jnp.where(lo_half, qp, zero), jnp.where(lo_half, zero, qp)], axis=0)
        if g == 0:
            last = kp_ref.shape[-2] // sub - 1
            k_prev, v_prev = load(kp_ref, r, last, ps), load(vp_ref, r, last, ps)
        else:
            k_prev, v_prev = load(k_ref, r, g - 1, ps), load(v_ref, r, g - 1, ps)
        k2 = jnp.concatenate([k_prev, load(k_ref, r, g, ps)], axis=0)
        v2 = jnp.concatenate([v_prev, load(v_ref, r, g, ps)], axis=0)
        s = lax.dot_general(qs, k2, NT_DIMS, preferred_element_type=F32)
        s = s + (bias_head if g == 0 else bias_rest)
        if bounded:
            e = jnp.exp2(s)
        else:
            m = jnp.max(s, axis=-1, keepdims=True)
            e = jnp.exp2(s - m)
        l = jnp.sum(e, axis=-1, keepdims=True)
        pv = _dot(e.astype(BF16), v2)
        out = jnp.where(lo_half, pv[0:blk], pv[blk:])
        if chain and bounded:
            out = out + load(acc_in_ref, r, g, ps).astype(F32)
        store_out(r, g, ps, out)
        store_cols(lse_ref, r, g, slice(2 * p, 2 * p + 1), l[0:blk])
        store_cols(lse_ref, r, g, slice(2 * p + 1, 2 * p + 2), l[blk:])
        if not bounded:
            c0 = N_HEADS + 2 * p
            store_cols(lse_ref, r, g, slice(c0, c0 + 1), m[0:blk])
            store_cols(lse_ref, r, g, slice(c0 + 1, c0 + 2), m[blk:])

    def merge_block(r, g):
        everything = slice(0, LANES)
        head_lane = lane < N_HEADS
        s_own, s_prev = load(lse_ref, r, g, everything), load(st_in_ref, r, g, everything)
        m_own = pltpu.roll(s_own, LANES - N_HEADS, axis=1)
        m_prev = pltpu.roll(s_prev, LANES - N_HEADS, axis=1)
        m_new = jnp.maximum(m_prev, m_own)
        a = jnp.where(head_lane, jnp.exp2(m_prev - m_new), 0.0)
        b = jnp.where(head_lane, jnp.exp2(m_own - m_new), 0.0)
        stats = (jnp.where(head_lane, a * s_prev + b * s_own, 0.0)
                 + pltpu.roll(jnp.where(head_lane, m_new, 0.0), N_HEADS, axis=1))
        store_cols(lse_ref, r, g, everything, stats)

        def expand(v):
            hi, lo = _split_bf16(v)
            return _dot(jnp.concatenate([hi, lo], axis=1), e2_ref[...])

        wide = slice(0, o_ref.shape[-1])
        merged = (expand(a) * load(acc_in_ref, r, g, wide).astype(F32)
                  + expand(b) * load(o_ref, r, g, wide).astype(F32))
        store_out(r, g, wide, merged)

    def body(bounded):
        lse_ref[...] = jnp.zeros(lse_ref.shape, F32)
        for r in range(n_res):
            if planes == 16:
                order = [(g0 + h, p) for g0 in range(0, group, 2)
                         for p in range(N_HEADS // 2) for h in range(2)]
            else:
                order = [(g, p) for g in range(group) for p in range(N_HEADS // 2)]
            for g, p in order:
                pair(bounded, r, g, p)
                if chain and not bounded and p == N_HEADS // 2 - 1:
                    merge_block(r, g)
        if chain and bounded:
            lse_ref[...] = lse_ref[...] + st_in_ref[...]

    bounded = bounded_ref[0] != 0

    @pl.when(bounded)
    def _():
        body(True)

    @pl.when(jnp.logical_not(bounded))
    def _():
        body(False)


def _dilated_attention(bounded, q, k, v, dilation, e2, running=None):
    bsz, n_planes, rows, w = q.shape
    planes = n_planes // dilation
    assert planes in (1, 4, 16)
    sub = ATT_BLK // planes
    nb = rows // sub
    group = min(ATT_GROUP, nb)
    assert rows % sub == 0 and nb % group == 0
    n_res = 1
    window = 1
    if planes == 1:
        n_res = min(ATT_GROUP // group, dilation)
        view = lambda t: t
        blk_shape = lambda n, width: (1, n_res, n * ATT_BLK, width)
        cur = lambda b, res, j: (b, res, j, 0)
        prev = lambda b, res, j: (b, res, jnp.maximum(j * group - 1, 0), 0)
    elif planes == 4:
        view = lambda t: t.reshape(bsz, planes, dilation, rows, t.shape[-1])
        blk_shape = lambda n, width: (1, planes, 1, n * sub, width)
        cur = lambda b, res, j: (b, 0, res, j, 0)
        prev = lambda b, res, j: (b, 0, res, jnp.maximum(j * group - 1, 0), 0)
    else:
        assert group % 2 == 0
        window = 2
        view = lambda t: t
        blk_shape = lambda n, width: (1, planes, n * sub, width)
        cur = lambda b, res, j: (b, 0, j, 0)
        prev = lambda b, res, j: (b, 0, jnp.maximum(j * (group // 2) - 1, 0), 0)
    tile = lambda width=w: pl.BlockSpec(blk_shape(group, width), cur)
    single = pl.BlockSpec(blk_shape(window, w), prev)
    lse_shape = (bsz, n_planes, rows, LANES)
    chain = running is not None
    extra_specs = [tile(), tile(LANES), _const_spec(e2.shape)] if chain else []
    extra_args = [view(running[0]), view(running[1]), e2] if chain else []
    o, lse = pl.pallas_call(
        functools.partial(_attn_kernel, planes, group, chain),
        grid=(bsz, dilation // n_res, nb // group),
        in_specs=[pl.BlockSpec(memory_space=pltpu.SMEM), tile(), tile(), tile(), single, single,
                  *extra_specs],
        out_specs=[tile(), tile(LANES)],
        out_shape=[jax.ShapeDtypeStruct(_view_shape(q.shape, planes, dilation), BF16),
                   jax.ShapeDtypeStruct(_view_shape(lse_shape, planes, dilation), F32)],
        compiler_params=pltpu.CompilerParams(
            dimension_semantics=("parallel", "parallel", "arbitrary"),
            vmem_limit_bytes=VMEM_LIMIT),
        name=f"dilated_attn_r{dilation}",
    )(bounded, view(q), view(k), view(v), view(k), view(v), *extra_args)
    return o.reshape(q.shape), lse.reshape(lse_shape)


def _view_shape(shape, planes, dilation):
    bsz, _, rows, width = shape
    return (bsz, planes, dilation, rows, width) if planes == 4 else shape


def _outproj_kernel(x_ref, mod_ref, ys_ref, o_ref, st_ref, nw_ref, e2_ref, w_ref, out_ref,
                    wide_ref, wide2_ref, acc_ref):
    tm = x_ref.shape[1]
    cls_rows = tm // CLS_STRIDE
    k_early = D_INNER // 2
    acc_ref[...] = _dot(ys_ref[0, :, 0:k_early], w_ref[0:k_early, :])

    def regrouped(cls_ref):
        return cls_ref[0].reshape(tm, cls_ref.shape[-1])

    def natural(val):
        tiles = wide_ref.shape[0]
        quarter = cls_rows * 4
        for res in range(CLS_STRIDE):
            rho, a = res % 4, res // 4
            for t in range(tiles):
                wide2_ref[t, pl.ds(rho * quarter + a, cls_rows, stride=4), :] = (
                    val[res * cls_rows:(res + 1) * cls_rows, t * LANES:(t + 1) * LANES])
        for t in range(tiles):
            for rho in range(4):
                wide_ref[t, pl.ds(rho, quarter, stride=4), :] = (
                    wide2_ref[t, rho * quarter:(rho + 1) * quarter, :])
        return jnp.concatenate([wide_ref[t] for t in range(tiles)], axis=1)

    st = regrouped(st_ref)
    head_lane = lax.broadcasted_iota(jnp.int32, st.shape, 1) < N_HEADS
    hi, lo = _split_bf16(jnp.where(head_lane, 1.0 / st, 0.0))
    inv_x = _dot(jnp.concatenate([hi, lo], axis=1), e2_ref[...])
    o = inv_x * regrouped(o_ref).astype(F32)
    ms = jnp.mean(o * o, axis=-1, keepdims=True)
    y_att = (o * lax.rsqrt(ms + EPS) * nw_ref[...]).astype(BF16)
    att = _dot(y_att, w_ref[D_INNER:, :])
    late = _dot(ys_ref[0, :, k_early:D_INNER], w_ref[k_early:D_INNER, :])
    mix = acc_ref[...] + late + natural(att)
    out_ref[0] = x_ref[0] + mod_ref[0, 2:3, :] * mix


def _out_projection(x, mod3, y_ssd, att_acc, att_stats, norm_w, e2, w_out, tm):
    bsz, s, d = x.shape
    row = lambda b, i: (b, i, 0)
    wide = pl.BlockSpec((1, tm, D_INNER), row)
    cls = lambda width: pl.BlockSpec((1, CLS_STRIDE, tm // CLS_STRIDE, width),
                                     lambda b, i: (b, 0, i, 0))
    return pl.pallas_call(
        _outproj_kernel,
        grid=(bsz, s // tm),
        in_specs=[
            pl.BlockSpec((1, tm, d), row),
            pl.BlockSpec((1, N_MOD, d), lambda b, i: (b, 0, 0)),
            wide, cls(D_INNER), cls(LANES),
            _const_spec((1, D_INNER)),
            _const_spec((2 * LANES, D_INNER)),
            _const_spec((2 * D_INNER, d)),
        ],
        out_specs=pl.BlockSpec((1, tm, d), row),
        out_shape=jax.ShapeDtypeStruct((bsz, s, d), F32),
        scratch_shapes=[pltpu.VMEM((d // LANES, tm, LANES), F32),
                        pltpu.VMEM((d // LANES, tm, LANES), F32),
                        pltpu.VMEM((tm, d), F32)],
        compiler_params=pltpu.CompilerParams(
            dimension_semantics=("parallel", "arbitrary"), vmem_limit_bytes=VMEM_LIMIT),
        name="out_proj",
    )(x, mod3, y_ssd, att_acc, att_stats, norm_w, e2, w_out)


FF_TN = 1024


def _mlp_kernel(x_ref, mod_ref, nw_ref, w1_ref, w2_ref, out_ref, h_ref):
    x = x_ref[0]
    h = _norm_mod(x, nw_ref[...], mod_ref[0, 4:5, :], mod_ref[0, 3:4, :])
    h_ref[...] = h.astype(BF16)
    acc = jnp.zeros(x.shape, F32)
    for j in range(D_FF // FF_TN):
        fs = slice(j * FF_TN, (j + 1) * FF_TN)
        u = jnp.maximum(_dot(h_ref[...], w1_ref[:, fs]), 0.0)
        acc = acc + _dot((u * u).astype(BF16), w2_ref[fs, :])
    out_ref[0] = x + mod_ref[0, 5:6, :] * acc


def _mlp(x, mod3, norm_w, w1, w2, tm):
    bsz, s, d = x.shape
    row = lambda b, i: (b, i, 0)
    return pl.pallas_call(
        _mlp_kernel,
        grid=(bsz, s // tm),
        in_specs=[
            pl.BlockSpec((1, tm, d), row),
            pl.BlockSpec((1, N_MOD, d), lambda b, i: (b, 0, 0)),
            _const_spec((1, d)),
            _const_spec((d, D_FF)),
            _const_spec((D_FF, d)),
        ],
        out_specs=pl.BlockSpec((1, tm, d), row),
        out_shape=jax.ShapeDtypeStruct((bsz, s, d), F32),
        scratch_shapes=[pltpu.VMEM((tm, d), BF16)],
        compiler_params=pltpu.CompilerParams(
            dimension_semantics=("parallel", "arbitrary"), vmem_limit_bytes=VMEM_LIMIT),
        name="mlp",
    )(x, mod3, norm_w, w1, w2)


def _head_expand_matrix():
    head_of_lane = jnp.arange(D_INNER) // HEAD_DIM
    e = (jnp.arange(LANES)[:, None] == head_of_lane[None, :]).astype(BF16)
    return jnp.concatenate([e, e], axis=0)


def _pad_lanes(v, n=LANES):
    return jnp.pad(v.astype(F32), (0, n - v.shape[0])).reshape(1, n)


def kernel(x, c, norm1_w, norm2_w, w_ada, b_ada, w_in, conv_w, conv_b, dt_bias, a_log, d_skip,
           ssd_norm_w, q_norm_w, k_norm_w, attn_norm_w, w_out, w_ff1, w_ff2):
    bsz, s, d = x.shape
    depth = w_ada.shape[0]
    tm = ROW_TILE
    assert d == D_MODEL and s % ROW_TILE == 0 and s % TAIL_ROW_TILE == 0 and s % SSD_ROWS == 0
    assert s % (CLS_STRIDE * ATT_BLK) == 0
    e2 = _head_expand_matrix()
    idx = jnp.arange(MXU_TILE) // HEAD_DIM
    bd = (idx[:, None] == idx[None, :]).astype(BF16)
    t = (jnp.arange(SSD_CHUNK)[:, None] >= jnp.arange(SSD_CHUNK)[None, :]).astype(BF16)
    tril2 = jnp.concatenate([t, t], axis=1)
    o_xbc = D_INNER + CONV_CH
    o_dt = o_xbc + N_HEADS

    for l in range(depth):
        mod3 = _modulation(c, w_ada[l], b_ada[l]).reshape(bsz, N_MOD, d)
        w_all, w_dt = _split_in_weight(jnp.swapaxes(w_in[l], 0, 1))
        z, xbc, dt, q16, k16, v16 = _in_projection(
            x, mod3, norm1_w[l].reshape(1, d), w_all, w_all, w_dt, _pad_lanes(dt_bias[l]),
            jnp.tile(q_norm_w[l], N_HEADS).reshape(1, D_INNER),
            jnp.tile(k_norm_w[l], N_HEADS).reshape(1, D_INNER), bd, tm)
        y_ssd = _ssd_mixer(
            xbc, z, dt, conv_w[l], conv_b[l].reshape(1, CONV_CH),
            _pad_lanes(a_log[l]), jnp.repeat(d_skip[l], HEAD_DIM).reshape(1, D_INNER),
            ssd_norm_w[l].reshape(1, D_INNER), tril2, e2)
        score_bound = (HEAD_DIM ** 0.5 * LOG2E * jnp.max(jnp.abs(q_norm_w[l]))
                       * jnp.max(jnp.abs(k_norm_w[l])))
        bounded = (score_bound * BF16_SLACK <= SCORE_BOUND).astype(jnp.int32).reshape(1)
        running = None
        for window, dilation in DILATED_PATTERNS:
            assert window // dilation == ATT_BLK
            running = _dilated_attention(bounded, q16, k16, v16, dilation, e2, running)
        x = _out_projection(x, mod3, y_ssd, *running, attn_norm_w[l].reshape(1, D_INNER),
                            e2, w_out[l].astype(BF16), TAIL_ROW_TILE)
        x = _mlp(x, mod3, norm2_w[l].reshape(1, d), w_ff1[l].astype(BF16),
                 w_ff2[l].astype(BF16), TAIL_ROW_TILE)
    return x.astype(c.dtype)
```
